```python
import jax, jax.numpy as jnp
from jax import lax
import numpy as np

D_MODEL = 2048
BATCH = 8
SEQ = 2048
DEPTH = 2

D_MIX = D_MODEL
GROUP_W = D_MIX // 4
MOBA_HEAD_DIM = 64
MOBA_HEADS = GROUP_W // MOBA_HEAD_DIM
MOBA_BLOCK = 256
MOBA_TOPK = 3
MOBA_Q_CHUNK = 32
MLSTM_HEADS = 4
MLSTM_HEAD_DIM = GROUP_W // MLSTM_HEADS
MLSTM_CHUNK = 64
MLSTM_CONV = 4
MLSTM_GATE_CAP = 15.0
SWA_HEAD_DIM = 64
SWA_HEADS = (D_MIX - 2 * GROUP_W) // SWA_HEAD_DIM
SWA_KV_HEADS = SWA_HEADS // 8
SWA_WINDOW = 128
D_FF = 5632
N_EXPERTS = 8
TOP_K = 2
D_FF_EXPERT = D_FF // 2
RMS_EPS = 1e-6

MOBA_W = MOBA_HEADS * MOBA_HEAD_DIM
MLSTM_W = MLSTM_HEADS * MLSTM_HEAD_DIM
SWA_W = SWA_HEADS * SWA_HEAD_DIM
SWA_KV_W = SWA_KV_HEADS * SWA_HEAD_DIM
IN_SPLIT = (MOBA_W, MOBA_W, MOBA_W, MLSTM_W, MLSTM_W, MLSTM_W, MLSTM_W, MLSTM_HEADS, MLSTM_HEADS, SWA_W, SWA_KV_W, SWA_KV_W)
N_IN = sum(IN_SPLIT)
N_DENSE = (DEPTH + 1) // 2
N_MOE = DEPTH // 2

kernel_name = 'hymba_moba_mlstm_swa_moe_block'


def rms_norm(x, g):
    x32 = x.astype(jnp.float32)
    y = x32 * lax.rsqrt(jnp.mean(x32 * x32, axis=-1, keepdims=True) + RMS_EPS)
    return (y * g.astype(jnp.float32)).astype(x.dtype)


def causal_depthwise_conv(x, w, b):
    K, C = w.shape
    y = lax.conv_general_dilated(x, w[:, None, :].astype(x.dtype), window_strides=(1,), padding=[(K - 1, 0)], dimension_numbers=('NWC', 'WIO', 'NWC'), feature_group_count=C)
    return y + b


def moba_attention(q, k, v):
    B, S, H, Dh = q.shape
    n_blk = -(-S // MOBA_BLOCK)
    s_pad = n_blk * MOBA_BLOCK
    k_sel = min(MOBA_TOPK, n_blk - 1)
    n_chunk = S // MOBA_Q_CHUNK
    scale = Dh ** -0.5
    pad = ((0, 0), (0, s_pad - S), (0, 0), (0, 0))
    kb = jnp.pad(k, pad).reshape(B, n_blk, MOBA_BLOCK, H, Dh).transpose(0, 3, 1, 2, 4)
    vb = jnp.pad(v, pad).reshape(B, n_blk, MOBA_BLOCK, H, Dh).transpose(0, 3, 1, 2, 4)
    q_bh = q.transpose(0, 2, 1, 3)
    q_c = q_bh.reshape(B, H, n_chunk, MOBA_Q_CHUNK, Dh).transpose(2, 0, 1, 3, 4)
    if k_sel > 0:
        k_mean = jnp.mean(kb.astype(jnp.float32), axis=3)
        gate = jnp.einsum('bhsd,bhnd->bhsn', q_bh.astype(jnp.float32), k_mean)
        fully_past = jnp.arange(n_blk)[None, :] < (jnp.arange(S) // MOBA_BLOCK)[:, None]
        gate = jnp.where(fully_past, gate, -jnp.inf)
        _, idx = lax.top_k(gate, k_sel)
    else:
        idx = jnp.zeros((B, H, S, 0), jnp.int32)
    idx_c = idx.reshape(B, H, n_chunk, MOBA_Q_CHUNK, k_sel).transpose(2, 0, 1, 3, 4)
    b_ix = jnp.arange(B)[:, None, None, None]
    h_ix = jnp.arange(H)[None, :, None, None]

    def chunk(args):
        ci, qc, ic = args
        q0 = ci * MOBA_Q_CHUNK
        own = q0 // MOBA_BLOCK
        q_pos = q0 + jnp.arange(MOBA_Q_CHUNK)
        k_pos = own * MOBA_BLOCK + jnp.arange(MOBA_BLOCK)
        k_own = lax.dynamic_index_in_dim(kb, own, axis=2, keepdims=False)
        v_own = lax.dynamic_index_in_dim(vb, own, axis=2, keepdims=False)
        s_own = jnp.einsum('bhqd,bhkd->bhqk', qc, k_own).astype(jnp.float32) * scale
        s_own = jnp.where(k_pos[None, :] <= q_pos[:, None], s_own, -jnp.inf)
        if k_sel == 0:
            p = jax.nn.softmax(s_own, axis=-1).astype(v.dtype)
            return jnp.einsum('bhqk,bhkd->bhqd', p, v_own)
        k_g = kb[b_ix, h_ix, ic]
        v_g = vb[b_ix, h_ix, ic]
        s_g = jnp.einsum('bhqd,bhqnkd->bhqnk', qc, k_g).astype(jnp.float32) * scale
        slot_ok = jnp.arange(k_sel) < own
        s_g = jnp.where(slot_ok[:, None], s_g, -jnp.inf)
        n_g = k_sel * MOBA_BLOCK
        logits = jnp.concatenate([s_g.reshape(B, H, MOBA_Q_CHUNK, n_g), s_own], axis=-1)
        p = jax.nn.softmax(logits, axis=-1).astype(v.dtype)
        p_g = p[..., :n_g].reshape(B, H, MOBA_Q_CHUNK, k_sel, MOBA_BLOCK)
        return jnp.einsum('bhqnk,bhqnkd->bhqd', p_g, v_g) + jnp.einsum('bhqk,bhkd->bhqd', p[..., n_g:], v_own)

    out = lax.map(chunk, (jnp.arange(n_chunk), q_c, idx_c))
    return out.transpose(1, 0, 3, 2, 4).reshape(B, S, H * Dh)


def mlstm_chunkwise(q, k, v, log_i, log_f):
    B, S, H, Dh = q.shape
    L = MLSTM_CHUNK
    nc = S // L
    f32 = jnp.float32

    def chunks(t):
        return t.astype(f32).reshape(B, nc, L, H, -1).transpose(0, 3, 1, 2, 4)

    qc = chunks(q)
    kc = chunks(k) * (Dh ** -0.5)
    vc = chunks(v)
    li = chunks(log_i[..., None])[..., 0]
    lf = chunks(log_f[..., None])[..., 0]
    b = jnp.cumsum(lf, axis=-1)
    b_last = b[..., -1]
    causal = jnp.tril(jnp.ones((L, L), dtype=bool))
    d_log = jnp.where(causal, b[..., :, None] - b[..., None, :] + li[..., None, :], -jnp.inf)
    a = b_last[..., None] - b + li
    a_max = jnp.max(a, axis=-1)
    w_a = jnp.exp(a - a_max[..., None])
    c_chunk = jnp.einsum('bhcs,bhcsd,bhcse->bhcde', w_a, kc, vc)
    n_chunk = jnp.einsum('bhcs,bhcsd->bhcd', w_a, kc)

    def step(carry, xs):
        c_st, n_st, m_st = carry
        c_in, n_in, a_mx, b_l = xs
        m_new = jnp.maximum(b_l + m_st, a_mx)
        decay = jnp.exp(b_l + m_st - m_new)
        inject = jnp.exp(a_mx - m_new)
        c_new = decay[..., None, None] * c_st + inject[..., None, None] * c_in
        n_new = decay[..., None] * n_st + inject[..., None] * n_in
        return (c_new, n_new, m_new), (c_st, n_st, m_st)

    init = (jnp.zeros((B, H, Dh, Dh), f32), jnp.zeros((B, H, Dh), f32), jnp.zeros((B, H), f32))
    xs = (jnp.moveaxis(c_chunk, 2, 0), jnp.moveaxis(n_chunk, 2, 0), jnp.moveaxis(a_max, 2, 0), jnp.moveaxis(b_last, 2, 0))
    _, (c_prev, n_prev, m_prev) = lax.scan(step, init, xs)
    c_prev = jnp.moveaxis(c_prev, 0, 2)
    n_prev = jnp.moveaxis(n_prev, 0, 2)
    m_prev = jnp.moveaxis(m_prev, 0, 2)
    inter_log = b + m_prev[..., None]
    m_out = jnp.maximum(inter_log, jnp.max(d_log, axis=-1))
    w_inter = jnp.exp(inter_log - m_out)
    qk = jnp.einsum('bhctd,bhcsd->bhcts', qc, kc) * jnp.exp(d_log - m_out[..., None])
    num = jnp.einsum('bhcts,bhcse->bhcte', qk, vc) + w_inter[..., None] * jnp.einsum('bhctd,bhcde->bhcte', qc, c_prev)
    den = jnp.sum(qk, axis=-1) + w_inter * jnp.einsum('bhctd,bhcd->bhct', qc, n_prev)
    h = num / jnp.maximum(jnp.abs(den), jnp.exp(-m_out))[..., None]
    return h.transpose(0, 2, 3, 1, 4).reshape(B, S, H, Dh)


def swa_sink_attention(q, k, v, sinks):
    B, S, Hq, Dh = q.shape
    Hkv = k.shape[2]
    G = Hq // Hkv
    W = SWA_WINDOW
    nb = S // W
    scale = Dh ** -0.5
    qb = q.reshape(B, nb, W, Hkv, G, Dh)
    pad = ((0, 0), (W, 0), (0, 0), (0, 0))
    kp = jnp.pad(k, pad).reshape(B, nb + 1, W, Hkv, Dh)
    vp = jnp.pad(v, pad).reshape(B, nb + 1, W, Hkv, Dh)
    k_band = jnp.concatenate([kp[:, :-1], kp[:, 1:]], axis=2)
    v_band = jnp.concatenate([vp[:, :-1], vp[:, 1:]], axis=2)
    s = jnp.einsum('bnqkgd,bnskd->bnkgqs', qb, k_band).astype(jnp.float32) * scale
    qi = jnp.arange(W)[:, None]
    si = jnp.arange(2 * W)[None, :]
    k_abs = jnp.arange(nb)[:, None, None] * W - W + si
    allowed = (si > qi) & (si <= qi + W) & (k_abs >= 0)
    s = jnp.where(allowed[None, :, None, None], s, -jnp.inf)
    sink = jnp.broadcast_to(sinks.astype(jnp.float32).reshape(1, 1, Hkv, G, 1, 1), s.shape[:-1] + (1,))
    p = jax.nn.softmax(jnp.concatenate([s, sink], axis=-1), axis=-1)[..., :-1].astype(v.dtype)
    o = jnp.einsum('bnkgqs,bnskd->bnqkgd', p, v_band)
    return o.reshape(B, S, Hq * Dh)


def token_mixer(h, w_in, w_out, conv_w, conv_b, igate_b, fgate_b, mlstm_norm_w, swa_sinks):
    B, S, _ = h.shape
    proj = h @ w_in
    offs = []
    o = 0
    for w in IN_SPLIT[:-1]:
        o += w
        offs.append(o)
    mb_q, mb_k, mb_v, ml_q, ml_k, ml_v, ml_o, ml_i, ml_f, sw_q, sw_k, sw_v = jnp.split(proj, offs, axis=-1)
    hd = (B, S, MOBA_HEADS, MOBA_HEAD_DIM)
    y_moba = moba_attention(mb_q.reshape(hd), mb_k.reshape(hd), mb_v.reshape(hd))
    qk = jax.nn.silu(causal_depthwise_conv(jnp.concatenate([ml_q, ml_k], axis=-1), conv_w, conv_b))
    ml_q, ml_k = jnp.split(qk, 2, axis=-1)
    cap = MLSTM_GATE_CAP
    log_i = cap * jnp.tanh((ml_i + igate_b).astype(jnp.float32) / cap)
    log_f = jax.nn.log_sigmoid(cap * jnp.tanh((ml_f + fgate_b).astype(jnp.float32) / cap))
    hm_shape = (B, S, MLSTM_HEADS, MLSTM_HEAD_DIM)
    hm = mlstm_chunkwise(ml_q.reshape(hm_shape), ml_k.reshape(hm_shape), ml_v.reshape(hm_shape), log_i, log_f)
    hm = rms_norm(hm, mlstm_norm_w.reshape(MLSTM_HEADS, MLSTM_HEAD_DIM)).astype(h.dtype)
    y_mlstm = jax.nn.sigmoid(ml_o) * hm.reshape(B, S, MLSTM_W)
    y_swa = swa_sink_attention(sw_q.reshape(B, S, SWA_HEADS, SWA_HEAD_DIM), sw_k.reshape(B, S, SWA_KV_HEADS, SWA_HEAD_DIM), sw_v.reshape(B, S, SWA_KV_HEADS, SWA_HEAD_DIM), swa_sinks)
    return jnp.concatenate([y_moba, y_mlstm, y_swa], axis=-1) @ w_out


def swiglu(h, w_gate, w_up, w_down):
    return (jax.nn.silu(h @ w_gate) * (h @ w_up)) @ w_down


def moe_swiglu(h, w_router, w_gate, w_up, w_down):
    B, S, D = h.shape
    xt = h.reshape(B * S, D)
    logits = (xt @ w_router).astype(jnp.float32)
    top_v, top_i = lax.top_k(logits, TOP_K)
    top_w = jax.nn.softmax(top_v, axis=-1)
    combine = jnp.sum(jax.nn.one_hot(top_i, N_EXPERTS, dtype=jnp.float32) * top_w[..., None], axis=1).astype(h.dtype)
    out = jnp.zeros_like(xt)
    for e in range(N_EXPERTS):
        out = out + combine[:, e:e + 1] * swiglu(xt, w_gate[e], w_up[e], w_down[e])
    return out.reshape(B, S, D)


def setup_inputs(seed: int = 0) -> dict:
    key = jax.random.key(seed)
    ks = jax.random.split(key, 24)
    f32 = jnp.float32
    D = D_MODEL

    def nrm(k, shape, std):
        return jax.random.normal(k, shape, f32) * std

    return {
        'x': nrm(ks[0], (BATCH, SEQ, D), 1.0),
        'c': nrm(ks[1], (BATCH, D), 1.0),
        'ada_w': nrm(ks[2], (DEPTH, D, 6 * D), D ** -0.5),
        'ada_b': nrm(ks[3], (DEPTH, 6 * D), 0.02),
        'g_pre_mix': 1.0 + nrm(ks[4], (DEPTH, D), 0.02),
        'g_post_mix': 1.0 + nrm(ks[5], (DEPTH, D), 0.02),
        'g_pre_ffn': 1.0 + nrm(ks[6], (DEPTH, D), 0.02),
        'g_post_ffn': 1.0 + nrm(ks[7], (DEPTH, D), 0.02),
        'w_in': nrm(ks[8], (DEPTH, D, N_IN), D ** -0.5),
        'w_out': nrm(ks[9], (DEPTH, D_MIX, D), D_MIX ** -0.5),
        'conv_w': nrm(ks[10], (DEPTH, MLSTM_CONV, 2 * MLSTM_W), MLSTM_CONV ** -0.5),
        'conv_b': nrm(ks[11], (DEPTH, 2 * MLSTM_W), 0.02),
        'igate_b': nrm(ks[12], (DEPTH, MLSTM_HEADS), 0.1),
        'fgate_b': 3.0 + nrm(ks[13], (DEPTH, MLSTM_HEADS), 0.5),
        'mlstm_norm_w': 1.0 + nrm(ks[14], (DEPTH, MLSTM_W), 0.02),
        'swa_sinks': nrm(ks[15], (DEPTH, SWA_HEADS), 0.5),
        'ffn_w_gate': nrm(ks[16], (N_DENSE, D, D_FF), D ** -0.5),
        'ffn_w_up': nrm(ks[17], (N_DENSE, D, D_FF), D ** -0.5),
        'ffn_w_down': nrm(ks[18], (N_DENSE, D_FF, D), D_FF ** -0.5),
        'moe_router': nrm(ks[19], (N_MOE, D, N_EXPERTS), D ** -0.5),
        'moe_w_gate': nrm(ks[20], (N_MOE, N_EXPERTS, D, D_FF_EXPERT), D ** -0.5),
        'moe_w_up': nrm(ks[21], (N_MOE, N_EXPERTS, D, D_FF_EXPERT), D ** -0.5),
        'moe_w_down': nrm(ks[22], (N_MOE, N_EXPERTS, D_FF_EXPERT, D), D_FF_EXPERT ** -0.5),
    }


def reference(x, c, ada_w, ada_b, g_pre_mix, g_post_mix, g_pre_ffn, g_post_ffn, w_in, w_out, conv_w, conv_b, igate_b, fgate_b, mlstm_norm_w, swa_sinks, ffn_w_gate, ffn_w_up, ffn_w_down, moe_router, moe_w_gate, moe_w_up, moe_w_down):
    cond = jax.nn.silu(c)
    for l in range(DEPTH):
        mod = (cond @ ada_w[l] + ada_b[l])[:, None, :]
        sh_m, sc_m, gt_m, sh_f, sc_f, gt_f = jnp.split(mod, 6, axis=-1)
        h = rms_norm(x, g_pre_mix[l]) * (1.0 + sc_m) + sh_m
        y = token_mixer(h, w_in[l], w_out[l], conv_w[l], conv_b[l], igate_b[l], fgate_b[l], mlstm_norm_w[l], swa_sinks[l])
        x = x + gt_m * rms_norm(y, g_post_mix[l])
        h = rms_norm(x, g_pre_ffn[l]) * (1.0 + sc_f) + sh_f
        j = l // 2
        if l % 2 == 0:
            y = swiglu(h, ffn_w_gate[j], ffn_w_up[j], ffn_w_down[j])
        else:
            y = moe_swiglu(h, moe_router[j], moe_w_gate[j], moe_w_up[j], moe_w_down[j])
        x = x + gt_f * rms_norm(y, g_post_ffn[l])
    return x
```

```python
import functools

import jax
import jax.numpy as jnp
from jax import lax
from jax.experimental import pallas as pl
from jax.experimental.pallas import tpu as pltpu

F32 = jnp.float32
BF16 = jnp.bfloat16

LANES = 128
HEAD_W = 64
MOBA_BLOCK = 256
MOBA_TOPK = 3
MOBA_PAIRS = 4
MLSTM_HEADS = 4
MLSTM_CHUNK = 128
MLSTM_CONV = 4
MLSTM_GATE_CAP = 15.0
SWA_WINDOW = 128
SWA_GROUPS = 2
SWA_GROUP_HEADS = 8
N_EXPERTS = 8
RMS_EPS = 1e-6
NEG_INF = float("-inf")
VMEM_LIMIT = 56 * 1024 * 1024

PB_SWA_Q = 0
PB_MOBA_Q = 8
PB_MOBA_K = 12
PB_MOBA_V = 16
PB_ML_Q = 20
PB_ML_K = 24
PB_ML_V = 28
PB_ML_O = 32
PB_ML_G = 36
PB_SWA_K = 38
PB_SWA_V = 40
PROJ_BLOCKS = 42


def _params(sem):
    return pltpu.CompilerParams(dimension_semantics=sem, vmem_limit_bytes=VMEM_LIMIT)


def _rms(x, g):
    return x * lax.rsqrt(jnp.mean(x * x, axis=-1, keepdims=True) + RMS_EPS) * g


def _dot(a, b):
    return jnp.dot(a, b, preferred_element_type=F32)


def _dot_nt(a, b):
    return lax.dot_general(a, b, (((1,), (1,)), ((), ())), preferred_element_type=F32)


def _adaln_kernel(c_ref, w_ref, b_ref, o_ref):
    c = c_ref[...]
    cond = (c * jax.nn.sigmoid(c)).astype(BF16)
    o_ref[...] = _dot(cond, w_ref[...].astype(BF16)) + b_ref[...]


def _adaln(c, ada_w, ada_b):
    depth, d, n6 = ada_w.shape
    b = c.shape[0]
    tn = 1024
    return pl.pallas_call(
        _adaln_kernel,
        grid=(depth, n6 // tn),
        in_specs=[
            pl.BlockSpec((b, d), lambda l, j: (0, 0)),
            pl.BlockSpec((None, d, tn), lambda l, j: (l, 0, j)),
            pl.BlockSpec((None, 1, tn), lambda l, j: (l, 0, j)),
        ],
        out_specs=pl.BlockSpec((None, b, tn), lambda l, j: (l, 0, j)),
        out_shape=jax.ShapeDtypeStruct((depth, b, n6), F32),
        compiler_params=_params(("arbitrary", "arbitrary")),
        name="adaln",
    )(c, ada_w, ada_b.reshape(depth, 1, n6))


def _inproj_kernel(x_ref, g_ref, sc_ref, sh_ref, w_ref, o_ref, h_scr):
    @pl.when(pl.program_id(1) == 0)
    def _():
        h = _rms(x_ref[...], g_ref[...]) * (1.0 + sc_ref[...]) + sh_ref[...]
        h_scr[...] = h.astype(BF16)

    o_ref[...] = _dot(h_scr[...], w_ref[...])


def _inproj(x2, g, mod, w, seq):
    n, d = x2.shape
    nc = w.shape[1]
    tm, tn = 512, 768
    per_b = seq // tm
    return pl.pallas_call(
        _inproj_kernel,
        grid=(n // tm, nc // tn),
        in_specs=[
            pl.BlockSpec((tm, d), lambda i, j: (i, 0)),
            pl.BlockSpec((1, d), lambda i, j: (0, 0)),
            pl.BlockSpec((None, None, 1, d), lambda i, j: (i // per_b, 1, 0, 0)),
            pl.BlockSpec((None, None, 1, d), lambda i, j: (i // per_b, 0, 0, 0)),
            pl.BlockSpec((d, tn), lambda i, j: (0, j)),
        ],
        out_specs=pl.BlockSpec((tm, tn), lambda i, j: (i, j)),
        out_shape=jax.ShapeDtypeStruct((n, nc), F32),
        scratch_shapes=[pltpu.VMEM((tm, d), BF16)],
        compiler_params=_params(("arbitrary", "arbitrary")),
        name="inproj",
    )(x2, g, mod, mod, w)


def _moba_kernel(q_ref, k_ref, v_ref, o_ref, kb_scr, vb_scr, km_scr):
    i = pl.program_id(2)
    seq = k_ref.shape[0]
    n_blk = seq // MOBA_BLOCK
    tq = q_ref.shape[0]

    @pl.when(i == 0)
    def _():
        k = k_ref[...]
        kb_scr[...] = k.astype(BF16)
        vb_scr[...] = v_ref[...].astype(BF16)
        km_scr[...] = jnp.mean(k.reshape(n_blk, MOBA_BLOCK, LANES), axis=1)

    q = q_ref[...]
    lane = lax.broadcasted_iota(jnp.int32, (1, LANES), 1)
    kpos = lax.broadcasted_iota(jnp.int32, (1, seq), 1)
    kblk = lax.shift_right_logical(kpos, 8)
    qpos = i * tq + lax.broadcasted_iota(jnp.int32, (tq, 1), 0)
    causal_own = (kpos <= qpos) & (kblk == i)
    col = lax.broadcasted_iota(jnp.int32, (tq, n_blk), 1)
    valid = col < i
    expand = (kblk == lax.broadcasted_iota(jnp.int32, (n_blk, 1), 0)).astype(BF16)
    kmean = km_scr[...].astype(BF16)
    scale = HEAD_W ** -0.5
    outs = []
    for hh in range(2):
        in_head = (lane >= HEAD_W * hh) & (lane < HEAD_W * (hh + 1))
        qm = jnp.where(in_head, q, 0.0).astype(BF16)
        gate = jnp.where(valid, _dot_nt(qm, kmean), NEG_INF)
        beaten_by = jnp.zeros((tq, n_blk), jnp.int32)
        for j in range(n_blk):
            gj = gate[:, j:j + 1]
            beats = (gj > gate) | ((gj == gate) & (col > j))
            beaten_by = beaten_by + beats.astype(jnp.int32)
        sel = jnp.where(valid & (beaten_by < MOBA_TOPK), 1.0, 0.0).astype(BF16)
        allowed = (_dot(sel, expand) > 0.5) | causal_own
        s = jnp.where(allowed, _dot_nt(qm, kb_scr[...]) * scale, NEG_INF)
        m = jnp.max(s, axis=-1, keepdims=True)
        p = jnp.exp(s - m)
        l = jnp.sum(p, axis=-1, keepdims=True)
        outs.append(_dot(p.astype(BF16), vb_scr[...]) / l)
    o_ref[...] = jnp.where(lane < HEAD_W, outs[0], outs[1]).astype(o_ref.dtype)


def _moba(proj, batch, seq):
    n = proj.shape[0]
    n_blk = seq // MOBA_BLOCK
    tq = MOBA_BLOCK
    return pl.pallas_call(
        _moba_kernel,
        grid=(batch, MOBA_PAIRS, n_blk),
        in_specs=[
            pl.BlockSpec((tq, LANES), lambda b, p, i: (b * n_blk + i, PB_MOBA_Q + p)),
            pl.BlockSpec((seq, LANES), lambda b, p, i: (b, PB_MOBA_K + p)),
            pl.BlockSpec((seq, LANES), lambda b, p, i: (b, PB_MOBA_V + p)),
        ],
        out_specs=pl.BlockSpec((tq, LANES), lambda b, p, i: (b * n_blk + i, p)),
        out_shape=jax.ShapeDtypeStruct((n, MOBA_PAIRS * LANES), BF16),
        scratch_shapes=[
            pltpu.VMEM((seq, LANES), BF16),
            pltpu.VMEM((seq, LANES), BF16),
            pltpu.VMEM((n_blk, LANES), F32),
        ],
        compiler_params=_params(("arbitrary", "arbitrary", "arbitrary")),
        name="moba",
    )(proj, proj, proj)


def _causal_conv_silu(x, w, b):
    seq = x.shape[0]
    row = lax.broadcasted_iota(jnp.int32, (seq, 1), 0)
    y = b + w[MLSTM_CONV - 1:MLSTM_CONV, :] * x
    for shift in range(1, MLSTM_CONV):
        xs = jnp.where(row >= shift, pltpu.roll(x, shift, axis=0), 0.0)
        y = y + w[MLSTM_CONV - 1 - shift:MLSTM_CONV - shift, :] * xs
    return y * jax.nn.sigmoid(y)


def _mlstm_kernel(q_ref, k_ref, v_ref, og_ref, gate_ref, cwq_ref, cwk_ref, cbq_ref, cbk_ref,
                  gb_ref, nw_ref, o_ref, qc_scr, kc_scr, vb_scr, acol_scr, bcol_scr,
                  arow_scr, brow_scr):
    h = pl.program_id(1)
    seq = q_ref.shape[0]
    L = MLSTM_CHUNK
    n_chunk = seq // L
    lane = lax.broadcasted_iota(jnp.int32, (1, LANES), 1)
    ri = lax.broadcasted_iota(jnp.int32, (L, L), 0)
    ci = lax.broadcasted_iota(jnp.int32, (L, L), 1)
    causal = ri >= ci

    @pl.when(h == 0)
    def _():
        t = MLSTM_GATE_CAP * jnp.tanh((gate_ref[...] + gb_ref[...]) / MLSTM_GATE_CAP)
        a = jnp.where(lane < MLSTM_HEADS, t, jax.nn.log_sigmoid(t))
        acol_scr[...] = a
        arow_scr[...] = a.T[0:8, :]
        lower = causal.astype(F32)
        upper = (ri <= ci).astype(F32)
        for c in range(n_chunk):
            rows = slice(c * L, (c + 1) * L)
            bcol_scr[rows, :] = jnp.dot(lower, acol_scr[rows, :], precision=lax.Precision.HIGHEST,
                                        preferred_element_type=F32)
            brow_scr[:, rows] = jnp.dot(arow_scr[:, rows], upper, precision=lax.Precision.HIGHEST,
                                        preferred_element_type=F32)

    qc_scr[...] = _causal_conv_silu(q_ref[...], cwq_ref[...], cbq_ref[...]).astype(BF16)
    kc = _causal_conv_silu(k_ref[...], cwk_ref[...], cbk_ref[...]) * (LANES ** -0.5)
    kc_scr[...] = kc.astype(BF16)
    vb_scr[...] = v_ref[...].astype(BF16)
    norm_w = nw_ref[...]
    sel_i = lane == h
    sel_f = lane == h + MLSTM_HEADS
    sub = lax.broadcasted_iota(jnp.int32, (8, 1), 0)

    def chunk(c, carry):
        c_st, n_st, m_st = carry
        r0 = pl.multiple_of(c * L, L)
        q = qc_scr[pl.ds(r0, L), :]
        k = kc_scr[pl.ds(r0, L), :]
        v = vb_scr[pl.ds(r0, L), :]
        b_col = jnp.sum(jnp.where(sel_f, bcol_scr[pl.ds(r0, L), :], 0.0), axis=1, keepdims=True)
        li_col = jnp.sum(jnp.where(sel_i, acol_scr[pl.ds(r0, L), :], 0.0), axis=1, keepdims=True)
        b_row = jnp.sum(jnp.where(sub == h + MLSTM_HEADS, brow_scr[:, pl.ds(r0, L)], 0.0),
                        axis=0, keepdims=True)
        li_row = jnp.sum(jnp.where(sub == h, arow_scr[:, pl.ds(r0, L)], 0.0), axis=0, keepdims=True)
        b_last = b_row[:, L - 1:L]
        d_log = jnp.where(causal, b_col - b_row + li_row, NEG_INF)
        inter_log = b_col + m_st
        m_out = jnp.maximum(inter_log, jnp.max(d_log, axis=1, keepdims=True))
        w_inter = jnp.exp(inter_log - m_out)
        qk = _dot_nt(q, k) * jnp.exp(d_log - m_out)
        num = _dot(qk.astype(BF16), v) + w_inter * _dot(q, c_st.astype(BF16))
        den = jnp.sum(qk, axis=1, keepdims=True) + w_inter * jnp.sum(
            q.astype(F32) * n_st, axis=1, keepdims=True)
        hv = num / jnp.maximum(jnp.abs(den), jnp.exp(-m_out))
        hn = _rms(hv, norm_w)
        o_ref[pl.ds(r0, L), :] = (jax.nn.sigmoid(og_ref[pl.ds(r0, L), :]) * hn).astype(o_ref.dtype)
        a_max = jnp.max(b_last - b_row + li_row, axis=1, keepdims=True)
        kw = k.astype(F32) * jnp.exp(b_last - b_col + li_col - a_max)
        c_in = _dot(kw.T.astype(BF16), v)
        n_in = jnp.sum(kw, axis=0, keepdims=True)
        m_new = jnp.maximum(b_last + m_st, a_max)
        decay = jnp.exp(b_last + m_st - m_new)
        inject = jnp.exp(a_max - m_new)
        return decay * c_st + inject * c_in, decay * n_st + inject * n_in, m_new

    init = (jnp.zeros((LANES, LANES), F32), jnp.zeros((1, LANES), F32), jnp.zeros((1, 1), F32))
    lax.fori_loop(0, n_chunk, chunk, init)


def _mlstm(proj, conv_w, conv_b, gate_bias, norm_w, batch, seq):
    n = proj.shape[0]
    hw = MLSTM_HEADS

    def col(base):
        return pl.BlockSpec((seq, LANES), lambda b, h: (b, base + h))

    return pl.pallas_call(
        _mlstm_kernel,
        grid=(batch, hw),
        in_specs=[
            col(PB_ML_Q), col(PB_ML_K), col(PB_ML_V), col(PB_ML_O),
            pl.BlockSpec((seq, LANES), lambda b, h: (b, PB_ML_G)),
            pl.BlockSpec((MLSTM_CONV, LANES), lambda b, h: (0, h)),
            pl.BlockSpec((MLSTM_CONV, LANES), lambda b, h: (0, hw + h)),
            pl.BlockSpec((1, LANES), lambda b, h: (0, h)),
            pl.BlockSpec((1, LANES), lambda b, h: (0, hw + h)),
            pl.BlockSpec((1, LANES), lambda b, h: (0, 0)),
            pl.BlockSpec((1, LANES), lambda b, h: (0, h)),
        ],
        out_specs=pl.BlockSpec((seq, LANES), lambda b, h: (b, h)),
        out_shape=jax.ShapeDtypeStruct((n, hw * LANES), BF16),
        scratch_shapes=[
            pltpu.VMEM((seq, LANES), BF16),
            pltpu.VMEM((seq, LANES), BF16),
            pltpu.VMEM((seq, LANES), BF16),
            pltpu.VMEM((seq, LANES), F32),
            pltpu.VMEM((seq, LANES), F32),
            pltpu.VMEM((8, seq), F32),
            pltpu.VMEM((8, seq), F32),
        ],
        compiler_params=_params(("arbitrary", "arbitrary")),
        name="mlstm",
    )(proj, proj, proj, proj, proj, conv_w, conv_w, conv_b, conv_b, gate_bias, norm_w)


def _swa_kernel(sink_ref, q_ref, k_ref, v_ref, o_ref, kb_scr, vb_scr):
    g = pl.program_id(1)
    seq = q_ref.shape[0]
    W = SWA_WINDOW
    n_pairs = SWA_GROUP_HEADS // 2
    kb_scr[...] = k_ref[...].astype(BF16)
    vb_scr[...] = v_ref[...].astype(BF16)
    lane = lax.broadcasted_iota(jnp.int32, (1, LANES), 1)
    low = lane < HEAD_W
    scale = HEAD_W ** -0.5
    sinks = [sink_ref[g * SWA_GROUP_HEADS + hd] for hd in range(SWA_GROUP_HEADS)]

    def block(nb, carry):
        start = pl.multiple_of(jnp.maximum(nb - 1, 0) * W, W)
        r0 = pl.multiple_of(nb * W, W)
        qn = q_ref[pl.ds(r0, W), :]
        kband = kb_scr[pl.ds(start, 2 * W), :]
        vband = vb_scr[pl.ds(start, 2 * W), :]
        parts = []
        for p in range(n_pairs):
            qp = qn[:, p * LANES:(p + 1) * LANES]
            parts.append(jnp.where(low, qp, 0.0).astype(BF16))
            parts.append(jnp.where(low, 0.0, qp).astype(BF16))
        s_all = _dot_nt(jnp.concatenate(parts, axis=0), kband) * scale
        kabs = start + lax.broadcasted_iota(jnp.int32, (1, 2 * W), 1)
        qabs = r0 + lax.broadcasted_iota(jnp.int32, (W, 1), 0)
        allowed = (kabs <= qabs) & (kabs > qabs - W)
        probs = []
        for hd in range(SWA_GROUP_HEADS):
            s = jnp.where(allowed, s_all[hd * W:(hd + 1) * W, :], NEG_INF)
            m = jnp.maximum(jnp.max(s, axis=-1, keepdims=True), sinks[hd])
            e = jnp.exp(s - m)
            l = jnp.sum(e, axis=-1, keepdims=True) + jnp.exp(sinks[hd] - m)
            probs.append((e / l).astype(BF16))
        o_all = _dot(jnp.concatenate(probs, axis=0), vband)
        outs = []
        for p in range(n_pairs):
            lo = o_all[(2 * p) * W:(2 * p + 1) * W, :]
            hi = o_all[(2 * p + 1) * W:(2 * p + 2) * W, :]
            outs.append(jnp.where(low, lo, hi))
        o_ref[pl.ds(r0, W), :] = jnp.concatenate(outs, axis=1).astype(o_ref.dtype)
        return carry

    lax.fori_loop(0, seq // W, block, 0)


def _swa(proj, sinks, batch, seq):
    n = proj.shape[0]
    gw = SWA_GROUP_HEADS * HEAD_W
    gb = gw // LANES
    return pl.pallas_call(
        _swa_kernel,
        grid=(batch, SWA_GROUPS),
        in_specs=[
            pl.BlockSpec(memory_space=pltpu.SMEM),
            pl.BlockSpec((seq, gw), lambda b, g: (b, PB_SWA_Q // gb + g)),
            pl.BlockSpec((seq, LANES), lambda b, g: (b, PB_SWA_K + g)),
            pl.BlockSpec((seq, LANES), lambda b, g: (b, PB_SWA_V + g)),
        ],
        out_specs=pl.BlockSpec((seq, gw), lambda b, g: (b, g)),
        out_shape=jax.ShapeDtypeStruct((n, SWA_GROUPS * gw), BF16),
        scratch_shapes=[pltpu.VMEM((seq, LANES), BF16), pltpu.VMEM((seq, LANES), BF16)],
        compiler_params=_params(("arbitrary", "arbitrary")),
        name="swa",
    )(sinks, proj, proj, proj)


def _outproj_kernel(*refs, with_router):
    if with_router:
        (ya_ref, yb_ref, yc_ref, w_ref, x_ref, gpost_ref, gt_ref, gpre_ref, sc_ref, sh_ref,
         wr_ref, xo_ref, h_ref, comb_ref) = refs
    else:
        (ya_ref, yb_ref, yc_ref, w_ref, x_ref, gpost_ref, gt_ref, gpre_ref, sc_ref, sh_ref,
         xo_ref, h_ref) = refs
    wa = ya_ref.shape[1]
    wb = yb_ref.shape[1]
    y = _dot(ya_ref[...], w_ref[0:wa, :])
    y = y + _dot(yb_ref[...], w_ref[wa:wa + wb, :])
    y = y + _dot(yc_ref[...], w_ref[wa + wb:, :])
    xn = x_ref[...] + gt_ref[...] * _rms(y, gpost_ref[...])
    xo_ref[...] = xn
    hb = (_rms(xn, gpre_ref[...]) * (1.0 + sc_ref[...]) + sh_ref[...]).astype(BF16)
    h_ref[...] = hb
    if with_router:
        lane = lax.broadcasted_iota(jnp.int32, (1, LANES), 1)
        logits = jnp.where(lane < N_EXPERTS, _dot(hb, wr_ref[...]), NEG_INF)
        m1 = jnp.max(logits, axis=-1, keepdims=True)
        i1 = jnp.min(jnp.where(logits == m1, lane, LANES), axis=-1, keepdims=True)
        rest = jnp.where(lane == i1, NEG_INF, logits)
        m2 = jnp.max(rest, axis=-1, keepdims=True)
        i2 = jnp.min(jnp.where(rest == m2, lane, LANES), axis=-1, keepdims=True)
        e2 = jnp.exp(m2 - m1)
        w1 = 1.0 / (1.0 + e2)
        comb_ref[...] = jnp.where(lane == i1, w1, 0.0) + jnp.where(lane == i2, e2 * w1, 0.0)


def _outproj(ya, yb, yc, w, x2, gpost, gpre, mod, seq, w_router=None):
    n, d = x2.shape
    tm = 256
    per_b = seq // tm
    with_router = w_router is not None

    def rows(width):
        return pl.BlockSpec((tm, width), lambda i: (i, 0))

    def vec():
        return pl.BlockSpec((1, d), lambda i: (0, 0))

    def modrow(k):
        return pl.BlockSpec((None, None, 1, d), lambda i: (i // per_b, k, 0, 0))

    in_specs = [rows(ya.shape[1]), rows(yb.shape[1]), rows(yc.shape[1]),
                pl.BlockSpec((d, d), lambda i: (0, 0)), rows(d),
                vec(), modrow(2), vec(), modrow(4), modrow(3)]
    args = [ya, yb, yc, w, x2, gpost, mod, gpre, mod, mod]
    out_specs = [rows(d), rows(d)]
    out_shape = [jax.ShapeDtypeStruct((n, d), F32), jax.ShapeDtypeStruct((n, d), BF16)]
    if with_router:
        in_specs.append(pl.BlockSpec((d, LANES), lambda i: (0, 0)))
        args.append(w_router)
        out_specs.append(rows(LANES))
        out_shape.append(jax.ShapeDtypeStruct((n, LANES), F32))
    return pl.pallas_call(
        functools.partial(_outproj_kernel, with_router=with_router),
        grid=(n // tm,),
        in_specs=in_specs,
        out_specs=out_specs,
        out_shape=out_shape,
        compiler_params=_params(("arbitrary",)),
        name="outproj",
    )(*args)


def _ffn_kernel(*refs, n_expert):
    if n_expert:
        h_ref, comb_ref, wg_ref, wu_ref, wd_ref, x_ref, gpost_ref, gt_ref, o_ref, acc = refs
        e, f = pl.program_id(1), pl.program_id(2)
        first = (e == 0) & (f == 0)
        last = (e == pl.num_programs(1) - 1) & (f == pl.num_programs(2) - 1)
    else:
        h_ref, wg_ref, wu_ref, wd_ref, x_ref, gpost_ref, gt_ref, o_ref, acc = refs
        f = pl.program_id(1)
        first = f == 0
        last = f == pl.num_programs(1) - 1

    @pl.when(first)
    def _():
        acc[...] = jnp.zeros_like(acc)

    h = h_ref[...]
    gate = _dot(h, wg_ref[...])
    a = gate * jax.nn.sigmoid(gate) * _dot(h, wu_ref[...])
    if n_expert:
        lane = lax.broadcasted_iota(jnp.int32, (1, LANES), 1)
        a = a * jnp.sum(jnp.where(lane == e, comb_ref[...], 0.0), axis=-1, keepdims=True)
    acc[...] += _dot(a.astype(BF16), wd_ref[...])

    @pl.when(last)
    def _():
        o_ref[...] = x_ref[...] + gt_ref[...] * _rms(acc[...], gpost_ref[...])


def _ffn(h, wg, wu, wd, x2, gpost, mod, seq, comb=None):
    n, d = x2.shape
    tm = 512
    per_b = seq // tm
    if comb is None:
        n_expert = 0
        dff = wg.shape[1]
        tf = 512
        grid = (n // tm, dff // tf)
        row = lambda i, f: (i, 0)
        in_specs = [
            pl.BlockSpec((tm, d), row),
            pl.BlockSpec((d, tf), lambda i, f: (0, f)),
            pl.BlockSpec((d, tf), lambda i, f: (0, f)),
            pl.BlockSpec((tf, d), lambda i, f: (f, 0)),
            pl.BlockSpec((tm, d), row),
            pl.BlockSpec((1, d), lambda i, f: (0, 0)),
            pl.BlockSpec((None, None, 1, d), lambda i, f: (i // per_b, 5, 0, 0)),
        ]
        args = (h, wg, wu, wd, x2, gpost, mod)
        sem = ("arbitrary", "arbitrary")
    else:
        n_expert, _, dff = wg.shape
        tf = 256
        grid = (n // tm, n_expert, dff // tf)
        row = lambda i, e, f: (i, 0)
        in_specs = [
            pl.BlockSpec((tm, d), row),
            pl.BlockSpec((tm, LANES), row),
            pl.BlockSpec((None, d, tf), lambda i, e, f: (e, 0, f)),
            pl.BlockSpec((None, d, tf), lambda i, e, f: (e, 0, f)),
            pl.BlockSpec((None, tf, d), lambda i, e, f: (e, f, 0)),
            pl.BlockSpec((tm, d), row),
            pl.BlockSpec((1, d), lambda i, e, f: (0, 0)),
            pl.BlockSpec((None, None, 1, d), lambda i, e, f: (i // per_b, 5, 0, 0)),
        ]
        args = (h, comb, wg, wu, wd, x2, gpost, mod)
        sem = ("arbitrary", "arbitrary", "arbitrary")
    return pl.pallas_call(
        functools.partial(_ffn_kernel, n_expert=n_expert),
        grid=grid,
        in_specs=in_specs,
        out_specs=pl.BlockSpec((tm, d), row),
        out_shape=jax.ShapeDtypeStruct((n, d), F32),
        scratch_shapes=[pltpu.VMEM((tm, d), F32)],
        compiler_params=_params(sem),
        name="moe_ffn" if n_expert else "ffn",
    )(*args)


def _pack_w_in(w_in):
    d = w_in.shape[0]
    gw = 4 * LANES
    offs = [0]
    for width in (gw, gw, gw, gw, gw, gw, gw, MLSTM_HEADS, MLSTM_HEADS,
                  2 * SWA_GROUP_HEADS * HEAD_W, SWA_GROUPS * HEAD_W, SWA_GROUPS * HEAD_W):
        offs.append(offs[-1] + width)
    (mb_q, mb_k, mb_v, ml_q, ml_k, ml_v, ml_o, ml_i, ml_f, sw_q, sw_k, sw_v) = [
        w_in[:, offs[j]:offs[j + 1]] for j in range(12)]
    gates = jnp.concatenate(
        [ml_i, ml_f, jnp.zeros((d, 2 * LANES - 2 * MLSTM_HEADS), w_in.dtype)], axis=1)

    def dup(w):
        heads = [w[:, g * HEAD_W:(g + 1) * HEAD_W] for g in range(SWA_GROUPS)]
        return jnp.concatenate([t for hd in heads for t in (hd, hd)], axis=1)

    packed = jnp.concatenate(
        [sw_q, mb_q, mb_k, mb_v, ml_q, ml_k, ml_v, ml_o, gates, dup(sw_k), dup(sw_v)], axis=1)
    assert packed.shape[1] == PROJ_BLOCKS * LANES
    return packed.astype(BF16)


def kernel(x, c, ada_w, ada_b, g_pre_mix, g_post_mix, g_pre_ffn, g_post_ffn, w_in, w_out, conv_w,
           conv_b, igate_b, fgate_b, mlstm_norm_w, swa_sinks, ffn_w_gate, ffn_w_up, ffn_w_down,
           moe_router, moe_w_gate, moe_w_up, moe_w_down):
    batch, seq, d = x.shape
    depth = ada_w.shape[0]
    n = batch * seq
    x2 = x.reshape(n, d)
    mod_all = _adaln(c, ada_w, ada_b).reshape(depth, batch, 6, 1, d)
    for l in range(depth):
        mod = mod_all[l]
        proj = _inproj(x2, g_pre_mix[l].reshape(1, d), mod, _pack_w_in(w_in[l]), seq)
        y_moba = _moba(proj, batch, seq)
        gate_bias = jnp.concatenate(
            [igate_b[l], fgate_b[l], jnp.zeros((LANES - 2 * MLSTM_HEADS,), F32)]).reshape(1, LANES)
        y_mlstm = _mlstm(proj, conv_w[l], conv_b[l].reshape(1, -1), gate_bias,
                         mlstm_norm_w[l].reshape(1, -1), batch, seq)
        y_swa = _swa(proj, swa_sinks[l], batch, seq)
        j = l // 2
        w_router = None
        if l % 2 == 1:
            w_router = jnp.pad(moe_router[j], ((0, 0), (0, LANES - N_EXPERTS))).astype(BF16)
        outs = _outproj(y_moba, y_mlstm, y_swa, w_out[l].astype(BF16), x2,
                        g_post_mix[l].reshape(1, d), g_pre_ffn[l].reshape(1, d), mod, seq, w_router)
        gpost = g_post_ffn[l].reshape(1, d)
        if l % 2 == 0:
            x2, h = outs
            x2 = _ffn(h, ffn_w_gate[j].astype(BF16), ffn_w_up[j].astype(BF16),
                      ffn_w_down[j].astype(BF16), x2, gpost, mod, seq)
        else:
            x2, h, comb = outs
            x2 = _ffn(h, moe_w_gate[j].astype(BF16), moe_w_up[j].astype(BF16),
                      moe_w_down[j].astype(BF16), x2, gpost, mod, seq, comb)
    return x2.reshape(batch, seq, d)
```

```python
import functools

import jax
import jax.numpy as jnp
from jax import lax
from jax.experimental import pallas as pl
from jax.experimental.pallas import tpu as pltpu

F32 = jnp.float32
BF16 = jnp.bfloat16

LANES = 128
HEAD_W = 64
MOBA_BLOCK = 256
MOBA_TOPK = 3
MOBA_PAIRS = 4
MLSTM_HEADS = 4
MLSTM_CHUNK = 128
MLSTM_CONV = 4
MLSTM_GATE_CAP = 15.0
SWA_WINDOW = 128
SWA_GROUPS = 2
SWA_GROUP_HEADS = 8
N_EXPERTS = 8
RMS_EPS = 1e-6
NEG_INF = float("-inf")
MASK_NEG = -1e30
MOE_TILE = 512
TOP_K = 2
RT_E1, RT_E2, RT_W1, RT_W2, RT_R1, RT_R2 = range(6)
VMEM_LIMIT = 56 * 1024 * 1024

PB_SWA_Q = 0
PB_MOBA_Q = 8
PB_MOBA_K = 12
PB_MOBA_V = 16
PB_ML_Q = 20
PB_ML_K = 24
PB_ML_V = 28
PB_ML_O = 32
PB_ML_G = 36
PB_SWA_K = 38
PB_SWA_V = 40
PROJ_BLOCKS = 42


def _params(sem):
    return pltpu.CompilerParams(dimension_semantics=sem, vmem_limit_bytes=VMEM_LIMIT)


def _rms(x, g):
    return x * lax.rsqrt(jnp.mean(x * x, axis=-1, keepdims=True) + RMS_EPS) * g


def _dot(a, b):
    return jnp.dot(a, b, preferred_element_type=F32)


def _dot_nt(a, b):
    return lax.dot_general(a, b, (((1,), (1,)), ((), ())), preferred_element_type=F32)


def _adaln_kernel(c_ref, w_ref, b_ref, o_ref):
    c = c_ref[...]
    cond = (c * jax.nn.sigmoid(c)).astype(BF16)
    o_ref[...] = _dot(cond, w_ref[...].astype(BF16)) + b_ref[...]


def _adaln(c, ada_w, ada_b):
    depth, d, n6 = ada_w.shape
    b = c.shape[0]
    tn = 1024
    return pl.pallas_call(
        _adaln_kernel,
        grid=(depth, n6 // tn),
        in_specs=[
            pl.BlockSpec((b, d), lambda l, j: (0, 0)),
            pl.BlockSpec((None, d, tn), lambda l, j: (l, 0, j)),
            pl.BlockSpec((None, 1, tn), lambda l, j: (l, 0, j)),
        ],
        out_specs=pl.BlockSpec((None, b, tn), lambda l, j: (l, 0, j)),
        out_shape=jax.ShapeDtypeStruct((depth, b, n6), F32),
        compiler_params=_params(("arbitrary", "arbitrary")),
        name="adaln",
    )(c, ada_w, ada_b.reshape(depth, 1, n6))


def _inproj_kernel(x_ref, g_ref, sc_ref, sh_ref, w_ref, o_ref, *, tn):
    h = (_rms(x_ref[...], g_ref[...]) * (1.0 + sc_ref[...]) + sh_ref[...]).astype(BF16)
    for c in range(w_ref.shape[1] // tn):
        o_ref[:, c * tn:(c + 1) * tn] = _dot(h, w_ref[:, c * tn:(c + 1) * tn])


def _inproj(x2, g, mod, w, seq):
    n, d = x2.shape
    nc = w.shape[1]
    tm, tn = 256, 768
    per_b = seq // tm
    return pl.pallas_call(
        functools.partial(_inproj_kernel, tn=tn),
        grid=(n // tm,),
        in_specs=[
            pl.BlockSpec((tm, d), lambda i: (i, 0)),
            pl.BlockSpec((1, d), lambda i: (0, 0)),
            pl.BlockSpec((None, None, 1, d), lambda i: (i // per_b, 1, 0, 0)),
            pl.BlockSpec((None, None, 1, d), lambda i: (i // per_b, 0, 0, 0)),
            pl.BlockSpec((d, nc), lambda i: (0, 0), pipeline_mode=pl.Buffered(1)),
        ],
        out_specs=pl.BlockSpec((tm, nc), lambda i: (i, 0)),
        out_shape=jax.ShapeDtypeStruct((n, nc), F32),
        compiler_params=_params(("arbitrary",)),
        name="inproj",
    )(x2, g, mod, mod, w)


def _moba_kernel(q_ref, k_ref, v_ref, o_ref, ka_scr, vb_scr):
    seq = k_ref.shape[0]
    blk = MOBA_BLOCK
    n_blk = seq // blk
    lane = lax.broadcasted_iota(jnp.int32, (1, LANES), 1)
    k = k_ref[...]
    vb_scr[...] = v_ref[...].astype(BF16)
    kmean = jnp.mean(k.reshape(n_blk, blk, LANES), axis=1)
    kblk = lax.shift_right_logical(lax.broadcasted_iota(jnp.int32, (seq, 1), 0), 8)
    ri = lax.broadcasted_iota(jnp.int32, (blk, blk), 0)
    ci = lax.broadcasted_iota(jnp.int32, (blk, blk), 1)
    causal_bias = jnp.where(ri >= ci, 0.0, MASK_NEG)
    scale = HEAD_W ** -0.5
    in_head, slot, kmp = [], [], []
    for hh in range(2):
        base = HEAD_W * (1 - hh)
        in_head.append((lane >= HEAD_W * hh) & (lane < HEAD_W * (hh + 1)))
        slot.append(lane - base)
        ka_scr[hh] = jnp.where(in_head[hh], k, jnp.where(slot[hh] == kblk, 1.0, 0.0)).astype(BF16)
        rows = [jnp.where(in_head[hh], kmean, 0.0), jnp.zeros((LANES - base - n_blk, LANES), F32)]
        if base:
            rows.insert(0, jnp.zeros((base, LANES), F32))
        kmp.append(jnp.concatenate(rows, axis=0).astype(BF16))

    for i in range(n_blk):
        q = q_ref[i * blk:(i + 1) * blk, :]
        outs = []
        for hh in range(2):
            base = HEAD_W * (1 - hh)
            qa = jnp.where(in_head[hh], q * scale, 0.0)
            if i > MOBA_TOPK:
                valid = (slot[hh] >= 0) & (slot[hh] < i)
                qm = jnp.where(in_head[hh], q, 0.0).astype(BF16)
                gate = jnp.where(valid, _dot_nt(qm, kmp[hh]), NEG_INF)
                beaten_by = jnp.zeros((blk, LANES), jnp.int32)
                for j in range(i):
                    gj = gate[:, base + j:base + j + 1]
                    beats = (gj > gate) | ((gj == gate) & (slot[hh] > j))
                    beaten_by = beaten_by + beats.astype(jnp.int32)
                qa = jnp.where(valid & (beaten_by >= MOBA_TOPK), MASK_NEG, qa)
            s = _dot_nt(qa.astype(BF16), ka_scr[hh, 0:(i + 1) * blk, :])
            s_own = s[:, i * blk:] + causal_bias
            m = jnp.max(s_own, axis=-1, keepdims=True)
            if i:
                s_past = s[:, :i * blk]
                m = jnp.maximum(m, jnp.max(s_past, axis=-1, keepdims=True))
            p_own = jnp.exp(s_own - m)
            l = jnp.sum(p_own, axis=-1, keepdims=True)
            acc = _dot(p_own.astype(BF16), vb_scr[i * blk:(i + 1) * blk, :])
            if i:
                p_past = jnp.exp(s_past - m)
                l = l + jnp.sum(p_past, axis=-1, keepdims=True)
                acc = acc + _dot(p_past.astype(BF16), vb_scr[0:i * blk, :])
            outs.append(acc / l)
        o_ref[i * blk:(i + 1) * blk, :] = jnp.where(lane < HEAD_W, outs[0], outs[1]).astype(o_ref.dtype)


def _moba(proj, batch, seq):
    n = proj.shape[0]
    return pl.pallas_call(
        _moba_kernel,
        grid=(batch, MOBA_PAIRS),
        in_specs=[
            pl.BlockSpec((seq, LANES), lambda b, p: (b, PB_MOBA_Q + p)),
            pl.BlockSpec((seq, LANES), lambda b, p: (b, PB_MOBA_K + p)),
            pl.BlockSpec((seq, LANES), lambda b, p: (b, PB_MOBA_V + p)),
        ],
        out_specs=pl.BlockSpec((seq, LANES), lambda b, p: (b, p)),
        out_shape=jax.ShapeDtypeStruct((n, MOBA_PAIRS * LANES), BF16),
        scratch_shapes=[
            pltpu.VMEM((2, seq, LANES), BF16),
            pltpu.VMEM((seq, LANES), BF16),
        ],
        compiler_params=_params(("arbitrary", "arbitrary")),
        name="moba",
    )(proj, proj, proj)


def _causal_conv_silu(x, w, b):
    seq = x.shape[0]
    row = lax.broadcasted_iota(jnp.int32, (seq, 1), 0)
    y = b + w[MLSTM_CONV - 1:MLSTM_CONV, :] * x
    for shift in range(1, MLSTM_CONV):
        xs = jnp.where(row >= shift, pltpu.roll(x, shift, axis=0), 0.0)
        y = y + w[MLSTM_CONV - 1 - shift:MLSTM_CONV - shift, :] * xs
    return y * jax.nn.sigmoid(y)


def _mlstm_kernel(q_ref, k_ref, v_ref, og_ref, gate_ref, cwq_ref, cwk_ref, cbq_ref, cbk_ref,
                  gb_ref, nw_ref, o_ref, qc_scr, kc_scr, vb_scr, acol_scr, bcol_scr,
                  arow_scr, brow_scr):
    h = pl.program_id(1)
    seq = q_ref.shape[0]
    L = MLSTM_CHUNK
    n_chunk = seq // L
    lane = lax.broadcasted_iota(jnp.int32, (1, LANES), 1)
    ri = lax.broadcasted_iota(jnp.int32, (L, L), 0)
    ci = lax.broadcasted_iota(jnp.int32, (L, L), 1)
    causal = ri >= ci

    @pl.when(h == 0)
    def _():
        t = MLSTM_GATE_CAP * jnp.tanh((gate_ref[...] + gb_ref[...]) / MLSTM_GATE_CAP)
        a = jnp.where(lane < MLSTM_HEADS, t, jax.nn.log_sigmoid(t))
        acol_scr[...] = a
        arow_scr[...] = a.T[0:8, :]
        lower = causal.astype(F32)
        upper = (ri <= ci).astype(F32)
        for c in range(n_chunk):
            rows = slice(c * L, (c + 1) * L)
            bcol_scr[rows, :] = jnp.dot(lower, acol_scr[rows, :], precision=lax.Precision.HIGHEST,
                                        preferred_element_type=F32)
            brow_scr[:, rows] = jnp.dot(arow_scr[:, rows], upper, precision=lax.Precision.HIGHEST,
                                        preferred_element_type=F32)

    qc_scr[...] = _causal_conv_silu(q_ref[...], cwq_ref[...], cbq_ref[...]).astype(BF16)
    kc = _causal_conv_silu(k_ref[...], cwk_ref[...], cbk_ref[...]) * (LANES ** -0.5)
    kc_scr[...] = kc.astype(BF16)
    vb_scr[...] = v_ref[...].astype(BF16)
    norm_w = nw_ref[...]
    sel_i = lane == h
    sel_f = lane == h + MLSTM_HEADS
    sub = lax.broadcasted_iota(jnp.int32, (8, 1), 0)

    def chunk(c, carry):
        c_st, n_st, m_st = carry
        r0 = pl.multiple_of(c * L, L)
        q = qc_scr[pl.ds(r0, L), :]
        k = kc_scr[pl.ds(r0, L), :]
        v = vb_scr[pl.ds(r0, L), :]
        b_col = jnp.sum(jnp.where(sel_f, bcol_scr[pl.ds(r0, L), :], 0.0), axis=1, keepdims=True)
        li_col = jnp.sum(jnp.where(sel_i, acol_scr[pl.ds(r0, L), :], 0.0), axis=1, keepdims=True)
        b_row = jnp.sum(jnp.where(sub == h + MLSTM_HEADS, brow_scr[:, pl.ds(r0, L)], 0.0),
                        axis=0, keepdims=True)
        li_row = jnp.sum(jnp.where(sub == h, arow_scr[:, pl.ds(r0, L)], 0.0), axis=0, keepdims=True)
        b_last = b_row[:, L - 1:L]
        d_log = jnp.where(causal, b_col - b_row + li_row, NEG_INF)
        inter_log = b_col + m_st
        m_out = jnp.maximum(inter_log, jnp.max(d_log, axis=1, keepdims=True))
        w_inter = jnp.exp(inter_log - m_out)
        qk = _dot_nt(q, k) * jnp.exp(d_log - m_out)
        num = _dot(qk.astype(BF16), v) + w_inter * _dot(q, c_st.astype(BF16))
        den = jnp.sum(qk, axis=1, keepdims=True) + w_inter * jnp.sum(
            q.astype(F32) * n_st, axis=1, keepdims=True)
        hv = num / jnp.maximum(jnp.abs(den), jnp.exp(-m_out))
        hn = _rms(hv, norm_w)
        o_ref[pl.ds(r0, L), :] = (jax.nn.sigmoid(og_ref[pl.ds(r0, L), :]) * hn).astype(o_ref.dtype)
        a_max = jnp.max(b_last - b_row + li_row, axis=1, keepdims=True)
        kw = k.astype(F32) * jnp.exp(b_last - b_col + li_col - a_max)
        c_in = _dot(kw.T.astype(BF16), v)
        n_in = jnp.sum(kw, axis=0, keepdims=True)
        m_new = jnp.maximum(b_last + m_st, a_max)
        decay = jnp.exp(b_last + m_st - m_new)
        inject = jnp.exp(a_max - m_new)
        return decay * c_st + inject * c_in, decay * n_st + inject * n_in, m_new

    init = (jnp.zeros((LANES, LANES), F32), jnp.zeros((1, LANES), F32), jnp.zeros((1, 1), F32))
    lax.fori_loop(0, n_chunk, chunk, init)


def _mlstm(proj, conv_w, conv_b, gate_bias, norm_w, batch, seq):
    n = proj.shape[0]
    hw = MLSTM_HEADS

    def col(base):
        return pl.BlockSpec((seq, LANES), lambda b, h: (b, base + h))

    return pl.pallas_call(
        _mlstm_kernel,
        grid=(batch, hw),
        in_specs=[
            col(PB_ML_Q), col(PB_ML_K), col(PB_ML_V), col(PB_ML_O),
            pl.BlockSpec((seq, LANES), lambda b, h: (b, PB_ML_G)),
            pl.BlockSpec((MLSTM_CONV, LANES), lambda b, h: (0, h)),
            pl.BlockSpec((MLSTM_CONV, LANES), lambda b, h: (0, hw + h)),
            pl.BlockSpec((1, LANES), lambda b, h: (0, h)),
            pl.BlockSpec((1, LANES), lambda b, h: (0, hw + h)),
            pl.BlockSpec((1, LANES), lambda b, h: (0, 0)),
            pl.BlockSpec((1, LANES), lambda b, h: (0, h)),
        ],
        out_specs=pl.BlockSpec((seq, LANES), lambda b, h: (b, h)),
        out_shape=jax.ShapeDtypeStruct((n, hw * LANES), BF16),
        scratch_shapes=[
            pltpu.VMEM((seq, LANES), BF16),
            pltpu.VMEM((seq, LANES), BF16),
            pltpu.VMEM((seq, LANES), BF16),
            pltpu.VMEM((seq, LANES), F32),
            pltpu.VMEM((seq, LANES), F32),
            pltpu.VMEM((8, seq), F32),
            pltpu.VMEM((8, seq), F32),
        ],
        compiler_params=_params(("arbitrary", "arbitrary")),
        name="mlstm",
    )(proj, proj, proj, proj, proj, conv_w, conv_w, conv_b, conv_b, gate_bias, norm_w)


def _swa_kernel(sink_ref, q_ref, k_ref, v_ref, o_ref, kb_scr, vb_scr):
    g = pl.program_id(1)
    seq = q_ref.shape[0]
    W = SWA_WINDOW
    n_pairs = SWA_GROUP_HEADS // 2
    kb_scr[...] = k_ref[...].astype(BF16)
    vb_scr[...] = v_ref[...].astype(BF16)
    lane = lax.broadcasted_iota(jnp.int32, (1, LANES), 1)
    low = lane < HEAD_W
    scale = HEAD_W ** -0.5
    sinks = [sink_ref[g * SWA_GROUP_HEADS + hd] for hd in range(SWA_GROUP_HEADS)]

    def block(nb, carry):
        start = pl.multiple_of(jnp.maximum(nb - 1, 0) * W, W)
        r0 = pl.multiple_of(nb * W, W)
        qn = q_ref[pl.ds(r0, W), :]
        kband = kb_scr[pl.ds(start, 2 * W), :]
        vband = vb_scr[pl.ds(start, 2 * W), :]
        parts = []
        for p in range(n_pairs):
            qp = qn[:, p * LANES:(p + 1) * LANES]
            parts.append(jnp.where(low, qp, 0.0).astype(BF16))
            parts.append(jnp.where(low, 0.0, qp).astype(BF16))
        s_all = _dot_nt(jnp.concatenate(parts, axis=0), kband) * scale
        kabs = start + lax.broadcasted_iota(jnp.int32, (1, 2 * W), 1)
        qabs = r0 + lax.broadcasted_iota(jnp.int32, (W, 1), 0)
        allowed = (kabs <= qabs) & (kabs > qabs - W)
        probs = []
        for hd in range(SWA_GROUP_HEADS):
            s = jnp.where(allowed, s_all[hd * W:(hd + 1) * W, :], NEG_INF)
            m = jnp.maximum(jnp.max(s, axis=-1, keepdims=True), sinks[hd])
            e = jnp.exp(s - m)
            l = jnp.sum(e, axis=-1, keepdims=True) + jnp.exp(sinks[hd] - m)
            probs.append((e / l).astype(BF16))
        o_all = _dot(jnp.concatenate(probs, axis=0), vband)
        outs = []
        for p in range(n_pairs):
            lo = o_all[(2 * p) * W:(2 * p + 1) * W, :]
            hi = o_all[(2 * p + 1) * W:(2 * p + 2) * W, :]
            outs.append(jnp.where(low, lo, hi))
        o_ref[pl.ds(r0, W), :] = jnp.concatenate(outs, axis=1).astype(o_ref.dtype)
        return carry

    lax.fori_loop(0, seq // W, block, 0)


def _swa(proj, sinks, batch, seq):
    n = proj.shape[0]
    gw = SWA_GROUP_HEADS * HEAD_W
    gb = gw // LANES
    return pl.pallas_call(
        _swa_kernel,
        grid=(batch, SWA_GROUPS),
        in_specs=[
            pl.BlockSpec(memory_space=pltpu.SMEM),
            pl.BlockSpec((seq, gw), lambda b, g: (b, PB_SWA_Q // gb + g)),
            pl.BlockSpec((seq, LANES), lambda b, g: (b, PB_SWA_K + g)),
            pl.BlockSpec((seq, LANES), lambda b, g: (b, PB_SWA_V + g)),
        ],
        out_specs=pl.BlockSpec((seq, gw), lambda b, g: (b, g)),
        out_shape=jax.ShapeDtypeStruct((n, SWA_GROUPS * gw), BF16),
        scratch_shapes=[pltpu.VMEM((seq, LANES), BF16), pltpu.VMEM((seq, LANES), BF16)],
        compiler_params=_params(("arbitrary", "arbitrary")),
        name="swa",
    )(sinks, proj, proj, proj)


def _outproj_kernel(*refs, with_router):
    if with_router:
        (ya_ref, yb_ref, yc_ref, w_ref, x_ref, gpost_ref, gt_ref, gpre_ref, sc_ref, sh_ref,
         wr_ref, xo_ref, h_ref, route_ref, count_ref, count_scr) = refs
    else:
        (ya_ref, yb_ref, yc_ref, w_ref, x_ref, gpost_ref, gt_ref, gpre_ref, sc_ref, sh_ref,
         xo_ref, h_ref) = refs
    wa = ya_ref.shape[1]
    wb = yb_ref.shape[1]
    y = _dot(ya_ref[...], w_ref[0:wa, :])
    y = y + _dot(yb_ref[...], w_ref[wa:wa + wb, :])
    y = y + _dot(yc_ref[...], w_ref[wa + wb:, :])
    xn = x_ref[...] + gt_ref[...] * _rms(y, gpost_ref[...])
    xo_ref[...] = xn
    hb = (_rms(xn, gpre_ref[...]) * (1.0 + sc_ref[...]) + sh_ref[...]).astype(BF16)
    h_ref[...] = hb.astype(h_ref.dtype)
    if with_router:
        @pl.when(pl.program_id(0) == 0)
        def _():
            count_scr[...] = jnp.zeros_like(count_scr)

        tm = hb.shape[0]
        lane = lax.broadcasted_iota(jnp.int32, (1, LANES), 1)
        logits = jnp.where(lane < N_EXPERTS, _dot(hb, wr_ref[...]), NEG_INF)
        m1 = jnp.max(logits, axis=-1, keepdims=True)
        i1 = jnp.min(jnp.where(logits == m1, lane, LANES), axis=-1, keepdims=True)
        rest = jnp.where(lane == i1, NEG_INF, logits)
        m2 = jnp.max(rest, axis=-1, keepdims=True)
        i2 = jnp.min(jnp.where(rest == m2, lane, LANES), axis=-1, keepdims=True)
        e2 = jnp.exp(m2 - m1)
        w1 = 1.0 / (1.0 + e2)
        picked = jnp.where((lane == i1) | (lane == i2), 1.0, 0.0)
        ri = lax.broadcasted_iota(jnp.int32, (tm, tm), 0)
        ci = lax.broadcasted_iota(jnp.int32, (tm, tm), 1)
        before = jnp.where(ri > ci, 1.0, 0.0).astype(BF16)
        rank = _dot(before, picked.astype(BF16)) + count_scr[...]
        r1 = jnp.sum(jnp.where(lane == i1, rank, 0.0), axis=-1, keepdims=True)
        r2 = jnp.sum(jnp.where(lane == i2, rank, 0.0), axis=-1, keepdims=True)
        count_scr[...] += jnp.sum(picked, axis=0, keepdims=True)
        count_ref[...] = jnp.broadcast_to(count_scr[...], count_ref.shape)
        rec = jnp.where(lane == RT_E1, i1.astype(F32), 0.0)
        for slot_lane, val in ((RT_E2, i2.astype(F32)), (RT_W1, w1), (RT_W2, e2 * w1),
                               (RT_R1, r1), (RT_R2, r2)):
            rec = jnp.where(lane == slot_lane, val, rec)
        route_ref[...] = rec


def _outproj(ya, yb, yc, w, x2, gpost, gpre, mod, seq, w_router=None):
    n, d = x2.shape
    tm = 256
    per_b = seq // tm
    with_router = w_router is not None

    def rows(width):
        return pl.BlockSpec((tm, width), lambda i: (i, 0))

    def vec():
        return pl.BlockSpec((1, d), lambda i: (0, 0))

    def modrow(k):
        return pl.BlockSpec((None, None, 1, d), lambda i: (i // per_b, k, 0, 0))

    in_specs = [rows(ya.shape[1]), rows(yb.shape[1]), rows(yc.shape[1]),
                pl.BlockSpec((d, d), lambda i: (0, 0)), rows(d),
                vec(), modrow(2), vec(), modrow(4), modrow(3)]
    args = [ya, yb, yc, w, x2, gpost, mod, gpre, mod, mod]
    out_specs = [rows(d), rows(d)]
    out_shape = [jax.ShapeDtypeStruct((n, d), F32),
                 jax.ShapeDtypeStruct((n, d), F32 if with_router else BF16)]
    scratch = []
    if with_router:
        in_specs.append(pl.BlockSpec((d, LANES), lambda i: (0, 0)))
        args.append(w_router)
        out_specs += [rows(LANES), pl.BlockSpec((8, LANES), lambda i: (0, 0))]
        out_shape += [jax.ShapeDtypeStruct((n, LANES), F32), jax.ShapeDtypeStruct((8, LANES), F32)]
        scratch.append(pltpu.VMEM((1, LANES), F32))
    return pl.pallas_call(
        functools.partial(_outproj_kernel, with_router=with_router),
        grid=(n // tm,),
        in_specs=in_specs,
        out_specs=out_specs,
        out_shape=out_shape,
        scratch_shapes=scratch,
        compiler_params=_params(("arbitrary",)),
        name="outproj",
    )(*args)


def _swiglu_step(h, wg_ref, wu_ref, wd_ref):
    gate = _dot(h, wg_ref[...])
    a = gate * jax.nn.sigmoid(gate) * _dot(h, wu_ref[...])
    return _dot(a.astype(BF16), wd_ref[...])


def _ffn_kernel(h_ref, wg_ref, wu_ref, wd_ref, x_ref, gpost_ref, gt_ref, o_ref, acc):
    f = pl.program_id(1)

    @pl.when(f == 0)
    def _():
        acc[...] = jnp.zeros_like(acc)

    acc[...] += _swiglu_step(h_ref[...], wg_ref, wu_ref, wd_ref)

    @pl.when(f == pl.num_programs(1) - 1)
    def _():
        o_ref[...] = x_ref[...] + gt_ref[...] * _rms(acc[...], gpost_ref[...])


def _ffn(h, wg, wu, wd, x2, gpost, mod, seq):
    n, d = x2.shape
    tm, tf = 512, 512
    per_b = seq // tm
    dff = wg.shape[1]
    row = lambda i, f: (i, 0)
    return pl.pallas_call(
        _ffn_kernel,
        grid=(n // tm, dff // tf),
        in_specs=[
            pl.BlockSpec((tm, d), row),
            pl.BlockSpec((d, tf), lambda i, f: (0, f)),
            pl.BlockSpec((d, tf), lambda i, f: (0, f)),
            pl.BlockSpec((tf, d), lambda i, f: (f, 0)),
            pl.BlockSpec((tm, d), row),
            pl.BlockSpec((1, d), lambda i, f: (0, 0)),
            pl.BlockSpec((None, None, 1, d), lambda i, f: (i // per_b, 5, 0, 0)),
        ],
        out_specs=pl.BlockSpec((tm, d), row),
        out_shape=jax.ShapeDtypeStruct((n, d), F32),
        scratch_shapes=[pltpu.VMEM((tm, d), F32)],
        compiler_params=_params(("arbitrary", "arbitrary")),
        name="ffn",
    )(h, wg, wu, wd, x2, gpost, mod)


def _dispatch_kernel(p1_ref, p2_ref, h_ref, xs_init_ref, xs_ref, sems):
    del xs_init_ref
    tm = h_ref.shape[0]
    base = pl.program_id(0) * tm

    def issue(r, carry):
        row = h_ref.at[pl.ds(r, 1)]
        pltpu.make_async_copy(row, xs_ref.at[pl.ds(p1_ref[base + r], 1)], sems.at[0]).start()
        pltpu.make_async_copy(row, xs_ref.at[pl.ds(p2_ref[base + r], 1)], sems.at[1]).start()
        return carry

    lax.fori_loop(0, tm, issue, 0, unroll=8)
    for k in range(TOP_K):
        pltpu.make_async_copy(h_ref, xs_ref.at[pl.ds(0, tm)], sems.at[k]).wait()


def _dispatch(h, p1, p2, n_slots):
    n, d = h.shape
    tm = 512
    return pl.pallas_call(
        _dispatch_kernel,
        grid_spec=pltpu.PrefetchScalarGridSpec(
            num_scalar_prefetch=2,
            grid=(n // tm,),
            in_specs=[pl.BlockSpec((tm, d), lambda i, p1, p2: (i, 0)),
                      pl.BlockSpec(memory_space=pl.ANY)],
            out_specs=pl.BlockSpec(memory_space=pl.ANY),
            scratch_shapes=[pltpu.SemaphoreType.DMA((TOP_K,))],
        ),
        out_shape=jax.ShapeDtypeStruct((n_slots, d), F32),
        input_output_aliases={3: 0},
        compiler_params=_params(("arbitrary",)),
        name="moe_dispatch",
    )(p1, p2, h, jnp.zeros((n_slots, d), F32))


def _moe_ffn_kernel(te_ref, nv_ref, x_ref, wg_ref, wu_ref, wd_ref, y_ref, h_scr):
    del te_ref
    f = pl.program_id(1)
    used = pl.program_id(0) < nv_ref[0]

    @pl.when(f == 0)
    def _():
        y_ref[...] = jnp.zeros_like(y_ref)

    @pl.when(used & (f == 0))
    def _():
        h_scr[...] = x_ref[...].astype(BF16)

    @pl.when(used)
    def _():
        y_ref[...] += _swiglu_step(h_scr[...], wg_ref, wu_ref, wd_ref)


def _moe_ffn(xs, tile_expert, n_valid, wg, wu, wd):
    n_slots, d = xs.shape
    dff = wg.shape[2]
    tm, tf = MOE_TILE, 256
    nf = dff // tf

    def row(j, f, te, nv):
        return (jnp.minimum(j, nv[0] - 1), 0)

    def fcol(j, f, nv):
        return jnp.where(j < nv[0], f, nf - 1)

    return pl.pallas_call(
        _moe_ffn_kernel,
        grid_spec=pltpu.PrefetchScalarGridSpec(
            num_scalar_prefetch=2,
            grid=(n_slots // tm, nf),
            in_specs=[
                pl.BlockSpec((tm, d), row),
                pl.BlockSpec((None, d, tf), lambda j, f, te, nv: (te[j], 0, fcol(j, f, nv))),
                pl.BlockSpec((None, d, tf), lambda j, f, te, nv: (te[j], 0, fcol(j, f, nv))),
                pl.BlockSpec((None, tf, d), lambda j, f, te, nv: (te[j], fcol(j, f, nv), 0)),
            ],
            out_specs=pl.BlockSpec((tm, d), lambda j, f, te, nv: (j, 0)),
            scratch_shapes=[pltpu.VMEM((tm, d), BF16)],
        ),
        out_shape=jax.ShapeDtypeStruct((n_slots, d), F32),
        compiler_params=_params(("arbitrary", "arbitrary")),
        name="moe_ffn",
    )(tile_expert, n_valid, xs, wg, wu, wd)


def _combine_kernel(p1_ref, p2_ref, ys_ref, route_ref, x_ref, gpost_ref, gt_ref, o_ref,
                    y1_scr, y2_scr, sems):
    tm = x_ref.shape[0]
    base = pl.program_id(0) * tm

    def issue(r, carry):
        pltpu.make_async_copy(ys_ref.at[pl.ds(p1_ref[base + r], 1)], y1_scr.at[pl.ds(r, 1)],
                              sems.at[0]).start()
        pltpu.make_async_copy(ys_ref.at[pl.ds(p2_ref[base + r], 1)], y2_scr.at[pl.ds(r, 1)],
                              sems.at[1]).start()
        return carry


    lax.fori_loop(0, tm, issue, 0, unroll=8)
    pltpu.make_async_copy(ys_ref.at[pl.ds(0, tm)], y1_scr, sems.at[0]).wait()
    pltpu.make_async_copy(ys_ref.at[pl.ds(0, tm)], y2_scr, sems.at[1]).wait()
    lane = lax.broadcasted_iota(jnp.int32, (1, LANES), 1)
    route = route_ref[...]
    w1 = jnp.sum(jnp.where(lane == RT_W1, route, 0.0), axis=-1, keepdims=True)
    w2 = jnp.sum(jnp.where(lane == RT_W2, route, 0.0), axis=-1, keepdims=True)
    y = w1 * y1_scr[...] + w2 * y2_scr[...]
    o_ref[...] = x_ref[...] + gt_ref[...] * _rms(y, gpost_ref[...])


def _combine(ys, p1, p2, route, x2, gpost, mod, seq):
    n, d = x2.shape
    tm = 512
    per_b = seq // tm
    row = lambda i, p1, p2: (i, 0)
    return pl.pallas_call(
        _combine_kernel,
        grid_spec=pltpu.PrefetchScalarGridSpec(
            num_scalar_prefetch=2,
            grid=(n // tm,),
            in_specs=[
                pl.BlockSpec(memory_space=pl.ANY),
                pl.BlockSpec((tm, LANES), row),
                pl.BlockSpec((tm, d), row),
                pl.BlockSpec((1, d), lambda i, p1, p2: (0, 0)),
                pl.BlockSpec((None, None, 1, d), lambda i, p1, p2: (i // per_b, 5, 0, 0)),
            ],
            out_specs=pl.BlockSpec((tm, d), row),
            scratch_shapes=[pltpu.VMEM((tm, d), F32), pltpu.VMEM((tm, d), F32),
                            pltpu.SemaphoreType.DMA((TOP_K,))],
        ),
        out_shape=jax.ShapeDtypeStruct((n, d), F32),
        compiler_params=_params(("arbitrary",)),
        name="moe_combine",
    )(p1, p2, ys, route, x2, gpost, mod)


def _route_plan(route, counts, n_tiles):
    e1 = route[:, RT_E1].astype(jnp.int32)
    e2 = route[:, RT_E2].astype(jnp.int32)
    cnt = counts[0, :N_EXPERTS].astype(jnp.int32)
    size = (cnt + MOE_TILE - 1) // MOE_TILE * MOE_TILE
    end = jnp.cumsum(size)
    start = end - size
    p1 = start[e1] + route[:, RT_R1].astype(jnp.int32)
    p2 = start[e2] + route[:, RT_R2].astype(jnp.int32)
    n_valid = end[-1] // MOE_TILE
    tile_start = jnp.minimum(jnp.arange(n_tiles), n_valid - 1) * MOE_TILE
    tile_expert = jnp.sum(tile_start[:, None] >= end[None, :], axis=1).astype(jnp.int32)
    return p1, p2, tile_expert, n_valid.reshape(1).astype(jnp.int32)


def _pack_w_in(w_in):
    d = w_in.shape[0]
    gw = 4 * LANES
    offs = [0]
    for width in (gw, gw, gw, gw, gw, gw, gw, MLSTM_HEADS, MLSTM_HEADS,
                  2 * SWA_GROUP_HEADS * HEAD_W, SWA_GROUPS * HEAD_W, SWA_GROUPS * HEAD_W):
        offs.append(offs[-1] + width)
    (mb_q, mb_k, mb_v, ml_q, ml_k, ml_v, ml_o, ml_i, ml_f, sw_q, sw_k, sw_v) = [
        w_in[:, offs[j]:offs[j + 1]] for j in range(12)]
    gates = jnp.concatenate(
        [ml_i, ml_f, jnp.zeros((d, 2 * LANES - 2 * MLSTM_HEADS), w_in.dtype)], axis=1)

    def dup(w):
        heads = [w[:, g * HEAD_W:(g + 1) * HEAD_W] for g in range(SWA_GROUPS)]
        return jnp.concatenate([t for hd in heads for t in (hd, hd)], axis=1)

    packed = jnp.concatenate(
        [sw_q, mb_q, mb_k, mb_v, ml_q, ml_k, ml_v, ml_o, gates, dup(sw_k), dup(sw_v)], axis=1)
    assert packed.shape[1] == PROJ_BLOCKS * LANES
    return packed.astype(BF16)


def kernel(x, c, ada_w, ada_b, g_pre_mix, g_post_mix, g_pre_ffn, g_post_ffn, w_in, w_out, conv_w,
           conv_b, igate_b, fgate_b, mlstm_norm_w, swa_sinks, ffn_w_gate, ffn_w_up, ffn_w_down,
           moe_router, moe_w_gate, moe_w_up, moe_w_down):
    batch, seq, d = x.shape
    depth = ada_w.shape[0]
    n = batch * seq
    x2 = x.reshape(n, d)
    mod_all = _adaln(c, ada_w, ada_b).reshape(depth, batch, 6, 1, d)
    for l in range(depth):
        mod = mod_all[l]
        proj = _inproj(x2, g_pre_mix[l].reshape(1, d), mod, _pack_w_in(w_in[l]), seq)
        y_moba = _moba(proj, batch, seq)
        gate_bias = jnp.concatenate(
            [igate_b[l], fgate_b[l], jnp.zeros((LANES - 2 * MLSTM_HEADS,), F32)]).reshape(1, LANES)
        y_mlstm = _mlstm(proj, conv_w[l], conv_b[l].reshape(1, -1), gate_bias,
                         mlstm_norm_w[l].reshape(1, -1), batch, seq)
        y_swa = _swa(proj, swa_sinks[l], batch, seq)
        j = l // 2
        w_router = None
        if l % 2 == 1:
            w_router = jnp.pad(moe_router[j], ((0, 0), (0, LANES - N_EXPERTS))).astype(BF16)
        outs = _outproj(y_moba, y_mlstm, y_swa, w_out[l].astype(BF16), x2,
                        g_post_mix[l].reshape(1, d), g_pre_ffn[l].reshape(1, d), mod, seq, w_router)
        gpost = g_post_ffn[l].reshape(1, d)
        if l % 2 == 0:
            x2, h = outs
            x2 = _ffn(h, ffn_w_gate[j].astype(BF16), ffn_w_up[j].astype(BF16),
                      ffn_w_down[j].astype(BF16), x2, gpost, mod, seq)
        else:
            x2, h, route, counts = outs
            n_tiles = (TOP_K * n) // MOE_TILE + N_EXPERTS
            p1, p2, tile_expert, n_valid = _route_plan(route, counts, n_tiles)
            xs = _dispatch(h, p1, p2, n_tiles * MOE_TILE)
            ys = _moe_ffn(xs, tile_expert, n_valid, moe_w_gate[j].astype(BF16),
                          moe_w_up[j].astype(BF16), moe_w_down[j].astype(BF16))
            x2 = _combine(ys, p1, p2, route, x2, gpost, mod, seq)
    return x2.reshape(batch, seq, d)
```

```python
import functools

import jax
import jax.numpy as jnp
from jax import lax
from jax.experimental import pallas as pl
from jax.experimental.pallas import tpu as pltpu

F32 = jnp.float32
BF16 = jnp.bfloat16

LANES = 128
HEAD_W = 64
MOBA_BLOCK = 256
MOBA_TOPK = 3
MOBA_PAIRS = 4
MLSTM_HEADS = 4
MLSTM_CHUNK = 256
MLSTM_CONV = 4
MLSTM_GATE_CAP = 15.0
SWA_WINDOW = 128
SWA_GROUPS = 2
SWA_GROUP_HEADS = 8
N_EXPERTS = 8
RMS_EPS = 1e-6
NEG_INF = float("-inf")
MASK_NEG = -1e30
MOE_TILE = 512
TOP_K = 2
RT_E1, RT_E2, RT_W1, RT_W2, RT_R1, RT_R2 = range(6)
VMEM_LIMIT = 56 * 1024 * 1024

PB_SWA_Q = 0
PB_MOBA_Q = 8
PB_MOBA_K = 12
PB_MOBA_V = 16
PB_ML_Q = 20
PB_ML_K = 24
PB_ML_V = 28
PB_ML_O = 32
PB_ML_G = 36
PB_SWA_K = 38
PB_SWA_V = 40
PROJ_BLOCKS = 42


def _params(sem):
    return pltpu.CompilerParams(dimension_semantics=sem, vmem_limit_bytes=VMEM_LIMIT)


def _rms(x, g):
    return x * lax.rsqrt(jnp.mean(x * x, axis=-1, keepdims=True) + RMS_EPS) * g


def _dot(a, b):
    return jnp.dot(a, b, preferred_element_type=F32)


def _dot_nt(a, b):
    return lax.dot_general(a, b, (((1,), (1,)), ((), ())), preferred_element_type=F32)


def _adaln_kernel(c_ref, w_ref, b_ref, o_ref):
    c = c_ref[...]
    cond = (c * jax.nn.sigmoid(c)).astype(BF16)
    o_ref[...] = _dot(cond, w_ref[...].astype(BF16)) + b_ref[...]


def _adaln(c, ada_w, ada_b):
    depth, d, n6 = ada_w.shape
    b = c.shape[0]
    tn = 1024
    return pl.pallas_call(
        _adaln_kernel,
        grid=(depth, n6 // tn),
        in_specs=[
            pl.BlockSpec((b, d), lambda l, j: (0, 0)),
            pl.BlockSpec((None, d, tn), lambda l, j: (l, 0, j)),
            pl.BlockSpec((None, 1, tn), lambda l, j: (l, 0, j)),
        ],
        out_specs=pl.BlockSpec((None, b, tn), lambda l, j: (l, 0, j)),
        out_shape=jax.ShapeDtypeStruct((depth, b, n6), F32),
        compiler_params=_params(("arbitrary", "arbitrary")),
        name="adaln",
    )(c, ada_w, ada_b.reshape(depth, 1, n6))


def _inproj_kernel(x_ref, g_ref, sc_ref, sh_ref, w_ref, o_ref, *, tn):
    h = (_rms(x_ref[...], g_ref[...]) * (1.0 + sc_ref[...]) + sh_ref[...]).astype(BF16)
    for c in range(w_ref.shape[1] // tn):
        o_ref[:, c * tn:(c + 1) * tn] = _dot(h, w_ref[:, c * tn:(c + 1) * tn])


def _inproj(x2, g, mod, w, seq):
    n, d = x2.shape
    nc = w.shape[1]
    tm, tn = 256, 768
    per_b = seq // tm
    return pl.pallas_call(
        functools.partial(_inproj_kernel, tn=tn),
        grid=(n // tm,),
        in_specs=[
            pl.BlockSpec((tm, d), lambda i: (i, 0)),
            pl.BlockSpec((1, d), lambda i: (0, 0)),
            pl.BlockSpec((None, None, 1, d), lambda i: (i // per_b, 1, 0, 0)),
            pl.BlockSpec((None, None, 1, d), lambda i: (i // per_b, 0, 0, 0)),
            pl.BlockSpec((d, nc), lambda i: (0, 0), pipeline_mode=pl.Buffered(1)),
        ],
        out_specs=pl.BlockSpec((tm, nc), lambda i: (i, 0)),
        out_shape=jax.ShapeDtypeStruct((n, nc), F32),
        compiler_params=_params(("arbitrary",)),
        name="inproj",
    )(x2, g, mod, mod, w)


def _moba_kernel(q_ref, k_ref, v_ref, o_ref, ka_scr, vb_scr):
    seq = k_ref.shape[0]
    blk = MOBA_BLOCK
    n_blk = seq // blk
    lane = lax.broadcasted_iota(jnp.int32, (1, LANES), 1)
    k = k_ref[...]
    vb_scr[...] = v_ref[...].astype(BF16)
    kmean = jnp.mean(k.reshape(n_blk, blk, LANES), axis=1)
    kblk = lax.shift_right_logical(lax.broadcasted_iota(jnp.int32, (seq, 1), 0), 8)
    ri = lax.broadcasted_iota(jnp.int32, (blk, blk), 0)
    ci = lax.broadcasted_iota(jnp.int32, (blk, blk), 1)
    causal_bias = jnp.where(ri >= ci, 0.0, MASK_NEG)
    scale = HEAD_W ** -0.5
    in_head, slot, kmp = [], [], []
    for hh in range(2):
        base = HEAD_W * (1 - hh)
        in_head.append((lane >= HEAD_W * hh) & (lane < HEAD_W * (hh + 1)))
        slot.append(lane - base)
        ka_scr[hh] = jnp.where(in_head[hh], k, jnp.where(slot[hh] == kblk, 1.0, 0.0)).astype(BF16)
        rows = [jnp.where(in_head[hh], kmean, 0.0), jnp.zeros((LANES - base - n_blk, LANES), F32)]
        if base:
            rows.insert(0, jnp.zeros((base, LANES), F32))
        kmp.append(jnp.concatenate(rows, axis=0).astype(BF16))

    for i in range(n_blk):
        q = q_ref[i * blk:(i + 1) * blk, :]
        outs = []
        for hh in range(2):
            base = HEAD_W * (1 - hh)
            qa = jnp.where(in_head[hh], q * scale, 0.0)
            if i > MOBA_TOPK:
                valid = (slot[hh] >= 0) & (slot[hh] < i)
                qm = jnp.where(in_head[hh], q, 0.0).astype(BF16)
                gate = jnp.where(valid, _dot_nt(qm, kmp[hh]), NEG_INF)
                beaten_by = jnp.zeros((blk, LANES), jnp.int32)
                for j in range(i):
                    gj = gate[:, base + j:base + j + 1]
                    beats = (gj > gate) | ((gj == gate) & (slot[hh] > j))
                    beaten_by = beaten_by + beats.astype(jnp.int32)
                qa = jnp.where(valid & (beaten_by >= MOBA_TOPK), MASK_NEG, qa)
            s = _dot_nt(qa.astype(BF16), ka_scr[hh, 0:(i + 1) * blk, :])
            s_own = s[:, i * blk:] + causal_bias
            m = jnp.max(s_own, axis=-1, keepdims=True)
            if i:
                s_past = s[:, :i * blk]
                m = jnp.maximum(m, jnp.max(s_past, axis=-1, keepdims=True))
            p_own = jnp.exp(s_own - m)
            l = jnp.sum(p_own, axis=-1, keepdims=True)
            acc = _dot(p_own.astype(BF16), vb_scr[i * blk:(i + 1) * blk, :])
            if i:
                p_past = jnp.exp(s_past - m)
                l = l + jnp.sum(p_past, axis=-1, keepdims=True)
                acc = acc + _dot(p_past.astype(BF16), vb_scr[0:i * blk, :])
            outs.append(acc / l)
        o_ref[i * blk:(i + 1) * blk, :] = jnp.where(lane < HEAD_W, outs[0], outs[1]).astype(o_ref.dtype)


def _moba(proj, batch, seq):
    n = proj.shape[0]
    return pl.pallas_call(
        _moba_kernel,
        grid=(batch, MOBA_PAIRS),
        in_specs=[
            pl.BlockSpec((seq, LANES), lambda b, p: (b, PB_MOBA_Q + p)),
            pl.BlockSpec((seq, LANES), lambda b, p: (b, PB_MOBA_K + p)),
            pl.BlockSpec((seq, LANES), lambda b, p: (b, PB_MOBA_V + p)),
        ],
        out_specs=pl.BlockSpec((seq, LANES), lambda b, p: (b, p)),
        out_shape=jax.ShapeDtypeStruct((n, MOBA_PAIRS * LANES), BF16),
        scratch_shapes=[
            pltpu.VMEM((2, seq, LANES), BF16),
            pltpu.VMEM((seq, LANES), BF16),
        ],
        compiler_params=_params(("arbitrary", "arbitrary")),
        name="moba",
    )(proj, proj, proj)


def _causal_conv_silu(x, tail, w, b):
    row = lax.broadcasted_iota(jnp.int32, (8, 1), 0)
    y = b + w[MLSTM_CONV - 1:MLSTM_CONV, :] * x
    for shift in range(1, MLSTM_CONV):
        xr = pltpu.roll(x, shift, axis=0)
        head = jnp.where(row < shift, pltpu.roll(tail, shift, axis=0), xr[0:8, :])
        xs = jnp.concatenate([head, xr[8:, :]], axis=0)
        y = y + w[MLSTM_CONV - 1 - shift:MLSTM_CONV - shift, :] * xs
    return y * jax.nn.sigmoid(y)


def _mlstm_kernel(q_ref, k_ref, v_ref, og_ref, gate_ref, cwq_ref, cwk_ref, cbq_ref, cbk_ref,
                  gb_ref, nw_ref, o_ref, qtail_scr, ktail_scr, c_scr, n_scr, m_scr):
    ts = L = q_ref.shape[0]
    lane = lax.broadcasted_iota(jnp.int32, (1, LANES), 1)
    ri = lax.broadcasted_iota(jnp.int32, (L, L), 0)
    ci = lax.broadcasted_iota(jnp.int32, (L, L), 1)
    causal = ri >= ci
    lower = causal.astype(F32)
    upper = (ri <= ci).astype(F32)

    @pl.when(pl.program_id(1) == 0)
    def _():
        qtail_scr[...] = jnp.zeros_like(qtail_scr)
        ktail_scr[...] = jnp.zeros_like(ktail_scr)
        c_scr[...] = jnp.zeros_like(c_scr)
        n_scr[...] = jnp.zeros_like(n_scr)
        m_scr[...] = jnp.zeros_like(m_scr)

    t = MLSTM_GATE_CAP * jnp.tanh((gate_ref[...] + gb_ref[...]) / MLSTM_GATE_CAP)
    a_col = jnp.where(lane < MLSTM_HEADS, t, jax.nn.log_sigmoid(t))
    a_row = a_col.T[0:8, :]

    xq, xk = q_ref[...], k_ref[...]
    qc = _causal_conv_silu(xq, qtail_scr[...], cwq_ref[...], cbq_ref[...]).astype(BF16)
    kc = (_causal_conv_silu(xk, ktail_scr[...], cwk_ref[...], cbk_ref[...])
          * (LANES ** -0.5)).astype(BF16)
    qtail_scr[...] = xq[ts - 8:, :]
    ktail_scr[...] = xk[ts - 8:, :]

    b_c = jnp.dot(lower, a_col, precision=lax.Precision.HIGHEST, preferred_element_type=F32)
    b_r = jnp.dot(a_row, upper, precision=lax.Precision.HIGHEST, preferred_element_type=F32)

    local = []
    for hd in range(MLSTM_HEADS):
        cols = slice(hd * LANES, (hd + 1) * LANES)
        q, k = qc[:, cols], kc[:, cols]
        v = v_ref[:, cols].astype(BF16)
        fl = hd + MLSTM_HEADS
        b_col, li_col = b_c[:, fl:fl + 1], a_col[:, hd:hd + 1]
        b_row, li_row = b_r[fl:fl + 1, :], a_row[hd:hd + 1, :]
        b_last = b_row[:, L - 1:L]
        d_log = jnp.where(causal, b_col - b_row + li_row, NEG_INF)
        d_max = jnp.max(d_log, axis=1, keepdims=True)
        qk = _dot_nt(q, k) * jnp.exp(d_log - d_max)
        pv = _dot(qk.astype(BF16), v)
        qk_sum = jnp.sum(qk, axis=1, keepdims=True)
        a_max = jnp.max(b_last - b_row + li_row, axis=1, keepdims=True)
        kw = k.astype(F32) * jnp.exp(b_last - b_col + li_col - a_max)
        c_in = _dot(kw.T.astype(BF16), v)
        n_in = jnp.sum(kw, axis=0, keepdims=True)
        local.append((q, b_col, b_last, d_max, pv, qk_sum, a_max, c_in, n_in))

    for hd in range(MLSTM_HEADS):
        cols = slice(hd * LANES, (hd + 1) * LANES)
        q, b_col, b_last, d_max, pv, qk_sum, a_max, c_in, n_in = local[hd]
        c_st, n_st, m_st = c_scr[hd], n_scr[hd], m_scr[hd][:, 0:1]
        inter_log = b_col + m_st
        m_out = jnp.maximum(inter_log, d_max)
        w_inter = jnp.exp(inter_log - m_out)
        w_local = jnp.exp(d_max - m_out)
        num = w_local * pv + w_inter * _dot(q, c_st.astype(BF16))
        den = w_local * qk_sum + w_inter * jnp.sum(q.astype(F32) * n_st, axis=1, keepdims=True)
        hv = num / jnp.maximum(jnp.abs(den), jnp.exp(-m_out))
        hn = _rms(hv, nw_ref[:, cols])
        o_ref[:, cols] = (jax.nn.sigmoid(og_ref[:, cols]) * hn).astype(o_ref.dtype)
        m_new = jnp.maximum(b_last + m_st, a_max)
        decay = jnp.exp(b_last + m_st - m_new)
        inject = jnp.exp(a_max - m_new)
        c_scr[hd] = decay * c_st + inject * c_in
        n_scr[hd] = decay * n_st + inject * n_in
        m_scr[hd] = jnp.broadcast_to(m_new, (1, LANES))


def _mlstm(proj, conv_w, conv_b, gate_bias, norm_w, batch, seq):
    n = proj.shape[0]
    hw = MLSTM_HEADS
    gw = hw * LANES
    ts = MLSTM_CHUNK
    tiles = seq // ts

    def col(base):
        return pl.BlockSpec((ts, gw), lambda b, t: (b * tiles + t, base // hw))

    def vec(rows, blk):
        return pl.BlockSpec((rows, gw), lambda b, t: (0, blk))

    return pl.pallas_call(
        _mlstm_kernel,
        grid=(batch, tiles),
        in_specs=[
            col(PB_ML_Q), col(PB_ML_K), col(PB_ML_V), col(PB_ML_O),
            pl.BlockSpec((ts, LANES), lambda b, t: (b * tiles + t, PB_ML_G)),
            vec(MLSTM_CONV, 0), vec(MLSTM_CONV, 1), vec(1, 0), vec(1, 1),
            pl.BlockSpec((1, LANES), lambda b, t: (0, 0)),
            vec(1, 0),
        ],
        out_specs=pl.BlockSpec((ts, gw), lambda b, t: (b * tiles + t, 0)),
        out_shape=jax.ShapeDtypeStruct((n, gw), BF16),
        scratch_shapes=[
            pltpu.VMEM((8, gw), F32),
            pltpu.VMEM((8, gw), F32),
            pltpu.VMEM((hw, LANES, LANES), F32),
            pltpu.VMEM((hw, 1, LANES), F32),
            pltpu.VMEM((hw, 1, LANES), F32),
        ],
        compiler_params=_params(("arbitrary", "arbitrary")),
        name="mlstm",
    )(proj, proj, proj, proj, proj, conv_w, conv_w, conv_b, conv_b, gate_bias, norm_w)


def _swa_kernel(sink_ref, q_ref, k_ref, v_ref, o_ref, kb_scr, vb_scr):
    g = pl.program_id(1)
    seq = q_ref.shape[0]
    W = SWA_WINDOW
    n_pairs = SWA_GROUP_HEADS // 2
    kb_scr[...] = k_ref[...].astype(BF16)
    vb_scr[...] = v_ref[...].astype(BF16)
    lane = lax.broadcasted_iota(jnp.int32, (1, LANES), 1)
    low = lane < HEAD_W
    scale = HEAD_W ** -0.5
    sinks = [sink_ref[g * SWA_GROUP_HEADS + hd] for hd in range(SWA_GROUP_HEADS)]
    r = lax.broadcasted_iota(jnp.int32, (W, 2 * W), 0)
    c = lax.broadcasted_iota(jnp.int32, (W, 2 * W), 1)
    bias_first = jnp.where(c <= r, 0.0, MASK_NEG)
    bias_rest = jnp.where((c > r) & (c <= r + W), 0.0, MASK_NEG)

    def block(nb, carry):
        start = pl.multiple_of(jnp.maximum(nb - 1, 0) * W, W)
        r0 = pl.multiple_of(nb * W, W)
        qn = q_ref[pl.ds(r0, W), :] * scale
        kband = kb_scr[pl.ds(start, 2 * W), :]
        vband = vb_scr[pl.ds(start, 2 * W), :]
        parts = []
        for p in range(n_pairs):
            qp = qn[:, p * LANES:(p + 1) * LANES]
            parts.append(jnp.where(low, qp, 0.0).astype(BF16))
            parts.append(jnp.where(low, 0.0, qp).astype(BF16))
        s_all = _dot_nt(jnp.concatenate(parts, axis=0), kband)
        bias = jnp.where(nb == 0, bias_first, bias_rest)
        probs, inv_l = [], []
        for hd in range(SWA_GROUP_HEADS):
            s = s_all[hd * W:(hd + 1) * W, :] + bias
            m = jnp.maximum(jnp.max(s, axis=-1, keepdims=True), sinks[hd])
            e = jnp.exp(s - m)
            inv_l.append(1.0 / (jnp.sum(e, axis=-1, keepdims=True) + jnp.exp(sinks[hd] - m)))
            probs.append(e.astype(BF16))
        o_all = _dot(jnp.concatenate(probs, axis=0), vband)
        outs = []
        for p in range(n_pairs):
            lo = o_all[(2 * p) * W:(2 * p + 1) * W, :] * inv_l[2 * p]
            hi = o_all[(2 * p + 1) * W:(2 * p + 2) * W, :] * inv_l[2 * p + 1]
            outs.append(jnp.where(low, lo, hi))
        o_ref[pl.ds(r0, W), :] = jnp.concatenate(outs, axis=1).astype(o_ref.dtype)
        return carry

    lax.fori_loop(0, seq // W, block, 0)


def _swa(proj, sinks, batch, seq):
    n = proj.shape[0]
    gw = SWA_GROUP_HEADS * HEAD_W
    gb = gw // LANES
    return pl.pallas_call(
        _swa_kernel,
        grid=(batch, SWA_GROUPS),
        in_specs=[
            pl.BlockSpec(memory_space=pltpu.SMEM),
            pl.BlockSpec((seq, gw), lambda b, g: (b, PB_SWA_Q // gb + g)),
            pl.BlockSpec((seq, LANES), lambda b, g: (b, PB_SWA_K + g)),
            pl.BlockSpec((seq, LANES), lambda b, g: (b, PB_SWA_V + g)),
        ],
        out_specs=pl.BlockSpec((seq, gw), lambda b, g: (b, g)),
        out_shape=jax.ShapeDtypeStruct((n, SWA_GROUPS * gw), BF16),
        scratch_shapes=[pltpu.VMEM((seq, LANES), BF16), pltpu.VMEM((seq, LANES), BF16)],
        compiler_params=_params(("arbitrary", "arbitrary")),
        name="swa",
    )(sinks, proj, proj, proj)


def _outproj_kernel(*refs, with_router):
    if with_router:
        (ya_ref, yb_ref, yc_ref, w_ref, x_ref, gpost_ref, gt_ref, gpre_ref, sc_ref, sh_ref,
         wr_ref, xo_ref, h_ref, route_ref, count_ref, count_scr) = refs
    else:
        (ya_ref, yb_ref, yc_ref, w_ref, x_ref, gpost_ref, gt_ref, gpre_ref, sc_ref, sh_ref,
         xo_ref, h_ref) = refs
    wa = ya_ref.shape[1]
    wb = yb_ref.shape[1]
    y = _dot(ya_ref[...], w_ref[0:wa, :])
    y = y + _dot(yb_ref[...], w_ref[wa:wa + wb, :])
    y = y + _dot(yc_ref[...], w_ref[wa + wb:, :])
    xn = x_ref[...] + gt_ref[...] * _rms(y, gpost_ref[...])
    xo_ref[...] = xn
    hb = (_rms(xn, gpre_ref[...]) * (1.0 + sc_ref[...]) + sh_ref[...]).astype(BF16)
    h_ref[...] = hb.astype(h_ref.dtype)
    if with_router:
        @pl.when(pl.program_id(0) == 0)
        def _():
            count_scr[...] = jnp.zeros_like(count_scr)

        tm = hb.shape[0]
        lane = lax.broadcasted_iota(jnp.int32, (1, LANES), 1)
        logits = jnp.where(lane < N_EXPERTS, _dot(hb, wr_ref[...]), NEG_INF)
        m1 = jnp.max(logits, axis=-1, keepdims=True)
        i1 = jnp.min(jnp.where(logits == m1, lane, LANES), axis=-1, keepdims=True)
        rest = jnp.where(lane == i1, NEG_INF, logits)
        m2 = jnp.max(rest, axis=-1, keepdims=True)
        i2 = jnp.min(jnp.where(rest == m2, lane, LANES), axis=-1, keepdims=True)
        e2 = jnp.exp(m2 - m1)
        w1 = 1.0 / (1.0 + e2)
        picked = jnp.where((lane == i1) | (lane == i2), 1.0, 0.0)
        ri = lax.broadcasted_iota(jnp.int32, (tm, tm), 0)
        ci = lax.broadcasted_iota(jnp.int32, (tm, tm), 1)
        before = jnp.where(ri > ci, 1.0, 0.0).astype(BF16)
        rank = _dot(before, picked.astype(BF16)) + count_scr[...]
        r1 = jnp.sum(jnp.where(lane == i1, rank, 0.0), axis=-1, keepdims=True)
        r2 = jnp.sum(jnp.where(lane == i2, rank, 0.0), axis=-1, keepdims=True)
        count_scr[...] += jnp.sum(picked, axis=0, keepdims=True)
        count_ref[...] = jnp.broadcast_to(count_scr[...], count_ref.shape)
        rec = jnp.where(lane == RT_E1, i1.astype(F32), 0.0)
        for slot_lane, val in ((RT_E2, i2.astype(F32)), (RT_W1, w1), (RT_W2, e2 * w1),
                               (RT_R1, r1), (RT_R2, r2)):
            rec = jnp.where(lane == slot_lane, val, rec)
        route_ref[...] = rec


def _outproj(ya, yb, yc, w, x2, gpost, gpre, mod, seq, w_router=None):
    n, d = x2.shape
    tm = 256
    per_b = seq // tm
    with_router = w_router is not None

    def rows(width):
        return pl.BlockSpec((tm, width), lambda i: (i, 0))

    def vec():
        return pl.BlockSpec((1, d), lambda i: (0, 0))

    def modrow(k):
        return pl.BlockSpec((None, None, 1, d), lambda i: (i // per_b, k, 0, 0))

    in_specs = [rows(ya.shape[1]), rows(yb.shape[1]), rows(yc.shape[1]),
                pl.BlockSpec((d, d), lambda i: (0, 0)), rows(d),
                vec(), modrow(2), vec(), modrow(4), modrow(3)]
    args = [ya, yb, yc, w, x2, gpost, mod, gpre, mod, mod]
    out_specs = [rows(d), rows(d)]
    out_shape = [jax.ShapeDtypeStruct((n, d), F32),
                 jax.ShapeDtypeStruct((n, d), F32 if with_router else BF16)]
    scratch = []
    if with_router:
        in_specs.append(pl.BlockSpec((d, LANES), lambda i: (0, 0)))
        args.append(w_router)
        out_specs += [rows(LANES), pl.BlockSpec((8, LANES), lambda i: (0, 0))]
        out_shape += [jax.ShapeDtypeStruct((n, LANES), F32), jax.ShapeDtypeStruct((8, LANES), F32)]
        scratch.append(pltpu.VMEM((1, LANES), F32))
    return pl.pallas_call(
        functools.partial(_outproj_kernel, with_router=with_router),
        grid=(n // tm,),
        in_specs=in_specs,
        out_specs=out_specs,
        out_shape=out_shape,
        scratch_shapes=scratch,
        compiler_params=_params(("arbitrary",)),
        name="outproj",
    )(*args)


def _swiglu_step(h, wg_ref, wu_ref, wd_ref):
    gate = _dot(h, wg_ref[...])
    a = gate * jax.nn.sigmoid(gate) * _dot(h, wu_ref[...])
    return _dot(a.astype(BF16), wd_ref[...])


def _ffn_kernel(*refs, n_cast):
    h_ref, wg_ref, wu_ref, wd_ref, x_ref, gpost_ref, gt_ref = refs[:7]
    cast_in = refs[7:7 + n_cast]
    o_ref = refs[7 + n_cast]
    cast_out = refs[8 + n_cast:]
    f = pl.program_id(1)

    @pl.when(f == 0)
    def _():
        o_ref[...] = jnp.zeros_like(o_ref)

    o_ref[...] += _swiglu_step(h_ref[...], wg_ref, wu_ref, wd_ref)

    @pl.when(f == pl.num_programs(1) - 1)
    def _():
        o_ref[...] = x_ref[...] + gt_ref[...] * _rms(o_ref[...], gpost_ref[...])

    for src, dst in zip(cast_in, cast_out):
        dst[...] = src[...].astype(BF16)


def _cast_slices(shape, gi, gf):
    rows, cols = shape
    if rows % (16 * gi) == 0 and cols % (LANES * gf) == 0:
        return (rows // gi, cols // gf), lambda i, f: (i, f)
    steps = gi * gf
    assert rows % (16 * steps) == 0, shape
    return (rows // steps, cols), lambda i, f: (i * gf + f, 0)


def _ffn(h, wg, wu, wd, x2, gpost, mod, seq, to_cast=()):
    n, d = x2.shape
    tm, tf = 512, 512
    per_b = seq // tm
    dff = wg.shape[1]
    gi, gf = n // tm, dff // tf
    row = lambda i, f: (i, 0)
    cast_specs = [pl.BlockSpec(*_cast_slices(a.shape, gi, gf)) for a in to_cast]
    outs = pl.pallas_call(
        functools.partial(_ffn_kernel, n_cast=len(to_cast)),
        grid=(gi, gf),
        in_specs=[
            pl.BlockSpec((tm, d), row),
            pl.BlockSpec((d, tf), lambda i, f: (0, f)),
            pl.BlockSpec((d, tf), lambda i, f: (0, f)),
            pl.BlockSpec((tf, d), lambda i, f: (f, 0)),
            pl.BlockSpec((tm, d), row),
            pl.BlockSpec((1, d), lambda i, f: (0, 0)),
            pl.BlockSpec((None, None, 1, d), lambda i, f: (i // per_b, 5, 0, 0)),
        ] + cast_specs,
        out_specs=[pl.BlockSpec((tm, d), row)] + cast_specs,
        out_shape=[jax.ShapeDtypeStruct((n, d), F32)]
        + [jax.ShapeDtypeStruct(a.shape, BF16) for a in to_cast],
        compiler_params=_params(("arbitrary", "arbitrary")),
        name="ffn",
    )(h, wg, wu, wd, x2, gpost, mod, *to_cast)
    return outs[0], tuple(outs[1:])


def _dispatch_kernel(p1_ref, p2_ref, fill_ref, h_ref, xs_ref, zero_scr, sems, fill_sem, *, n_rows):
    tm = h_ref.shape[0]
    base = pl.program_id(0) * tm

    @pl.when(pl.program_id(0) == 0)
    def _():
        zero_scr[...] = jnp.zeros_like(zero_scr)
        tail = [pltpu.make_async_copy(zero_scr.at[pl.ds(0, MOE_TILE)],
                                      xs_ref.at[pl.ds(t0, MOE_TILE)], fill_sem)
                for t0 in range(n_rows, xs_ref.shape[0], MOE_TILE)]
        for cp in tail:
            cp.start()
        for cp in tail:
            cp.wait()
        for e in range(N_EXPERTS):
            cp = pltpu.make_async_copy(
                zero_scr, xs_ref.at[pl.ds(pl.multiple_of(fill_ref[e], 8), zero_scr.shape[0])],
                fill_sem)
            cp.start()
            cp.wait()

    def issue(r, carry):
        row = h_ref.at[pl.ds(r, 1)]
        pltpu.make_async_copy(row, xs_ref.at[pl.ds(p1_ref[base + r], 1)], sems.at[0]).start()
        pltpu.make_async_copy(row, xs_ref.at[pl.ds(p2_ref[base + r], 1)], sems.at[1]).start()
        return carry

    lax.fori_loop(0, tm, issue, 0, unroll=8)
    for k in range(TOP_K):
        pltpu.make_async_copy(h_ref, xs_ref.at[pl.ds(0, tm)], sems.at[k]).wait()


def _dispatch(h, p1, p2, fill_start, n_slots):
    n, d = h.shape
    tm = 512
    return pl.pallas_call(
        functools.partial(_dispatch_kernel, n_rows=TOP_K * n),
        grid_spec=pltpu.PrefetchScalarGridSpec(
            num_scalar_prefetch=3,
            grid=(n // tm,),
            in_specs=[pl.BlockSpec((tm, d), lambda i, p1, p2, fs: (i, 0))],
            out_specs=pl.BlockSpec(memory_space=pl.ANY),
            scratch_shapes=[pltpu.VMEM((MOE_TILE + 8, d), F32),
                            pltpu.SemaphoreType.DMA((TOP_K,)),
                            pltpu.SemaphoreType.DMA(())],
        ),
        out_shape=jax.ShapeDtypeStruct((n_slots, d), F32),
        compiler_params=_params(("arbitrary",)),
        name="moe_dispatch",
    )(p1, p2, fill_start, h)


def _moe_ffn_kernel(te_ref, nv_ref, x_ref, wg_ref, wu_ref, wd_ref, y_ref, h_scr):
    del te_ref
    f = pl.program_id(1)
    used = pl.program_id(0) < nv_ref[0]

    @pl.when(f == 0)
    def _():
        y_ref[...] = jnp.zeros_like(y_ref)

    @pl.when(used & (f == 0))
    def _():
        h_scr[...] = x_ref[...].astype(BF16)

    @pl.when(used)
    def _():
        y_ref[...] += _swiglu_step(h_scr[...], wg_ref, wu_ref, wd_ref)


def _moe_ffn(xs, tile_expert, n_valid, wg, wu, wd):
    n_slots, d = xs.shape
    dff = wg.shape[2]
    tm, tf = MOE_TILE, 256
    nf = dff // tf

    def row(j, f, te, nv):
        return (jnp.minimum(j, nv[0] - 1), 0)

    def fcol(j, f, nv):
        return jnp.where(j < nv[0], f, nf - 1)

    return pl.pallas_call(
        _moe_ffn_kernel,
        grid_spec=pltpu.PrefetchScalarGridSpec(
            num_scalar_prefetch=2,
            grid=(n_slots // tm, nf),
            in_specs=[
                pl.BlockSpec((tm, d), row),
                pl.BlockSpec((None, d, tf), lambda j, f, te, nv: (te[j], 0, fcol(j, f, nv))),
                pl.BlockSpec((None, d, tf), lambda j, f, te, nv: (te[j], 0, fcol(j, f, nv))),
                pl.BlockSpec((None, tf, d), lambda j, f, te, nv: (te[j], fcol(j, f, nv), 0)),
            ],
            out_specs=pl.BlockSpec((tm, d), lambda j, f, te, nv: (j, 0)),
            scratch_shapes=[pltpu.VMEM((tm, d), BF16)],
        ),
        out_shape=jax.ShapeDtypeStruct((n_slots, d), F32),
        compiler_params=_params(("arbitrary", "arbitrary")),
        name="moe_ffn",
    )(tile_expert, n_valid, xs, wg, wu, wd)


def _combine_kernel(p1_ref, p2_ref, ys_ref, route_ref, x_ref, gpost_ref, gt_ref, o_ref,
                    y1_scr, y2_scr, sems):
    tm = x_ref.shape[0]
    base = pl.program_id(0) * tm

    def issue(r, carry):
        pltpu.make_async_copy(ys_ref.at[pl.ds(p1_ref[base + r], 1)], y1_scr.at[pl.ds(r, 1)],
                              sems.at[0]).start()
        pltpu.make_async_copy(ys_ref.at[pl.ds(p2_ref[base + r], 1)], y2_scr.at[pl.ds(r, 1)],
                              sems.at[1]).start()
        return carry


    lax.fori_loop(0, tm, issue, 0, unroll=8)
    pltpu.make_async_copy(ys_ref.at[pl.ds(0, tm)], y1_scr, sems.at[0]).wait()
    pltpu.make_async_copy(ys_ref.at[pl.ds(0, tm)], y2_scr, sems.at[1]).wait()
    lane = lax.broadcasted_iota(jnp.int32, (1, LANES), 1)
    route = route_ref[...]
    w1 = jnp.sum(jnp.where(lane == RT_W1, route, 0.0), axis=-1, keepdims=True)
    w2 = jnp.sum(jnp.where(lane == RT_W2, route, 0.0), axis=-1, keepdims=True)
    y = w1 * y1_scr[...] + w2 * y2_scr[...]
    o_ref[...] = x_ref[...] + gt_ref[...] * _rms(y, gpost_ref[...])


def _combine(ys, p1, p2, route, x2, gpost, mod, seq):
    n, d = x2.shape
    tm = 512
    per_b = seq // tm
    row = lambda i, p1, p2: (i, 0)
    return pl.pallas_call(
        _combine_kernel,
        grid_spec=pltpu.PrefetchScalarGridSpec(
            num_scalar_prefetch=2,
            grid=(n // tm,),
            in_specs=[
                pl.BlockSpec(memory_space=pl.ANY),
                pl.BlockSpec((tm, LANES), row),
                pl.BlockSpec((tm, d), row),
                pl.BlockSpec((1, d), lambda i, p1, p2: (0, 0)),
                pl.BlockSpec((None, None, 1, d), lambda i, p1, p2: (i // per_b, 5, 0, 0)),
            ],
            out_specs=pl.BlockSpec((tm, d), row),
            scratch_shapes=[pltpu.VMEM((tm, d), F32), pltpu.VMEM((tm, d), F32),
                            pltpu.SemaphoreType.DMA((TOP_K,))],
        ),
        out_shape=jax.ShapeDtypeStruct((n, d), F32),
        compiler_params=_params(("arbitrary",)),
        name="moe_combine",
    )(p1, p2, ys, route, x2, gpost, mod)


def _route_plan(route, counts, n_tiles):
    e1 = route[:, RT_E1].astype(jnp.int32)
    e2 = route[:, RT_E2].astype(jnp.int32)
    cnt = counts[0, :N_EXPERTS].astype(jnp.int32)
    size = (cnt + MOE_TILE - 1) // MOE_TILE * MOE_TILE
    end = jnp.cumsum(size)
    start = end - size
    p1 = start[e1] + route[:, RT_R1].astype(jnp.int32)
    p2 = start[e2] + route[:, RT_R2].astype(jnp.int32)
    n_valid = end[-1] // MOE_TILE
    tile_start = jnp.minimum(jnp.arange(n_tiles), n_valid - 1) * MOE_TILE
    tile_expert = jnp.sum(tile_start[:, None] >= end[None, :], axis=1).astype(jnp.int32)
    fill_start = (start + cnt) // 8 * 8
    return p1, p2, fill_start, tile_expert, n_valid.reshape(1).astype(jnp.int32)


def _pack_w_in(w_in):
    d = w_in.shape[0]
    gw = 4 * LANES
    offs = [0]
    for width in (gw, gw, gw, gw, gw, gw, gw, MLSTM_HEADS, MLSTM_HEADS,
                  2 * SWA_GROUP_HEADS * HEAD_W, SWA_GROUPS * HEAD_W, SWA_GROUPS * HEAD_W):
        offs.append(offs[-1] + width)
    w_in = w_in.astype(BF16)
    (mb_q, mb_k, mb_v, ml_q, ml_k, ml_v, ml_o, ml_i, ml_f, sw_q, sw_k, sw_v) = [
        w_in[:, offs[j]:offs[j + 1]] for j in range(12)]
    gates = jnp.concatenate(
        [ml_i, ml_f, jnp.zeros((d, 2 * LANES - 2 * MLSTM_HEADS), w_in.dtype)], axis=1)

    def dup(w):
        heads = [w[:, g * HEAD_W:(g + 1) * HEAD_W] for g in range(SWA_GROUPS)]
        return jnp.concatenate([t for hd in heads for t in (hd, hd)], axis=1)

    packed = jnp.concatenate(
        [sw_q, mb_q, mb_k, mb_v, ml_q, ml_k, ml_v, ml_o, gates, dup(sw_k), dup(sw_v)], axis=1)
    assert packed.shape[1] == PROJ_BLOCKS * LANES
    return packed.astype(BF16)


def kernel(x, c, ada_w, ada_b, g_pre_mix, g_post_mix, g_pre_ffn, g_post_ffn, w_in, w_out, conv_w,
           conv_b, igate_b, fgate_b, mlstm_norm_w, swa_sinks, ffn_w_gate, ffn_w_up, ffn_w_down,
           moe_router, moe_w_gate, moe_w_up, moe_w_down):
    batch, seq, d = x.shape
    depth = ada_w.shape[0]
    n = batch * seq
    x2 = x.reshape(n, d)
    mod_all = _adaln(c, ada_w, ada_b).reshape(depth, batch, 6, 1, d)
    for l in range(depth):
        mod = mod_all[l]
        proj = _inproj(x2, g_pre_mix[l].reshape(1, d), mod, _pack_w_in(w_in[l]), seq)
        y_moba = _moba(proj, batch, seq)
        gate_bias = jnp.concatenate(
            [igate_b[l], fgate_b[l], jnp.zeros((LANES - 2 * MLSTM_HEADS,), F32)]).reshape(1, LANES)
        y_mlstm = _mlstm(proj, conv_w[l], conv_b[l].reshape(1, -1), gate_bias,
                         mlstm_norm_w[l].reshape(1, -1), batch, seq)
        y_swa = _swa(proj, swa_sinks[l], batch, seq)
        j = l // 2
        w_router = None
        if l % 2 == 1:
            w_router = jnp.pad(moe_router[j], ((0, 0), (0, LANES - N_EXPERTS))).astype(BF16)
        outs = _outproj(y_moba, y_mlstm, y_swa, w_out[l].astype(BF16), x2,
                        g_post_mix[l].reshape(1, d), g_pre_ffn[l].reshape(1, d), mod, seq, w_router)
        gpost = g_post_ffn[l].reshape(1, d)
        if l % 2 == 0:
            x2, h = outs
            to_cast = ()
            if l + 1 < depth:
                to_cast = tuple(w[(l + 1) // 2].reshape(-1, w.shape[-1])
                                for w in (moe_w_gate, moe_w_up, moe_w_down))
            x2, moe_bf16 = _ffn(h, ffn_w_gate[j].astype(BF16), ffn_w_up[j].astype(BF16),
                                ffn_w_down[j].astype(BF16), x2, gpost, mod, seq, to_cast)
        else:
            x2, h, route, counts = outs
            n_tiles = (TOP_K * n) // MOE_TILE + N_EXPERTS + 2
            p1, p2, fill_start, tile_expert, n_valid = _route_plan(route, counts, n_tiles)
            xs = _dispatch(h, p1, p2, fill_start, n_tiles * MOE_TILE)
            wg, wu, wd = (w.reshape(N_EXPERTS, -1, w.shape[-1]) for w in moe_bf16)
            ys = _moe_ffn(xs, tile_expert, n_valid, wg, wu, wd)
            x2 = _combine(ys, p1, p2, route, x2, gpost, mod, seq)
    return x2.reshape(batch, seq, d)
```

```python
import functools

import jax
import jax.numpy as jnp
from jax import lax
from jax.experimental import pallas as pl
from jax.experimental.pallas import tpu as pltpu

F32 = jnp.float32
BF16 = jnp.bfloat16

LANES = 128
HEAD_W = 64
MOBA_BLOCK = 256
MOBA_TOPK = 3
MOBA_PAIRS = 4
MLSTM_HEADS = 4
MLSTM_CHUNK = 256
MLSTM_CONV = 4
MLSTM_GATE_CAP = 15.0
SWA_WINDOW = 128
SWA_GROUPS = 2
SWA_GROUP_HEADS = 8
N_EXPERTS = 8
RMS_EPS = 1e-6
NEG_INF = float("-inf")
MASK_NEG = -1e30
MOE_TILE = 512
TOP_K = 2
RT_E1, RT_E2, RT_W1, RT_W2, RT_R1, RT_R2 = range(6)
VMEM_LIMIT = 56 * 1024 * 1024

PB_SWA_Q = 0
PB_MOBA_Q = 8
PB_MOBA_K = 12
PB_MOBA_V = 16
PB_ML_Q = 20
PB_ML_K = 24
PB_ML_V = 28
PB_ML_O = 32
PB_ML_G = 36
PB_SWA_K = 38
PB_SWA_V = 40
PROJ_BLOCKS = 42


def _params(sem):
    return pltpu.CompilerParams(dimension_semantics=sem, vmem_limit_bytes=VMEM_LIMIT)


def _rms(x, g):
    return x * lax.rsqrt(jnp.mean(x * x, axis=-1, keepdims=True) + RMS_EPS) * g


def _dot(a, b):
    return jnp.dot(a, b, preferred_element_type=F32)


def _dot_nt(a, b):
    return lax.dot_general(a, b, (((1,), (1,)), ((), ())), preferred_element_type=F32)


def _adaln_kernel(c_ref, w_ref, b_ref, o_ref):
    c = c_ref[...]
    cond = (c * jax.nn.sigmoid(c)).astype(BF16)
    o_ref[...] = _dot(cond, w_ref[...].astype(BF16)) + b_ref[...]


def _adaln(c, ada_w, ada_b):
    depth, d, n6 = ada_w.shape
    b = c.shape[0]
    tn = 1024
    return pl.pallas_call(
        _adaln_kernel,
        grid=(depth, n6 // tn),
        in_specs=[
            pl.BlockSpec((b, d), lambda l, j: (0, 0)),
            pl.BlockSpec((None, d, tn), lambda l, j: (l, 0, j)),
            pl.BlockSpec((None, 1, tn), lambda l, j: (l, 0, j)),
        ],
        out_specs=pl.BlockSpec((None, b, tn), lambda l, j: (l, 0, j)),
        out_shape=jax.ShapeDtypeStruct((depth, b, n6), F32),
        compiler_params=_params(("arbitrary", "arbitrary")),
        name="adaln",
    )(c, ada_w, ada_b.reshape(depth, 1, n6))


def _inproj_kernel(x_ref, g_ref, sc_ref, sh_ref, w_ref, o_ref, *, tn):
    h = (_rms(x_ref[...], g_ref[...]) * (1.0 + sc_ref[...]) + sh_ref[...]).astype(BF16)
    for c in range(w_ref.shape[1] // tn):
        o_ref[:, c * tn:(c + 1) * tn] = _dot(h, w_ref[:, c * tn:(c + 1) * tn])


def _inproj(x2, g, mod, w, seq):
    n, d = x2.shape
    nc = w.shape[1]
    tm, tn = 256, 768
    per_b = seq // tm
    return pl.pallas_call(
        functools.partial(_inproj_kernel, tn=tn),
        grid=(n // tm,),
        in_specs=[
            pl.BlockSpec((tm, d), lambda i: (i, 0)),
            pl.BlockSpec((1, d), lambda i: (0, 0)),
            pl.BlockSpec((None, None, 1, d), lambda i: (i // per_b, 1, 0, 0)),
            pl.BlockSpec((None, None, 1, d), lambda i: (i // per_b, 0, 0, 0)),
            pl.BlockSpec((d, nc), lambda i: (0, 0), pipeline_mode=pl.Buffered(1)),
        ],
        out_specs=pl.BlockSpec((tm, nc), lambda i: (i, 0)),
        out_shape=jax.ShapeDtypeStruct((n, nc), F32),
        compiler_params=_params(("arbitrary",)),
        name="inproj",
    )(x2, g, mod, mod, w)


def _moba_kernel(q_ref, k_ref, v_ref, o_ref, ka_scr, vb_scr):
    seq = k_ref.shape[0]
    blk = MOBA_BLOCK
    n_blk = seq // blk
    lane = lax.broadcasted_iota(jnp.int32, (1, LANES), 1)
    k = k_ref[...]
    vb_scr[...] = v_ref[...].astype(BF16)
    kmean = jnp.mean(k.reshape(n_blk, blk, LANES), axis=1)
    kblk = lax.shift_right_logical(lax.broadcasted_iota(jnp.int32, (seq, 1), 0), 8)
    ri = lax.broadcasted_iota(jnp.int32, (blk, blk), 0)
    ci = lax.broadcasted_iota(jnp.int32, (blk, blk), 1)
    causal_bias = jnp.where(ri >= ci, 0.0, MASK_NEG)
    scale = HEAD_W ** -0.5
    in_head, slot, kmp = [], [], []
    for hh in range(2):
        base = HEAD_W * (1 - hh)
        in_head.append((lane >= HEAD_W * hh) & (lane < HEAD_W * (hh + 1)))
        slot.append(lane - base)
        ka_scr[hh] = jnp.where(in_head[hh], k, jnp.where(slot[hh] == kblk, 1.0, 0.0)).astype(BF16)
        rows = [jnp.where(in_head[hh], kmean, 0.0), jnp.zeros((LANES - base - n_blk, LANES), F32)]
        if base:
            rows.insert(0, jnp.zeros((base, LANES), F32))
        kmp.append(jnp.concatenate(rows, axis=0).astype(BF16))

    for i in range(n_blk):
        q = q_ref[i * blk:(i + 1) * blk, :]
        outs = []
        for hh in range(2):
            base = HEAD_W * (1 - hh)
            qa = jnp.where(in_head[hh], q * scale, 0.0)
            if i > MOBA_TOPK:
                valid = (slot[hh] >= 0) & (slot[hh] < i)
                qm = jnp.where(in_head[hh], q, 0.0).astype(BF16)
                gate = jnp.where(valid, _dot_nt(qm, kmp[hh]), NEG_INF)
                beaten_by = jnp.zeros((blk, LANES), jnp.int32)
                for j in range(i):
                    gj = gate[:, base + j:base + j + 1]
                    beats = (gj > gate) | ((gj == gate) & (slot[hh] > j))
                    beaten_by = beaten_by + beats.astype(jnp.int32)
                qa = jnp.where(valid & (beaten_by >= MOBA_TOPK), MASK_NEG, qa)
            s = _dot_nt(qa.astype(BF16), ka_scr[hh, 0:(i + 1) * blk, :])
            s_own = s[:, i * blk:] + causal_bias
            m = jnp.max(s_own, axis=-1, keepdims=True)
            if i:
                s_past = s[:, :i * blk]
                m = jnp.maximum(m, jnp.max(s_past, axis=-1, keepdims=True))
            p_own = jnp.exp(s_own - m)
            l = jnp.sum(p_own, axis=-1, keepdims=True)
            acc = _dot(p_own.astype(BF16), vb_scr[i * blk:(i + 1) * blk, :])
            if i:
                p_past = jnp.exp(s_past - m)
                l = l + jnp.sum(p_past, axis=-1, keepdims=True)
                acc = acc + _dot(p_past.astype(BF16), vb_scr[0:i * blk, :])
            outs.append(acc / l)
        o_ref[i * blk:(i + 1) * blk, :] = jnp.where(lane < HEAD_W, outs[0], outs[1]).astype(o_ref.dtype)


def _moba(proj, batch, seq):
    n = proj.shape[0]
    return pl.pallas_call(
        _moba_kernel,
        grid=(batch, MOBA_PAIRS),
        in_specs=[
            pl.BlockSpec((seq, LANES), lambda b, p: (b, PB_MOBA_Q + p)),
            pl.BlockSpec((seq, LANES), lambda b, p: (b, PB_MOBA_K + p)),
            pl.BlockSpec((seq, LANES), lambda b, p: (b, PB_MOBA_V + p)),
        ],
        out_specs=pl.BlockSpec((seq, LANES), lambda b, p: (b, p)),
        out_shape=jax.ShapeDtypeStruct((n, MOBA_PAIRS * LANES), BF16),
        scratch_shapes=[
            pltpu.VMEM((2, seq, LANES), BF16),
            pltpu.VMEM((seq, LANES), BF16),
        ],
        compiler_params=_params(("arbitrary", "arbitrary")),
        name="moba",
    )(proj, proj, proj)


def _causal_conv_silu(x, tail, w, b):
    row = lax.broadcasted_iota(jnp.int32, (8, 1), 0)
    y = b + w[MLSTM_CONV - 1:MLSTM_CONV, :] * x
    for shift in range(1, MLSTM_CONV):
        xr = pltpu.roll(x, shift, axis=0)
        head = jnp.where(row < shift, pltpu.roll(tail, shift, axis=0), xr[0:8, :])
        xs = jnp.concatenate([head, xr[8:, :]], axis=0)
        y = y + w[MLSTM_CONV - 1 - shift:MLSTM_CONV - shift, :] * xs
    return y * jax.nn.sigmoid(y)


def _mlstm_kernel(q_ref, k_ref, v_ref, og_ref, gate_ref, cwq_ref, cwk_ref, cbq_ref, cbk_ref,
                  gb_ref, nw_ref, o_ref, qtail_scr, ktail_scr, c_scr, n_scr, m_scr):
    ts = L = q_ref.shape[0]
    lane = lax.broadcasted_iota(jnp.int32, (1, LANES), 1)
    ri = lax.broadcasted_iota(jnp.int32, (L, L), 0)
    ci = lax.broadcasted_iota(jnp.int32, (L, L), 1)
    causal = ri >= ci
    lower = causal.astype(F32)
    upper = (ri <= ci).astype(F32)

    @pl.when(pl.program_id(1) == 0)
    def _():
        qtail_scr[...] = jnp.zeros_like(qtail_scr)
        ktail_scr[...] = jnp.zeros_like(ktail_scr)
        c_scr[...] = jnp.zeros_like(c_scr)
        n_scr[...] = jnp.zeros_like(n_scr)
        m_scr[...] = jnp.zeros_like(m_scr)

    t = MLSTM_GATE_CAP * jnp.tanh((gate_ref[...] + gb_ref[...]) / MLSTM_GATE_CAP)
    a_col = jnp.where(lane < MLSTM_HEADS, t, jax.nn.log_sigmoid(t))
    a_row = a_col.T[0:8, :]

    xq, xk = q_ref[...], k_ref[...]
    qc = _causal_conv_silu(xq, qtail_scr[...], cwq_ref[...], cbq_ref[...]).astype(BF16)
    kc = (_causal_conv_silu(xk, ktail_scr[...], cwk_ref[...], cbk_ref[...])
          * (LANES ** -0.5)).astype(BF16)
    qtail_scr[...] = xq[ts - 8:, :]
    ktail_scr[...] = xk[ts - 8:, :]

    b_c = jnp.dot(lower, a_col, precision=lax.Precision.HIGHEST, preferred_element_type=F32)
    b_r = jnp.dot(a_row, upper, precision=lax.Precision.HIGHEST, preferred_element_type=F32)

    local = []
    for hd in range(MLSTM_HEADS):
        cols = slice(hd * LANES, (hd + 1) * LANES)
        q, k = qc[:, cols], kc[:, cols]
        v = v_ref[:, cols].astype(BF16)
        fl = hd + MLSTM_HEADS
        b_col, li_col = b_c[:, fl:fl + 1], a_col[:, hd:hd + 1]
        b_row, li_row = b_r[fl:fl + 1, :], a_row[hd:hd + 1, :]
        b_last = b_row[:, L - 1:L]
        d_log = jnp.where(causal, b_col - b_row + li_row, NEG_INF)
        d_max = jnp.max(d_log, axis=1, keepdims=True)
        qk = _dot_nt(q, k) * jnp.exp(d_log - d_max)
        pv = _dot(qk.astype(BF16), v)
        qk_sum = jnp.sum(qk, axis=1, keepdims=True)
        a_max = jnp.max(b_last - b_row + li_row, axis=1, keepdims=True)
        kw = k.astype(F32) * jnp.exp(b_last - b_col + li_col - a_max)
        c_in = _dot(kw.T.astype(BF16), v)
        n_in = jnp.sum(kw, axis=0, keepdims=True)
        local.append((q, b_col, b_last, d_max, pv, qk_sum, a_max, c_in, n_in))

    for hd in range(MLSTM_HEADS):
        cols = slice(hd * LANES, (hd + 1) * LANES)
        q, b_col, b_last, d_max, pv, qk_sum, a_max, c_in, n_in = local[hd]
        c_st, n_st, m_st = c_scr[hd], n_scr[hd], m_scr[hd][:, 0:1]
        inter_log = b_col + m_st
        m_out = jnp.maximum(inter_log, d_max)
        w_inter = jnp.exp(inter_log - m_out)
        w_local = jnp.exp(d_max - m_out)
        num = w_local * pv + w_inter * _dot(q, c_st.astype(BF16))
        den = w_local * qk_sum + w_inter * jnp.sum(q.astype(F32) * n_st, axis=1, keepdims=True)
        hv = num / jnp.maximum(jnp.abs(den), jnp.exp(-m_out))
        hn = _rms(hv, nw_ref[:, cols])
        o_ref[:, cols] = (jax.nn.sigmoid(og_ref[:, cols]) * hn).astype(o_ref.dtype)
        m_new = jnp.maximum(b_last + m_st, a_max)
        decay = jnp.exp(b_last + m_st - m_new)
        inject = jnp.exp(a_max - m_new)
        c_scr[hd] = decay * c_st + inject * c_in
        n_scr[hd] = decay * n_st + inject * n_in
        m_scr[hd] = jnp.broadcast_to(m_new, (1, LANES))


def _mlstm(proj, conv_w, conv_b, gate_bias, norm_w, batch, seq):
    n = proj.shape[0]
    hw = MLSTM_HEADS
    gw = hw * LANES
    ts = MLSTM_CHUNK
    tiles = seq // ts

    def col(base):
        return pl.BlockSpec((ts, gw), lambda b, t: (b * tiles + t, base // hw))

    def vec(rows, blk):
        return pl.BlockSpec((rows, gw), lambda b, t: (0, blk))

    return pl.pallas_call(
        _mlstm_kernel,
        grid=(batch, tiles),
        in_specs=[
            col(PB_ML_Q), col(PB_ML_K), col(PB_ML_V), col(PB_ML_O),
            pl.BlockSpec((ts, LANES), lambda b, t: (b * tiles + t, PB_ML_G)),
            vec(MLSTM_CONV, 0), vec(MLSTM_CONV, 1), vec(1, 0), vec(1, 1),
            pl.BlockSpec((1, LANES), lambda b, t: (0, 0)),
            vec(1, 0),
        ],
        out_specs=pl.BlockSpec((ts, gw), lambda b, t: (b * tiles + t, 0)),
        out_shape=jax.ShapeDtypeStruct((n, gw), BF16),
        scratch_shapes=[
            pltpu.VMEM((8, gw), F32),
            pltpu.VMEM((8, gw), F32),
            pltpu.VMEM((hw, LANES, LANES), F32),
            pltpu.VMEM((hw, 1, LANES), F32),
            pltpu.VMEM((hw, 1, LANES), F32),
        ],
        compiler_params=_params(("arbitrary", "arbitrary")),
        name="mlstm",
    )(proj, proj, proj, proj, proj, conv_w, conv_w, conv_b, conv_b, gate_bias, norm_w)


def _swa_kernel(sink_ref, q_ref, k_ref, v_ref, o_ref, kb_scr, vb_scr):
    g = pl.program_id(1)
    seq = q_ref.shape[0]
    W = SWA_WINDOW
    n_pairs = SWA_GROUP_HEADS // 2
    kb_scr[...] = k_ref[...].astype(BF16)
    vb_scr[...] = v_ref[...].astype(BF16)
    lane = lax.broadcasted_iota(jnp.int32, (1, LANES), 1)
    low = lane < HEAD_W
    scale = HEAD_W ** -0.5
    sinks = [sink_ref[g * SWA_GROUP_HEADS + hd] for hd in range(SWA_GROUP_HEADS)]
    r = lax.broadcasted_iota(jnp.int32, (W, 2 * W), 0)
    c = lax.broadcasted_iota(jnp.int32, (W, 2 * W), 1)
    bias_first = jnp.where(c <= r, 0.0, MASK_NEG)
    bias_rest = jnp.where((c > r) & (c <= r + W), 0.0, MASK_NEG)

    def block(nb, carry):
        start = pl.multiple_of(jnp.maximum(nb - 1, 0) * W, W)
        r0 = pl.multiple_of(nb * W, W)
        qn = q_ref[pl.ds(r0, W), :] * scale
        kband = kb_scr[pl.ds(start, 2 * W), :]
        vband = vb_scr[pl.ds(start, 2 * W), :]
        parts = []
        for p in range(n_pairs):
            qp = qn[:, p * LANES:(p + 1) * LANES]
            parts.append(jnp.where(low, qp, 0.0).astype(BF16))
            parts.append(jnp.where(low, 0.0, qp).astype(BF16))
        s_all = _dot_nt(jnp.concatenate(parts, axis=0), kband)
        bias = jnp.where(nb == 0, bias_first, bias_rest)
        probs, inv_l = [], []
        for hd in range(SWA_GROUP_HEADS):
            s = s_all[hd * W:(hd + 1) * W, :] + bias
            m = jnp.maximum(jnp.max(s, axis=-1, keepdims=True), sinks[hd])
            e = jnp.exp(s - m)
            inv_l.append(1.0 / (jnp.sum(e, axis=-1, keepdims=True) + jnp.exp(sinks[hd] - m)))
            probs.append(e.astype(BF16))
        o_all = _dot(jnp.concatenate(probs, axis=0), vband)
        outs = []
        for p in range(n_pairs):
            lo = o_all[(2 * p) * W:(2 * p + 1) * W, :] * inv_l[2 * p]
            hi = o_all[(2 * p + 1) * W:(2 * p + 2) * W, :] * inv_l[2 * p + 1]
            outs.append(jnp.where(low, lo, hi))
        o_ref[pl.ds(r0, W), :] = jnp.concatenate(outs, axis=1).astype(o_ref.dtype)
        return carry

    lax.fori_loop(0, seq // W, block, 0)


def _swa(proj, sinks, batch, seq):
    n = proj.shape[0]
    gw = SWA_GROUP_HEADS * HEAD_W
    gb = gw // LANES
    return pl.pallas_call(
        _swa_kernel,
        grid=(batch, SWA_GROUPS),
        in_specs=[
            pl.BlockSpec(memory_space=pltpu.SMEM),
            pl.BlockSpec((seq, gw), lambda b, g: (b, PB_SWA_Q // gb + g)),
            pl.BlockSpec((seq, LANES), lambda b, g: (b, PB_SWA_K + g)),
            pl.BlockSpec((seq, LANES), lambda b, g: (b, PB_SWA_V + g)),
        ],
        out_specs=pl.BlockSpec((seq, gw), lambda b, g: (b, g)),
        out_shape=jax.ShapeDtypeStruct((n, SWA_GROUPS * gw), BF16),
        scratch_shapes=[pltpu.VMEM((seq, LANES), BF16), pltpu.VMEM((seq, LANES), BF16)],
        compiler_params=_params(("arbitrary", "arbitrary")),
        name="swa",
    )(sinks, proj, proj, proj)


def _outproj_kernel(*refs, with_router, sub):
    if with_router:
        (ya_ref, yb_ref, yc_ref, w_ref, x_ref, gpost_ref, gt_ref, gpre_ref, sc_ref, sh_ref,
         wr_ref, xo_ref, h_ref, route_ref, count_ref, count_scr) = refs
    else:
        (ya_ref, yb_ref, yc_ref, w_ref, x_ref, gpost_ref, gt_ref, gpre_ref, sc_ref, sh_ref,
         xo_ref, h_ref) = refs
    wa = ya_ref.shape[1]
    wb = yb_ref.shape[1]
    if with_router:
        @pl.when(pl.program_id(0) == 0)
        def _():
            count_scr[...] = jnp.zeros_like(count_scr)

    for r0 in range(0, x_ref.shape[0], sub):
        rows = slice(r0, r0 + sub)
        y = _dot(ya_ref[rows, :], w_ref[0:wa, :])
        y = y + _dot(yb_ref[rows, :], w_ref[wa:wa + wb, :])
        y = y + _dot(yc_ref[rows, :], w_ref[wa + wb:, :])
        xn = x_ref[rows, :] + gt_ref[...] * _rms(y, gpost_ref[...])
        xo_ref[rows, :] = xn
        hb = (_rms(xn, gpre_ref[...]) * (1.0 + sc_ref[...]) + sh_ref[...]).astype(BF16)
        h_ref[rows, :] = hb.astype(h_ref.dtype)
        if not with_router:
            continue
        tm = sub
        lane = lax.broadcasted_iota(jnp.int32, (1, LANES), 1)
        logits = jnp.where(lane < N_EXPERTS, _dot(hb, wr_ref[...]), NEG_INF)
        m1 = jnp.max(logits, axis=-1, keepdims=True)
        i1 = jnp.min(jnp.where(logits == m1, lane, LANES), axis=-1, keepdims=True)
        rest = jnp.where(lane == i1, NEG_INF, logits)
        m2 = jnp.max(rest, axis=-1, keepdims=True)
        i2 = jnp.min(jnp.where(rest == m2, lane, LANES), axis=-1, keepdims=True)
        e2 = jnp.exp(m2 - m1)
        w1 = 1.0 / (1.0 + e2)
        picked = jnp.where((lane == i1) | (lane == i2), 1.0, 0.0)
        ri = lax.broadcasted_iota(jnp.int32, (tm, tm), 0)
        ci = lax.broadcasted_iota(jnp.int32, (tm, tm), 1)
        before = jnp.where(ri > ci, 1.0, 0.0).astype(BF16)
        rank = _dot(before, picked.astype(BF16)) + count_scr[...]
        r1 = jnp.sum(jnp.where(lane == i1, rank, 0.0), axis=-1, keepdims=True)
        r2 = jnp.sum(jnp.where(lane == i2, rank, 0.0), axis=-1, keepdims=True)
        count_scr[...] += jnp.sum(picked, axis=0, keepdims=True)
        rec = jnp.where(lane == RT_E1, i1.astype(F32), 0.0)
        for slot_lane, val in ((RT_E2, i2.astype(F32)), (RT_W1, w1), (RT_W2, e2 * w1),
                               (RT_R1, r1), (RT_R2, r2)):
            rec = jnp.where(lane == slot_lane, val, rec)
        route_ref[rows, :] = rec
    if with_router:
        count_ref[...] = jnp.broadcast_to(count_scr[...], count_ref.shape)


def _outproj(ya, yb, yc, w, x2, gpost, gpre, mod, seq, w_router=None):
    n, d = x2.shape
    tm, sub = 512, 256
    per_b = seq // tm
    with_router = w_router is not None

    def rows(width):
        return pl.BlockSpec((tm, width), lambda i: (i, 0))

    def vec():
        return pl.BlockSpec((1, d), lambda i: (0, 0))

    def modrow(k):
        return pl.BlockSpec((None, None, 1, d), lambda i: (i // per_b, k, 0, 0))

    in_specs = [rows(ya.shape[1]), rows(yb.shape[1]), rows(yc.shape[1]),
                pl.BlockSpec((d, d), lambda i: (0, 0), pipeline_mode=pl.Buffered(1)), rows(d),
                vec(), modrow(2), vec(), modrow(4), modrow(3)]
    args = [ya, yb, yc, w, x2, gpost, mod, gpre, mod, mod]
    out_specs = [rows(d), rows(d)]
    out_shape = [jax.ShapeDtypeStruct((n, d), F32),
                 jax.ShapeDtypeStruct((n, d), F32 if with_router else BF16)]
    scratch = []
    if with_router:
        in_specs.append(pl.BlockSpec((d, LANES), lambda i: (0, 0)))
        args.append(w_router)
        out_specs += [rows(LANES), pl.BlockSpec((8, LANES), lambda i: (0, 0))]
        out_shape += [jax.ShapeDtypeStruct((n, LANES), F32), jax.ShapeDtypeStruct((8, LANES), F32)]
        scratch.append(pltpu.VMEM((1, LANES), F32))
    return pl.pallas_call(
        functools.partial(_outproj_kernel, with_router=with_router, sub=sub),
        grid=(n // tm,),
        in_specs=in_specs,
        out_specs=out_specs,
        out_shape=out_shape,
        scratch_shapes=scratch,
        compiler_params=_params(("arbitrary",)),
        name="outproj",
    )(*args)


def _swiglu_step(h, wg_ref, wu_ref, wd_ref):
    gate = _dot(h, wg_ref[...])
    a = gate * jax.nn.sigmoid(gate) * _dot(h, wu_ref[...])
    return _dot(a.astype(BF16), wd_ref[...])


def _ffn_kernel(*refs, n_cast):
    h_ref, wg_ref, wu_ref, wd_ref, x_ref, gpost_ref, gt_ref = refs[:7]
    cast_in = refs[7:7 + n_cast]
    o_ref = refs[7 + n_cast]
    cast_out = refs[8 + n_cast:]
    f = pl.program_id(1)

    @pl.when(f == 0)
    def _():
        o_ref[...] = jnp.zeros_like(o_ref)

    o_ref[...] += _swiglu_step(h_ref[...], wg_ref, wu_ref, wd_ref)

    @pl.when(f == pl.num_programs(1) - 1)
    def _():
        o_ref[...] = x_ref[...] + gt_ref[...] * _rms(o_ref[...], gpost_ref[...])

    for src, dst in zip(cast_in, cast_out):
        dst[...] = src[...].astype(BF16)


def _cast_slices(shape, gi, gf):
    rows, cols = shape
    if rows % (16 * gi) == 0 and cols % (LANES * gf) == 0:
        return (rows // gi, cols // gf), lambda i, f: (i, f)
    steps = gi * gf
    assert rows % (16 * steps) == 0, shape
    return (rows // steps, cols), lambda i, f: (i * gf + f, 0)


def _ffn(h, wg, wu, wd, x2, gpost, mod, seq, to_cast=()):
    n, d = x2.shape
    tm, tf = 512, 512
    per_b = seq // tm
    dff = wg.shape[1]
    gi, gf = n // tm, dff // tf
    row = lambda i, f: (i, 0)
    cast_specs = [pl.BlockSpec(*_cast_slices(a.shape, gi, gf)) for a in to_cast]
    outs = pl.pallas_call(
        functools.partial(_ffn_kernel, n_cast=len(to_cast)),
        grid=(gi, gf),
        in_specs=[
            pl.BlockSpec((tm, d), row),
            pl.BlockSpec((d, tf), lambda i, f: (0, f)),
            pl.BlockSpec((d, tf), lambda i, f: (0, f)),
            pl.BlockSpec((tf, d), lambda i, f: (f, 0)),
            pl.BlockSpec((tm, d), row),
            pl.BlockSpec((1, d), lambda i, f: (0, 0)),
            pl.BlockSpec((None, None, 1, d), lambda i, f: (i // per_b, 5, 0, 0)),
        ] + cast_specs,
        out_specs=[pl.BlockSpec((tm, d), row)] + cast_specs,
        out_shape=[jax.ShapeDtypeStruct((n, d), F32)]
        + [jax.ShapeDtypeStruct(a.shape, BF16) for a in to_cast],
        compiler_params=_params(("arbitrary", "arbitrary")),
        name="ffn",
    )(h, wg, wu, wd, x2, gpost, mod, *to_cast)
    return outs[0], tuple(outs[1:])


def _dispatch_kernel(p1_ref, p2_ref, fill_ref, h_ref, xs_ref, zero_scr, sems, fill_sem, *, n_rows):
    tm = h_ref.shape[0]
    base = pl.program_id(0) * tm

    @pl.when(pl.program_id(0) == 0)
    def _():
        zero_scr[...] = jnp.zeros_like(zero_scr)
        tail = [pltpu.make_async_copy(zero_scr.at[pl.ds(0, MOE_TILE)],
                                      xs_ref.at[pl.ds(t0, MOE_TILE)], fill_sem)
                for t0 in range(n_rows, xs_ref.shape[0], MOE_TILE)]
        for cp in tail:
            cp.start()
        for cp in tail:
            cp.wait()
        for e in range(N_EXPERTS):
            cp = pltpu.make_async_copy(
                zero_scr, xs_ref.at[pl.ds(pl.multiple_of(fill_ref[e], 8), zero_scr.shape[0])],
                fill_sem)
            cp.start()
            cp.wait()

    def issue(r, carry):
        row = h_ref.at[pl.ds(r, 1)]
        pltpu.make_async_copy(row, xs_ref.at[pl.ds(p1_ref[base + r], 1)], sems.at[0]).start()
        pltpu.make_async_copy(row, xs_ref.at[pl.ds(p2_ref[base + r], 1)], sems.at[1]).start()
        return carry

    lax.fori_loop(0, tm, issue, 0, unroll=8)
    for k in range(TOP_K):
        pltpu.make_async_copy(h_ref, xs_ref.at[pl.ds(0, tm)], sems.at[k]).wait()


def _dispatch(h, p1, p2, fill_start, n_slots):
    n, d = h.shape
    tm = 512
    return pl.pallas_call(
        functools.partial(_dispatch_kernel, n_rows=TOP_K * n),
        grid_spec=pltpu.PrefetchScalarGridSpec(
            num_scalar_prefetch=3,
            grid=(n // tm,),
            in_specs=[pl.BlockSpec((tm, d), lambda i, p1, p2, fs: (i, 0))],
            out_specs=pl.BlockSpec(memory_space=pl.ANY),
            scratch_shapes=[pltpu.VMEM((MOE_TILE + 8, d), F32),
                            pltpu.SemaphoreType.DMA((TOP_K,)),
                            pltpu.SemaphoreType.DMA(())],
        ),
        out_shape=jax.ShapeDtypeStruct((n_slots, d), F32),
        compiler_params=_params(("arbitrary",)),
        name="moe_dispatch",
    )(p1, p2, fill_start, h)


def _moe_ffn_kernel(te_ref, nv_ref, x_ref, wg_hbm, wu_hbm, wd_hbm, y_ref, h_scr,
                    wg_buf, wu_buf, wd_buf, sems, *, tf):
    j = pl.program_id(0)
    n_used = nv_ref[0]
    nf = wg_hbm.shape[2] // tf

    def slices(tile, f, slot):
        e = te_ref[tile]
        c0 = pl.multiple_of(f * tf, tf)
        return (
            pltpu.make_async_copy(wg_hbm.at[e, :, pl.ds(c0, tf)], wg_buf.at[slot], sems.at[0, slot]),
            pltpu.make_async_copy(wu_hbm.at[e, :, pl.ds(c0, tf)], wu_buf.at[slot], sems.at[1, slot]),
            pltpu.make_async_copy(wd_hbm.at[e, pl.ds(c0, tf), :], wd_buf.at[slot], sems.at[2, slot]),
        )

    y_ref[...] = jnp.zeros_like(y_ref)

    @pl.when(j < n_used)
    def _():
        @pl.when(j == 0)
        def _():
            for cp in slices(0, 0, 0):
                cp.start()

        h_scr[...] = x_ref[...].astype(BF16)

        def step(f, carry):
            slot = lax.rem(j * nf + f, 2)
            for cp in slices(j, f, slot):
                cp.wait()

            @pl.when(f + 1 < nf)
            def _():
                for cp in slices(j, f + 1, 1 - slot):
                    cp.start()

            @pl.when((f + 1 == nf) & (j + 1 < n_used))
            def _():
                for cp in slices(j + 1, 0, 1 - slot):
                    cp.start()

            y_ref[...] += _swiglu_step(h_scr[...], wg_buf.at[slot], wu_buf.at[slot],
                                       wd_buf.at[slot])
            return carry

        lax.fori_loop(0, nf, step, 0)


def _moe_ffn(xs, tile_expert, n_valid, wg, wu, wd):
    n_slots, d = xs.shape
    tm, tf = MOE_TILE, 256

    return pl.pallas_call(
        functools.partial(_moe_ffn_kernel, tf=tf),
        grid_spec=pltpu.PrefetchScalarGridSpec(
            num_scalar_prefetch=2,
            grid=(n_slots // tm,),
            in_specs=[
                pl.BlockSpec((tm, d), lambda j, te, nv: (jnp.minimum(j, nv[0] - 1), 0)),
                pl.BlockSpec(memory_space=pl.ANY),
                pl.BlockSpec(memory_space=pl.ANY),
                pl.BlockSpec(memory_space=pl.ANY),
            ],
            out_specs=pl.BlockSpec((tm, d), lambda j, te, nv: (j, 0)),
            scratch_shapes=[
                pltpu.VMEM((tm, d), BF16),
                pltpu.VMEM((2, d, tf), BF16),
                pltpu.VMEM((2, d, tf), BF16),
                pltpu.VMEM((2, tf, d), BF16),
                pltpu.SemaphoreType.DMA((3, 2)),
            ],
        ),
        out_shape=jax.ShapeDtypeStruct((n_slots, d), F32),
        compiler_params=_params(("arbitrary",)),
        name="moe_ffn",
    )(tile_expert, n_valid, xs, wg, wu, wd)


def _combine_kernel(p1_ref, p2_ref, ys_ref, route_ref, x_ref, gpost_ref, gt_ref, o_ref,
                    y1_scr, y2_scr, sems):
    tm = x_ref.shape[0]
    base = pl.program_id(0) * tm

    def issue(r, carry):
        pltpu.make_async_copy(ys_ref.at[pl.ds(p1_ref[base + r], 1)], y1_scr.at[pl.ds(r, 1)],
                              sems.at[0]).start()
        pltpu.make_async_copy(ys_ref.at[pl.ds(p2_ref[base + r], 1)], y2_scr.at[pl.ds(r, 1)],
                              sems.at[1]).start()
        return carry


    lax.fori_loop(0, tm, issue, 0, unroll=8)
    pltpu.make_async_copy(ys_ref.at[pl.ds(0, tm)], y1_scr, sems.at[0]).wait()
    pltpu.make_async_copy(ys_ref.at[pl.ds(0, tm)], y2_scr, sems.at[1]).wait()
    lane = lax.broadcasted_iota(jnp.int32, (1, LANES), 1)
    route = route_ref[...]
    w1 = jnp.sum(jnp.where(lane == RT_W1, route, 0.0), axis=-1, keepdims=True)
    w2 = jnp.sum(jnp.where(lane == RT_W2, route, 0.0), axis=-1, keepdims=True)
    y = w1 * y1_scr[...] + w2 * y2_scr[...]
    o_ref[...] = x_ref[...] + gt_ref[...] * _rms(y, gpost_ref[...])


def _combine(ys, p1, p2, route, x2, gpost, mod, seq):
    n, d = x2.shape
    tm = 512
    per_b = seq // tm
    row = lambda i, p1, p2: (i, 0)
    return pl.pallas_call(
        _combine_kernel,
        grid_spec=pltpu.PrefetchScalarGridSpec(
            num_scalar_prefetch=2,
            grid=(n // tm,),
            in_specs=[
                pl.BlockSpec(memory_space=pl.ANY),
                pl.BlockSpec((tm, LANES), row),
                pl.BlockSpec((tm, d), row),
                pl.BlockSpec((1, d), lambda i, p1, p2: (0, 0)),
                pl.BlockSpec((None, None, 1, d), lambda i, p1, p2: (i // per_b, 5, 0, 0)),
            ],
            out_specs=pl.BlockSpec((tm, d), row),
            scratch_shapes=[pltpu.VMEM((tm, d), F32), pltpu.VMEM((tm, d), F32),
                            pltpu.SemaphoreType.DMA((TOP_K,))],
        ),
        out_shape=jax.ShapeDtypeStruct((n, d), F32),
        compiler_params=_params(("arbitrary",)),
        name="moe_combine",
    )(p1, p2, ys, route, x2, gpost, mod)


def _route_plan(route, counts, n_tiles):
    e1 = route[:, RT_E1].astype(jnp.int32)
    e2 = route[:, RT_E2].astype(jnp.int32)
    cnt = counts[0, :N_EXPERTS].astype(jnp.int32)
    size = (cnt + MOE_TILE - 1) // MOE_TILE * MOE_TILE
    end = jnp.cumsum(size)
    start = end - size
    p1 = start[e1] + route[:, RT_R1].astype(jnp.int32)
    p2 = start[e2] + route[:, RT_R2].astype(jnp.int32)
    n_valid = end[-1] // MOE_TILE
    tile_start = jnp.minimum(jnp.arange(n_tiles), n_valid - 1) * MOE_TILE
    tile_expert = jnp.sum(tile_start[:, None] >= end[None, :], axis=1).astype(jnp.int32)
    fill_start = (start + cnt) // 8 * 8
    return p1, p2, fill_start, tile_expert, n_valid.reshape(1).astype(jnp.int32)


def _pack_w_in(w_in):
    d = w_in.shape[0]
    gw = 4 * LANES
    offs = [0]
    for width in (gw, gw, gw, gw, gw, gw, gw, MLSTM_HEADS, MLSTM_HEADS,
                  2 * SWA_GROUP_HEADS * HEAD_W, SWA_GROUPS * HEAD_W, SWA_GROUPS * HEAD_W):
        offs.append(offs[-1] + width)
    w_in = w_in.astype(BF16)
    (mb_q, mb_k, mb_v, ml_q, ml_k, ml_v, ml_o, ml_i, ml_f, sw_q, sw_k, sw_v) = [
        w_in[:, offs[j]:offs[j + 1]] for j in range(12)]
    gates = jnp.concatenate(
        [ml_i, ml_f, jnp.zeros((d, 2 * LANES - 2 * MLSTM_HEADS), w_in.dtype)], axis=1)

    def dup(w):
        heads = [w[:, g * HEAD_W:(g + 1) * HEAD_W] for g in range(SWA_GROUPS)]
        return jnp.concatenate([t for hd in heads for t in (hd, hd)], axis=1)

    packed = jnp.concatenate(
        [sw_q, mb_q, mb_k, mb_v, ml_q, ml_k, ml_v, ml_o, gates, dup(sw_k), dup(sw_v)], axis=1)
    assert packed.shape[1] == PROJ_BLOCKS * LANES
    return packed.astype(BF16)


def kernel(x, c, ada_w, ada_b, g_pre_mix, g_post_mix, g_pre_ffn, g_post_ffn, w_in, w_out, conv_w,
           conv_b, igate_b, fgate_b, mlstm_norm_w, swa_sinks, ffn_w_gate, ffn_w_up, ffn_w_down,
           moe_router, moe_w_gate, moe_w_up, moe_w_down):
    batch, seq, d = x.shape
    depth = ada_w.shape[0]
    n = batch * seq
    x2 = x.reshape(n, d)
    mod_all = _adaln(c, ada_w, ada_b).reshape(depth, batch, 6, 1, d)
    for l in range(depth):
        mod = mod_all[l]
        proj = _inproj(x2, g_pre_mix[l].reshape(1, d), mod, _pack_w_in(w_in[l]), seq)
        y_moba = _moba(proj, batch, seq)
        gate_bias = jnp.concatenate(
            [igate_b[l], fgate_b[l], jnp.zeros((LANES - 2 * MLSTM_HEADS,), F32)]).reshape(1, LANES)
        y_mlstm = _mlstm(proj, conv_w[l], conv_b[l].reshape(1, -1), gate_bias,
                         mlstm_norm_w[l].reshape(1, -1), batch, seq)
        y_swa = _swa(proj, swa_sinks[l], batch, seq)
        j = l // 2
        w_router = None
        if l % 2 == 1:
            w_router = jnp.pad(moe_router[j], ((0, 0), (0, LANES - N_EXPERTS))).astype(BF16)
        outs = _outproj(y_moba, y_mlstm, y_swa, w_out[l].astype(BF16), x2,
                        g_post_mix[l].reshape(1, d), g_pre_ffn[l].reshape(1, d), mod, seq, w_router)
        gpost = g_post_ffn[l].reshape(1, d)
        if l % 2 == 0:
            x2, h = outs
            to_cast = ()
            if l + 1 < depth:
                to_cast = tuple(w[(l + 1) // 2].reshape(-1, w.shape[-1])
                                for w in (moe_w_gate, moe_w_up, moe_w_down))
            x2, moe_bf16 = _ffn(h, ffn_w_gate[j].astype(BF16), ffn_w_up[j].astype(BF16),
                                ffn_w_down[j].astype(BF16), x2, gpost, mod, seq, to_cast)
        else:
            x2, h, route, counts = outs
            n_tiles = (TOP_K * n) // MOE_TILE + N_EXPERTS + 2
            p1, p2, fill_start, tile_expert, n_valid = _route_plan(route, counts, n_tiles)
            xs = _dispatch(h, p1, p2, fill_start, n_tiles * MOE_TILE)
            wg, wu, wd = (w.reshape(N_EXPERTS, -1, w.shape[-1]) for w in moe_bf16)
            ys = _moe_ffn(xs, tile_expert, n_valid, wg, wu, wd)
            x2 = _combine(ys, p1, p2, route, x2, gpost, mod, seq)
    return x2.reshape(batch, seq, d)
```

```python
import functools

import jax
import jax.numpy as jnp
from jax import lax
from jax.experimental import pallas as pl
from jax.experimental.pallas import tpu as pltpu

F32 = jnp.float32
BF16 = jnp.bfloat16

LANES = 128
HEAD_W = 64
MOBA_BLOCK = 256
MOBA_TOPK = 3
MOBA_PAIRS = 4
MLSTM_HEADS = 4
MLSTM_CHUNK = 256
MLSTM_CONV = 4
MLSTM_GATE_CAP = 15.0
SWA_WINDOW = 128
SWA_GROUPS = 2
SWA_GROUP_HEADS = 8
N_EXPERTS = 8
RMS_EPS = 1e-6
NEG_INF = float("-inf")
MASK_NEG = -1e30
MOE_TILE = 512
MOE_TF = 256
FFN_TM, FFN_TF = 512, 512
TOP_K = 2
RT_E1, RT_E2, RT_W1, RT_W2, RT_R1, RT_R2 = range(6)
VMEM_LIMIT = 56 * 1024 * 1024

PB_SWA_Q = 0
PB_MOBA_Q = 8
PB_MOBA_K = 12
PB_MOBA_V = 16
PB_ML_Q = 20
PB_ML_K = 24
PB_ML_V = 28
PB_ML_O = 32
PB_ML_G = 36
PB_SWA_K = 38
PB_SWA_V = 40
PROJ_BLOCKS = 42


def _params(sem):
    return pltpu.CompilerParams(dimension_semantics=sem, vmem_limit_bytes=VMEM_LIMIT)


def _rms(x, g):
    return x * lax.rsqrt(jnp.mean(x * x, axis=-1, keepdims=True) + RMS_EPS) * g


def _dot(a, b):
    return jnp.dot(a, b, preferred_element_type=F32)


def _dot_nt(a, b):
    return lax.dot_general(a, b, (((1,), (1,)), ((), ())), preferred_element_type=F32)


def _adaln_kernel(c_ref, w_ref, b_ref, o_ref):
    c = c_ref[...]
    cond = (c * jax.nn.sigmoid(c)).astype(BF16)
    o_ref[...] = _dot(cond, w_ref[...].astype(BF16)) + b_ref[...]


def _adaln(c, ada_w, ada_b):
    depth, d, n6 = ada_w.shape
    b = c.shape[0]
    tn = 1024
    return pl.pallas_call(
        _adaln_kernel,
        grid=(depth, n6 // tn),
        in_specs=[
            pl.BlockSpec((b, d), lambda l, j: (0, 0)),
            pl.BlockSpec((None, d, tn), lambda l, j: (l, 0, j)),
            pl.BlockSpec((None, 1, tn), lambda l, j: (l, 0, j)),
        ],
        out_specs=pl.BlockSpec((None, b, tn), lambda l, j: (l, 0, j)),
        out_shape=jax.ShapeDtypeStruct((depth, b, n6), F32),
        compiler_params=_params(("arbitrary", "arbitrary")),
        name="adaln",
    )(c, ada_w, ada_b.reshape(depth, 1, n6))


def _inproj_kernel(x_ref, g_ref, sc_ref, sh_ref, w_ref, o_ref, *, tn):
    h = (_rms(x_ref[...], g_ref[...]) * (1.0 + sc_ref[...]) + sh_ref[...]).astype(BF16)
    for c in range(w_ref.shape[1] // tn):
        o_ref[:, c * tn:(c + 1) * tn] = _dot(h, w_ref[:, c * tn:(c + 1) * tn])


def _inproj(x2, g, mod, w, seq):
    n, d = x2.shape
    nc = w.shape[1]
    tm, tn = 256, 768
    per_b = seq // tm
    return pl.pallas_call(
        functools.partial(_inproj_kernel, tn=tn),
        grid=(n // tm,),
        in_specs=[
            pl.BlockSpec((tm, d), lambda i: (i, 0)),
            pl.BlockSpec((1, d), lambda i: (0, 0)),
            pl.BlockSpec((None, None, 1, d), lambda i: (i // per_b, 1, 0, 0)),
            pl.BlockSpec((None, None, 1, d), lambda i: (i // per_b, 0, 0, 0)),
            pl.BlockSpec((d, nc), lambda i: (0, 0), pipeline_mode=pl.Buffered(1)),
        ],
        out_specs=pl.BlockSpec((tm, nc), lambda i: (i, 0)),
        out_shape=jax.ShapeDtypeStruct((n, nc), F32),
        compiler_params=_params(("arbitrary",)),
        name="inproj",
    )(x2, g, mod, mod, w)


def _moba_kernel(q_ref, k_ref, v_ref, o_ref, ka_scr, vb_scr):
    seq = k_ref.shape[0]
    blk = MOBA_BLOCK
    n_blk = seq // blk
    lane = lax.broadcasted_iota(jnp.int32, (1, LANES), 1)
    k = k_ref[...]
    vb_scr[...] = v_ref[...].astype(BF16)
    kmean = jnp.mean(k.reshape(n_blk, blk, LANES), axis=1)
    kblk = lax.shift_right_logical(lax.broadcasted_iota(jnp.int32, (seq, 1), 0), 8)
    ri = lax.broadcasted_iota(jnp.int32, (blk, blk), 0)
    ci = lax.broadcasted_iota(jnp.int32, (blk, blk), 1)
    causal_bias = jnp.where(ri >= ci, 0.0, MASK_NEG)
    scale = HEAD_W ** -0.5
    in_head, slot, kmp = [], [], []
    for hh in range(2):
        base = HEAD_W * (1 - hh)
        in_head.append((lane >= HEAD_W * hh) & (lane < HEAD_W * (hh + 1)))
        slot.append(lane - base)
        ka_scr[hh] = jnp.where(in_head[hh], k, jnp.where(slot[hh] == kblk, 1.0, 0.0)).astype(BF16)
        rows = [jnp.where(in_head[hh], kmean, 0.0), jnp.zeros((LANES - base - n_blk, LANES), F32)]
        if base:
            rows.insert(0, jnp.zeros((base, LANES), F32))
        kmp.append(jnp.concatenate(rows, axis=0).astype(BF16))

    for i in range(n_blk):
        q = q_ref[i * blk:(i + 1) * blk, :]
        outs = []
        for hh in range(2):
            base = HEAD_W * (1 - hh)
            qa = jnp.where(in_head[hh], q * scale, 0.0)
            if i > MOBA_TOPK:
                valid = (slot[hh] >= 0) & (slot[hh] < i)
                qm = jnp.where(in_head[hh], q, 0.0).astype(BF16)
                gate = jnp.where(valid, _dot_nt(qm, kmp[hh]), NEG_INF)
                beaten_by = jnp.zeros((blk, LANES), jnp.int32)
                for j in range(i):
                    gj = gate[:, base + j:base + j + 1]
                    beats = (gj > gate) | ((gj == gate) & (slot[hh] > j))
                    beaten_by = beaten_by + beats.astype(jnp.int32)
                qa = jnp.where(valid & (beaten_by >= MOBA_TOPK), MASK_NEG, qa)
            s = _dot_nt(qa.astype(BF16), ka_scr[hh, 0:(i + 1) * blk, :])
            s_own = s[:, i * blk:] + causal_bias
            m = jnp.max(s_own, axis=-1, keepdims=True)
            if i:
                s_past = s[:, :i * blk]
                m = jnp.maximum(m, jnp.max(s_past, axis=-1, keepdims=True))
            p_own = jnp.exp(s_own - m)
            l = jnp.sum(p_own, axis=-1, keepdims=True)
            acc = _dot(p_own.astype(BF16), vb_scr[i * blk:(i + 1) * blk, :])
            if i:
                p_past = jnp.exp(s_past - m)
                l = l + jnp.sum(p_past, axis=-1, keepdims=True)
                acc = acc + _dot(p_past.astype(BF16), vb_scr[0:i * blk, :])
            outs.append(acc / l)
        o_ref[i * blk:(i + 1) * blk, :] = jnp.where(lane < HEAD_W, outs[0], outs[1]).astype(o_ref.dtype)


def _moba(proj, batch, seq):
    n = proj.shape[0]
    return pl.pallas_call(
        _moba_kernel,
        grid=(batch, MOBA_PAIRS),
        in_specs=[
            pl.BlockSpec((seq, LANES), lambda b, p: (b, PB_MOBA_Q + p)),
            pl.BlockSpec((seq, LANES), lambda b, p: (b, PB_MOBA_K + p)),
            pl.BlockSpec((seq, LANES), lambda b, p: (b, PB_MOBA_V + p)),
        ],
        out_specs=pl.BlockSpec((seq, LANES), lambda b, p: (b, p)),
        out_shape=jax.ShapeDtypeStruct((n, MOBA_PAIRS * LANES), BF16),
        scratch_shapes=[
            pltpu.VMEM((2, seq, LANES), BF16),
            pltpu.VMEM((seq, LANES), BF16),
        ],
        compiler_params=_params(("arbitrary", "arbitrary")),
        name="moba",
    )(proj, proj, proj)


def _causal_conv_silu(x, tail, w, b):
    row = lax.broadcasted_iota(jnp.int32, (8, 1), 0)
    y = b + w[MLSTM_CONV - 1:MLSTM_CONV, :] * x
    for shift in range(1, MLSTM_CONV):
        xr = pltpu.roll(x, shift, axis=0)
        head = jnp.where(row < shift, pltpu.roll(tail, shift, axis=0), xr[0:8, :])
        xs = jnp.concatenate([head, xr[8:, :]], axis=0)
        y = y + w[MLSTM_CONV - 1 - shift:MLSTM_CONV - shift, :] * xs
    return y * jax.nn.sigmoid(y)


def _mlstm_kernel(q_ref, k_ref, v_ref, og_ref, gate_ref, cwq_ref, cwk_ref, cbq_ref, cbk_ref,
                  gb_ref, nw_ref, o_ref, qtail_scr, ktail_scr, c_scr, n_scr, m_scr):
    ts = L = q_ref.shape[0]
    lane = lax.broadcasted_iota(jnp.int32, (1, LANES), 1)
    ri = lax.broadcasted_iota(jnp.int32, (L, L), 0)
    ci = lax.broadcasted_iota(jnp.int32, (L, L), 1)
    causal = ri >= ci
    lower = causal.astype(F32)
    upper = (ri <= ci).astype(F32)

    @pl.when(pl.program_id(1) == 0)
    def _():
        qtail_scr[...] = jnp.zeros_like(qtail_scr)
        ktail_scr[...] = jnp.zeros_like(ktail_scr)
        c_scr[...] = jnp.zeros_like(c_scr)
        n_scr[...] = jnp.zeros_like(n_scr)
        m_scr[...] = jnp.zeros_like(m_scr)

    t = MLSTM_GATE_CAP * jnp.tanh((gate_ref[...] + gb_ref[...]) / MLSTM_GATE_CAP)
    a_col = jnp.where(lane < MLSTM_HEADS, t, jax.nn.log_sigmoid(t))
    a_row = a_col.T[0:8, :]

    xq, xk = q_ref[...], k_ref[...]
    qc = _causal_conv_silu(xq, qtail_scr[...], cwq_ref[...], cbq_ref[...]).astype(BF16)
    kc = (_causal_conv_silu(xk, ktail_scr[...], cwk_ref[...], cbk_ref[...])
          * (LANES ** -0.5)).astype(BF16)
    qtail_scr[...] = xq[ts - 8:, :]
    ktail_scr[...] = xk[ts - 8:, :]

    b_c = jnp.dot(lower, a_col, precision=lax.Precision.HIGHEST, preferred_element_type=F32)
    b_r = jnp.dot(a_row, upper, precision=lax.Precision.HIGHEST, preferred_element_type=F32)

    local = []
    for hd in range(MLSTM_HEADS):
        cols = slice(hd * LANES, (hd + 1) * LANES)
        q, k = qc[:, cols], kc[:, cols]
        v = v_ref[:, cols].astype(BF16)
        fl = hd + MLSTM_HEADS
        b_col, li_col = b_c[:, fl:fl + 1], a_col[:, hd:hd + 1]
        b_row, li_row = b_r[fl:fl + 1, :], a_row[hd:hd + 1, :]
        b_last = b_row[:, L - 1:L]
        d_log = jnp.where(causal, b_col - b_row + li_row, NEG_INF)
        d_max = jnp.max(d_log, axis=1, keepdims=True)
        qk = _dot_nt(q, k) * jnp.exp(d_log - d_max)
        pv = _dot(qk.astype(BF16), v)
        qk_sum = jnp.sum(qk, axis=1, keepdims=True)
        a_max = jnp.max(b_last - b_row + li_row, axis=1, keepdims=True)
        kw = k.astype(F32) * jnp.exp(b_last - b_col + li_col - a_max)
        c_in = _dot(kw.T.astype(BF16), v)
        n_in = jnp.sum(kw, axis=0, keepdims=True)
        local.append((q, b_col, b_last, d_max, pv, qk_sum, a_max, c_in, n_in))

    for hd in range(MLSTM_HEADS):
        cols = slice(hd * LANES, (hd + 1) * LANES)
        q, b_col, b_last, d_max, pv, qk_sum, a_max, c_in, n_in = local[hd]
        c_st, n_st, m_st = c_scr[hd], n_scr[hd], m_scr[hd][:, 0:1]
        inter_log = b_col + m_st
        m_out = jnp.maximum(inter_log, d_max)
        w_inter = jnp.exp(inter_log - m_out)
        w_local = jnp.exp(d_max - m_out)
        num = w_local * pv + w_inter * _dot(q, c_st.astype(BF16))
        den = w_local * qk_sum + w_inter * jnp.sum(q.astype(F32) * n_st, axis=1, keepdims=True)
        hv = num / jnp.maximum(jnp.abs(den), jnp.exp(-m_out))
        hn = _rms(hv, nw_ref[:, cols])
        o_ref[:, cols] = (jax.nn.sigmoid(og_ref[:, cols]) * hn).astype(o_ref.dtype)
        m_new = jnp.maximum(b_last + m_st, a_max)
        decay = jnp.exp(b_last + m_st - m_new)
        inject = jnp.exp(a_max - m_new)
        c_scr[hd] = decay * c_st + inject * c_in
        n_scr[hd] = decay * n_st + inject * n_in
        m_scr[hd] = jnp.broadcast_to(m_new, (1, LANES))


def _mlstm(proj, conv_w, conv_b, gate_bias, norm_w, batch, seq):
    n = proj.shape[0]
    hw = MLSTM_HEADS
    gw = hw * LANES
    ts = MLSTM_CHUNK
    tiles = seq // ts

    def col(base):
        return pl.BlockSpec((ts, gw), lambda b, t: (b * tiles + t, base // hw))

    def vec(rows, blk):
        return pl.BlockSpec((rows, gw), lambda b, t: (0, blk))

    return pl.pallas_call(
        _mlstm_kernel,
        grid=(batch, tiles),
        in_specs=[
            col(PB_ML_Q), col(PB_ML_K), col(PB_ML_V), col(PB_ML_O),
            pl.BlockSpec((ts, LANES), lambda b, t: (b * tiles + t, PB_ML_G)),
            vec(MLSTM_CONV, 0), vec(MLSTM_CONV, 1), vec(1, 0), vec(1, 1),
            pl.BlockSpec((1, LANES), lambda b, t: (0, 0)),
            vec(1, 0),
        ],
        out_specs=pl.BlockSpec((ts, gw), lambda b, t: (b * tiles + t, 0)),
        out_shape=jax.ShapeDtypeStruct((n, gw), BF16),
        scratch_shapes=[
            pltpu.VMEM((8, gw), F32),
            pltpu.VMEM((8, gw), F32),
            pltpu.VMEM((hw, LANES, LANES), F32),
            pltpu.VMEM((hw, 1, LANES), F32),
            pltpu.VMEM((hw, 1, LANES), F32),
        ],
        compiler_params=_params(("arbitrary", "arbitrary")),
        name="mlstm",
    )(proj, proj, proj, proj, proj, conv_w, conv_w, conv_b, conv_b, gate_bias, norm_w)


def _swa_kernel(sink_ref, q_ref, k_ref, v_ref, o_ref, kb_scr, vb_scr):
    g = pl.program_id(1)
    seq = q_ref.shape[0]
    W = SWA_WINDOW
    n_pairs = SWA_GROUP_HEADS // 2
    kb_scr[...] = k_ref[...].astype(BF16)
    vb_scr[...] = v_ref[...].astype(BF16)
    lane = lax.broadcasted_iota(jnp.int32, (1, LANES), 1)
    low = lane < HEAD_W
    scale = HEAD_W ** -0.5
    sinks = [sink_ref[g * SWA_GROUP_HEADS + hd] for hd in range(SWA_GROUP_HEADS)]
    r = lax.broadcasted_iota(jnp.int32, (W, 2 * W), 0)
    c = lax.broadcasted_iota(jnp.int32, (W, 2 * W), 1)
    bias_first = jnp.where(c <= r, 0.0, MASK_NEG)
    bias_rest = jnp.where((c > r) & (c <= r + W), 0.0, MASK_NEG)

    def block(nb, carry):
        start = pl.multiple_of(jnp.maximum(nb - 1, 0) * W, W)
        r0 = pl.multiple_of(nb * W, W)
        qn = q_ref[pl.ds(r0, W), :] * scale
        kband = kb_scr[pl.ds(start, 2 * W), :]
        vband = vb_scr[pl.ds(start, 2 * W), :]
        parts = []
        for p in range(n_pairs):
            qp = qn[:, p * LANES:(p + 1) * LANES]
            parts.append(jnp.where(low, qp, 0.0).astype(BF16))
            parts.append(jnp.where(low, 0.0, qp).astype(BF16))
        s_all = _dot_nt(jnp.concatenate(parts, axis=0), kband)
        bias = jnp.where(nb == 0, bias_first, bias_rest)
        probs, inv_l = [], []
        for hd in range(SWA_GROUP_HEADS):
            s = s_all[hd * W:(hd + 1) * W, :] + bias
            m = jnp.maximum(jnp.max(s, axis=-1, keepdims=True), sinks[hd])
            e = jnp.exp(s - m)
            inv_l.append(1.0 / (jnp.sum(e, axis=-1, keepdims=True) + jnp.exp(sinks[hd] - m)))
            probs.append(e.astype(BF16))
        o_all = _dot(jnp.concatenate(probs, axis=0), vband)
        outs = []
        for p in range(n_pairs):
            lo = o_all[(2 * p) * W:(2 * p + 1) * W, :] * inv_l[2 * p]
            hi = o_all[(2 * p + 1) * W:(2 * p + 2) * W, :] * inv_l[2 * p + 1]
            outs.append(jnp.where(low, lo, hi))
        o_ref[pl.ds(r0, W), :] = jnp.concatenate(outs, axis=1).astype(o_ref.dtype)
        return carry

    lax.fori_loop(0, seq // W, block, 0)


def _swa(proj, sinks, batch, seq):
    n = proj.shape[0]
    gw = SWA_GROUP_HEADS * HEAD_W
    gb = gw // LANES
    return pl.pallas_call(
        _swa_kernel,
        grid=(batch, SWA_GROUPS),
        in_specs=[
            pl.BlockSpec(memory_space=pltpu.SMEM),
            pl.BlockSpec((seq, gw), lambda b, g: (b, PB_SWA_Q // gb + g)),
            pl.BlockSpec((seq, LANES), lambda b, g: (b, PB_SWA_K + g)),
            pl.BlockSpec((seq, LANES), lambda b, g: (b, PB_SWA_V + g)),
        ],
        out_specs=pl.BlockSpec((seq, gw), lambda b, g: (b, g)),
        out_shape=jax.ShapeDtypeStruct((n, SWA_GROUPS * gw), BF16),
        scratch_shapes=[pltpu.VMEM((seq, LANES), BF16), pltpu.VMEM((seq, LANES), BF16)],
        compiler_params=_params(("arbitrary", "arbitrary")),
        name="swa",
    )(sinks, proj, proj, proj)


def _outproj_kernel(*refs, with_router, sub):
    if with_router:
        (ya_ref, yb_ref, yc_ref, w_ref, x_ref, gpost_ref, gt_ref, gpre_ref, sc_ref, sh_ref,
         wr_ref, xo_ref, h_ref, route_ref, count_ref, count_scr) = refs
    else:
        (ya_ref, yb_ref, yc_ref, w_ref, x_ref, gpost_ref, gt_ref, gpre_ref, sc_ref, sh_ref,
         xo_ref, h_ref) = refs
    wa = ya_ref.shape[1]
    wb = yb_ref.shape[1]
    if with_router:
        @pl.when(pl.program_id(0) == 0)
        def _():
            count_scr[...] = jnp.zeros_like(count_scr)

    for r0 in range(0, x_ref.shape[0], sub):
        rows = slice(r0, r0 + sub)
        y = _dot(ya_ref[rows, :], w_ref[0:wa, :])
        y = y + _dot(yb_ref[rows, :], w_ref[wa:wa + wb, :])
        y = y + _dot(yc_ref[rows, :], w_ref[wa + wb:, :])
        xn = x_ref[rows, :] + gt_ref[...] * _rms(y, gpost_ref[...])
        xo_ref[rows, :] = xn
        hb = (_rms(xn, gpre_ref[...]) * (1.0 + sc_ref[...]) + sh_ref[...]).astype(BF16)
        h_ref[rows, :] = hb.astype(h_ref.dtype)
        if not with_router:
            continue
        tm = sub
        lane = lax.broadcasted_iota(jnp.int32, (1, LANES), 1)
        logits = jnp.where(lane < N_EXPERTS, _dot(hb, wr_ref[...]), NEG_INF)
        m1 = jnp.max(logits, axis=-1, keepdims=True)
        i1 = jnp.min(jnp.where(logits == m1, lane, LANES), axis=-1, keepdims=True)
        rest = jnp.where(lane == i1, NEG_INF, logits)
        m2 = jnp.max(rest, axis=-1, keepdims=True)
        i2 = jnp.min(jnp.where(rest == m2, lane, LANES), axis=-1, keepdims=True)
        e2 = jnp.exp(m2 - m1)
        w1 = 1.0 / (1.0 + e2)
        picked = jnp.where((lane == i1) | (lane == i2), 1.0, 0.0)
        ri = lax.broadcasted_iota(jnp.int32, (tm, tm), 0)
        ci = lax.broadcasted_iota(jnp.int32, (tm, tm), 1)
        before = jnp.where(ri > ci, 1.0, 0.0).astype(BF16)
        rank = _dot(before, picked.astype(BF16)) + count_scr[...]
        r1 = jnp.sum(jnp.where(lane == i1, rank, 0.0), axis=-1, keepdims=True)
        r2 = jnp.sum(jnp.where(lane == i2, rank, 0.0), axis=-1, keepdims=True)
        count_scr[...] += jnp.sum(picked, axis=0, keepdims=True)
        rec = jnp.where(lane == RT_E1, i1.astype(F32), 0.0)
        for slot_lane, val in ((RT_E2, i2.astype(F32)), (RT_W1, w1), (RT_W2, e2 * w1),
                               (RT_R1, r1), (RT_R2, r2)):
            rec = jnp.where(lane == slot_lane, val, rec)
        route_ref[rows, :] = rec
    if with_router:
        count_ref[...] = jnp.broadcast_to(count_scr[...], count_ref.shape)


def _outproj(ya, yb, yc, w, x2, gpost, gpre, mod, seq, w_router=None):
    n, d = x2.shape
    tm, sub = 512, 256
    per_b = seq // tm
    with_router = w_router is not None

    def rows(width):
        return pl.BlockSpec((tm, width), lambda i: (i, 0))

    def vec():
        return pl.BlockSpec((1, d), lambda i: (0, 0))

    def modrow(k):
        return pl.BlockSpec((None, None, 1, d), lambda i: (i // per_b, k, 0, 0))

    in_specs = [rows(ya.shape[1]), rows(yb.shape[1]), rows(yc.shape[1]),
                pl.BlockSpec((d, d), lambda i: (0, 0), pipeline_mode=pl.Buffered(1)), rows(d),
                vec(), modrow(2), vec(), modrow(4), modrow(3)]
    args = [ya, yb, yc, w, x2, gpost, mod, gpre, mod, mod]
    out_specs = [rows(d), rows(d)]
    out_shape = [jax.ShapeDtypeStruct((n, d), F32),
                 jax.ShapeDtypeStruct((n, d), F32 if with_router else BF16)]
    scratch = []
    if with_router:
        in_specs.append(pl.BlockSpec((d, LANES), lambda i: (0, 0)))
        args.append(w_router)
        out_specs += [rows(LANES), pl.BlockSpec((8, LANES), lambda i: (0, 0))]
        out_shape += [jax.ShapeDtypeStruct((n, LANES), F32), jax.ShapeDtypeStruct((8, LANES), F32)]
        scratch.append(pltpu.VMEM((1, LANES), F32))
    return pl.pallas_call(
        functools.partial(_outproj_kernel, with_router=with_router, sub=sub),
        grid=(n // tm,),
        in_specs=in_specs,
        out_specs=out_specs,
        out_shape=out_shape,
        scratch_shapes=scratch,
        compiler_params=_params(("arbitrary",)),
        name="outproj",
    )(*args)


def _swiglu_step(h, wg_ref, wu_ref, wd_ref):
    gate = _dot(h, wg_ref[...])
    a = gate * jax.nn.sigmoid(gate) * _dot(h, wu_ref[...])
    return _dot(a.astype(BF16), wd_ref[...])


def _ffn_kernel(*refs, n_cast):
    h_ref, wg_ref, wu_ref, wd_ref, x_ref, gpost_ref, gt_ref = refs[:7]
    cast_in = refs[7:7 + n_cast]
    o_ref = refs[7 + n_cast]
    cast_out = refs[8 + n_cast:]
    f = pl.program_id(1)

    @pl.when(f == 0)
    def _():
        o_ref[...] = jnp.zeros_like(o_ref)

    o_ref[...] += _swiglu_step(h_ref[...], wg_ref, wu_ref, wd_ref)

    @pl.when(f == pl.num_programs(1) - 1)
    def _():
        o_ref[...] = x_ref[...] + gt_ref[...] * _rms(o_ref[...], gpost_ref[...])

    for src, dst in zip(cast_in, cast_out):
        dst[...] = src[...].astype(BF16)


def _expert_cast_jobs(w_gate, w_up, w_down, gi, gf):
    n_e, d, dff = w_gate.shape
    rows = n_e * d // gi
    per_e = d // rows
    assert rows * gi == n_e * d and per_e * rows == d and rows % 16 == 0
    assert dff == gf * MOE_TF
    drows = n_e * dff // (gi * gf)
    assert drows * gi * gf == n_e * dff and drows % 16 == 0
    gu_in = pl.BlockSpec((rows, MOE_TF), lambda i, f: (i, f))
    gu_out = pl.BlockSpec((None, rows, MOE_TF), lambda i, f: (i // per_e * gf + f, i % per_e, 0))
    gu_shape = jax.ShapeDtypeStruct((n_e * gf, d, MOE_TF), BF16)
    dn_spec = pl.BlockSpec((drows, d), lambda i, f: (i * gf + f, 0))
    dn_shape = jax.ShapeDtypeStruct((n_e * dff, d), BF16)
    return [(w_gate.reshape(n_e * d, dff), gu_in, gu_shape, gu_out),
            (w_up.reshape(n_e * d, dff), gu_in, gu_shape, gu_out),
            (w_down.reshape(n_e * dff, d), dn_spec, dn_shape, dn_spec)]


def _ffn(h, wg, wu, wd, x2, gpost, mod, seq, cast_jobs_fn=None):
    n, d = x2.shape
    tm, tf = FFN_TM, FFN_TF
    per_b = seq // tm
    gi, gf = n // tm, wg.shape[0]
    row = lambda i, f: (i, 0)
    jobs = cast_jobs_fn(gi, gf) if cast_jobs_fn else []
    outs = pl.pallas_call(
        functools.partial(_ffn_kernel, n_cast=len(jobs)),
        grid=(gi, gf),
        in_specs=[
            pl.BlockSpec((tm, d), row),
            pl.BlockSpec((None, d, tf), lambda i, f: (f, 0, 0)),
            pl.BlockSpec((None, d, tf), lambda i, f: (f, 0, 0)),
            pl.BlockSpec((tf, d), lambda i, f: (f, 0)),
            pl.BlockSpec((tm, d), row),
            pl.BlockSpec((1, d), lambda i, f: (0, 0)),
            pl.BlockSpec((None, None, 1, d), lambda i, f: (i // per_b, 5, 0, 0)),
        ] + [job[1] for job in jobs],
        out_specs=[pl.BlockSpec((tm, d), row)] + [job[3] for job in jobs],
        out_shape=[jax.ShapeDtypeStruct((n, d), F32)] + [job[2] for job in jobs],
        compiler_params=_params(("arbitrary", "arbitrary")),
        name="ffn",
    )(h, wg, wu, wd, x2, gpost, mod, *[job[0] for job in jobs])
    return outs[0], tuple(outs[1:])


def _slice_major(w, tf):
    d, dff = w.shape
    return w.astype(BF16).reshape(d, dff // tf, tf).transpose(1, 0, 2)


def _dispatch_kernel(p1_ref, p2_ref, fill_ref, h_ref, xs_ref, zero_scr, sems, fill_sem, *, n_rows):
    tm = h_ref.shape[0]
    base = pl.program_id(0) * tm

    @pl.when(pl.program_id(0) == 0)
    def _():
        zero_scr[...] = jnp.zeros_like(zero_scr)
        tail = [pltpu.make_async_copy(zero_scr.at[pl.ds(0, MOE_TILE)],
                                      xs_ref.at[pl.ds(t0, MOE_TILE)], fill_sem)
                for t0 in range(n_rows, xs_ref.shape[0], MOE_TILE)]
        for cp in tail:
            cp.start()
        for cp in tail:
            cp.wait()
        for e in range(N_EXPERTS):
            cp = pltpu.make_async_copy(
                zero_scr, xs_ref.at[pl.ds(pl.multiple_of(fill_ref[e], 8), zero_scr.shape[0])],
                fill_sem)
            cp.start()
            cp.wait()

    def issue(r, carry):
        row = h_ref.at[pl.ds(r, 1)]
        pltpu.make_async_copy(row, xs_ref.at[pl.ds(p1_ref[base + r], 1)], sems.at[0]).start()
        pltpu.make_async_copy(row, xs_ref.at[pl.ds(p2_ref[base + r], 1)], sems.at[1]).start()
        return carry

    lax.fori_loop(0, tm, issue, 0, unroll=8)
    for k in range(TOP_K):
        pltpu.make_async_copy(h_ref, xs_ref.at[pl.ds(0, tm)], sems.at[k]).wait()


def _dispatch(h, p1, p2, fill_start, n_slots):
    n, d = h.shape
    tm = 512
    return pl.pallas_call(
        functools.partial(_dispatch_kernel, n_rows=TOP_K * n),
        grid_spec=pltpu.PrefetchScalarGridSpec(
            num_scalar_prefetch=3,
            grid=(n // tm,),
            in_specs=[pl.BlockSpec((tm, d), lambda i, p1, p2, fs: (i, 0))],
            out_specs=pl.BlockSpec(memory_space=pl.ANY),
            scratch_shapes=[pltpu.VMEM((MOE_TILE + 8, d), F32),
                            pltpu.SemaphoreType.DMA((TOP_K,)),
                            pltpu.SemaphoreType.DMA(())],
        ),
        out_shape=jax.ShapeDtypeStruct((n_slots, d), F32),
        compiler_params=_params(("arbitrary",)),
        name="moe_dispatch",
    )(p1, p2, fill_start, h)


def _moe_ffn_kernel(te_ref, nv_ref, x_ref, wg_hbm, wu_hbm, wd_hbm, y_ref, h_scr,
                    wg_buf, wu_buf, wd_buf, sems, *, tf):
    j = pl.program_id(0)
    n_used = nv_ref[0]
    nf = wd_hbm.shape[1] // tf

    def slices(tile, f, slot):
        e = te_ref[tile]
        c0 = pl.multiple_of(f * tf, tf)
        return (
            pltpu.make_async_copy(wg_hbm.at[e * nf + f], wg_buf.at[slot], sems.at[0, slot]),
            pltpu.make_async_copy(wu_hbm.at[e * nf + f], wu_buf.at[slot], sems.at[1, slot]),
            pltpu.make_async_copy(wd_hbm.at[e, pl.ds(c0, tf), :], wd_buf.at[slot], sems.at[2, slot]),
        )

    y_ref[...] = jnp.zeros_like(y_ref)

    @pl.when(j < n_used)
    def _():
        @pl.when(j == 0)
        def _():
            for cp in slices(0, 0, 0):
                cp.start()

        h_scr[...] = x_ref[...].astype(BF16)

        def step(f, carry):
            slot = lax.rem(j * nf + f, 2)
            for cp in slices(j, f, slot):
                cp.wait()

            @pl.when(f + 1 < nf)
            def _():
                for cp in slices(j, f + 1, 1 - slot):
                    cp.start()

            @pl.when((f + 1 == nf) & (j + 1 < n_used))
            def _():
                for cp in slices(j + 1, 0, 1 - slot):
                    cp.start()

            y_ref[...] += _swiglu_step(h_scr[...], wg_buf.at[slot], wu_buf.at[slot],
                                       wd_buf.at[slot])
            return carry

        lax.fori_loop(0, nf, step, 0)


def _moe_ffn(xs, tile_expert, n_valid, wg, wu, wd):
    n_slots, d = xs.shape
    tm, tf = MOE_TILE, MOE_TF

    return pl.pallas_call(
        functools.partial(_moe_ffn_kernel, tf=tf),
        grid_spec=pltpu.PrefetchScalarGridSpec(
            num_scalar_prefetch=2,
            grid=(n_slots // tm,),
            in_specs=[
                pl.BlockSpec((tm, d), lambda j, te, nv: (jnp.minimum(j, nv[0] - 1), 0)),
                pl.BlockSpec(memory_space=pl.ANY),
                pl.BlockSpec(memory_space=pl.ANY),
                pl.BlockSpec(memory_space=pl.ANY),
            ],
            out_specs=pl.BlockSpec((tm, d), lambda j, te, nv: (j, 0)),
            scratch_shapes=[
                pltpu.VMEM((tm, d), BF16),
                pltpu.VMEM((2, d, tf), BF16),
                pltpu.VMEM((2, d, tf), BF16),
                pltpu.VMEM((2, tf, d), BF16),
                pltpu.SemaphoreType.DMA((3, 2)),
            ],
        ),
        out_shape=jax.ShapeDtypeStruct((n_slots, d), F32),
        compiler_params=_params(("arbitrary",)),
        name="moe_ffn",
    )(tile_expert, n_valid, xs, wg, wu, wd)


def _combine_kernel(p1_ref, p2_ref, ys_ref, route_ref, x_ref, gpost_ref, gt_ref, o_ref,
                    y1_scr, y2_scr, sems):
    tm = x_ref.shape[0]
    base = pl.program_id(0) * tm

    def issue(r, carry):
        pltpu.make_async_copy(ys_ref.at[pl.ds(p1_ref[base + r], 1)], y1_scr.at[pl.ds(r, 1)],
                              sems.at[0]).start()
        pltpu.make_async_copy(ys_ref.at[pl.ds(p2_ref[base + r], 1)], y2_scr.at[pl.ds(r, 1)],
                              sems.at[1]).start()
        return carry


    lax.fori_loop(0, tm, issue, 0, unroll=8)
    pltpu.make_async_copy(ys_ref.at[pl.ds(0, tm)], y1_scr, sems.at[0]).wait()
    pltpu.make_async_copy(ys_ref.at[pl.ds(0, tm)], y2_scr, sems.at[1]).wait()
    lane = lax.broadcasted_iota(jnp.int32, (1, LANES), 1)
    route = route_ref[...]
    w1 = jnp.sum(jnp.where(lane == RT_W1, route, 0.0), axis=-1, keepdims=True)
    w2 = jnp.sum(jnp.where(lane == RT_W2, route, 0.0), axis=-1, keepdims=True)
    y = w1 * y1_scr[...] + w2 * y2_scr[...]
    o_ref[...] = x_ref[...] + gt_ref[...] * _rms(y, gpost_ref[...])


def _combine(ys, p1, p2, route, x2, gpost, mod, seq):
    n, d = x2.shape
    tm = 512
    per_b = seq // tm
    row = lambda i, p1, p2: (i, 0)
    return pl.pallas_call(
        _combine_kernel,
        grid_spec=pltpu.PrefetchScalarGridSpec(
            num_scalar_prefetch=2,
            grid=(n // tm,),
            in_specs=[
                pl.BlockSpec(memory_space=pl.ANY),
                pl.BlockSpec((tm, LANES), row),
                pl.BlockSpec((tm, d), row),
                pl.BlockSpec((1, d), lambda i, p1, p2: (0, 0)),
                pl.BlockSpec((None, None, 1, d), lambda i, p1, p2: (i // per_b, 5, 0, 0)),
            ],
            out_specs=pl.BlockSpec((tm, d), row),
            scratch_shapes=[pltpu.VMEM((tm, d), F32), pltpu.VMEM((tm, d), F32),
                            pltpu.SemaphoreType.DMA((TOP_K,))],
        ),
        out_shape=jax.ShapeDtypeStruct((n, d), F32),
        compiler_params=_params(("arbitrary",)),
        name="moe_combine",
    )(p1, p2, ys, route, x2, gpost, mod)


def _route_plan(route, counts, n_tiles):
    e1 = route[:, RT_E1].astype(jnp.int32)
    e2 = route[:, RT_E2].astype(jnp.int32)
    cnt = counts[0, :N_EXPERTS].astype(jnp.int32)
    size = (cnt + MOE_TILE - 1) // MOE_TILE * MOE_TILE
    end = jnp.cumsum(size)
    start = end - size
    p1 = start[e1] + route[:, RT_R1].astype(jnp.int32)
    p2 = start[e2] + route[:, RT_R2].astype(jnp.int32)
    n_valid = end[-1] // MOE_TILE
    tile_start = jnp.minimum(jnp.arange(n_tiles), n_valid - 1) * MOE_TILE
    tile_expert = jnp.sum(tile_start[:, None] >= end[None, :], axis=1).astype(jnp.int32)
    fill_start = (start + cnt) // 8 * 8
    return p1, p2, fill_start, tile_expert, n_valid.reshape(1).astype(jnp.int32)


def _pack_w_in(w_in):
    d = w_in.shape[0]
    gw = 4 * LANES
    offs = [0]
    for width in (gw, gw, gw, gw, gw, gw, gw, MLSTM_HEADS, MLSTM_HEADS,
                  2 * SWA_GROUP_HEADS * HEAD_W, SWA_GROUPS * HEAD_W, SWA_GROUPS * HEAD_W):
        offs.append(offs[-1] + width)
    w_in = w_in.astype(BF16)
    (mb_q, mb_k, mb_v, ml_q, ml_k, ml_v, ml_o, ml_i, ml_f, sw_q, sw_k, sw_v) = [
        w_in[:, offs[j]:offs[j + 1]] for j in range(12)]
    gates = jnp.concatenate(
        [ml_i, ml_f, jnp.zeros((d, 2 * LANES - 2 * MLSTM_HEADS), w_in.dtype)], axis=1)

    def dup(w):
        heads = [w[:, g * HEAD_W:(g + 1) * HEAD_W] for g in range(SWA_GROUPS)]
        return jnp.concatenate([t for hd in heads for t in (hd, hd)], axis=1)

    packed = jnp.concatenate(
        [sw_q, mb_q, mb_k, mb_v, ml_q, ml_k, ml_v, ml_o, gates, dup(sw_k), dup(sw_v)], axis=1)
    assert packed.shape[1] == PROJ_BLOCKS * LANES
    return packed.astype(BF16)


def kernel(x, c, ada_w, ada_b, g_pre_mix, g_post_mix, g_pre_ffn, g_post_ffn, w_in, w_out, conv_w,
           conv_b, igate_b, fgate_b, mlstm_norm_w, swa_sinks, ffn_w_gate, ffn_w_up, ffn_w_down,
           moe_router, moe_w_gate, moe_w_up, moe_w_down):
    batch, seq, d = x.shape
    depth = ada_w.shape[0]
    n = batch * seq
    x2 = x.reshape(n, d)
    mod_all = _adaln(c, ada_w, ada_b).reshape(depth, batch, 6, 1, d)
    for l in range(depth):
        mod = mod_all[l]
        proj = _inproj(x2, g_pre_mix[l].reshape(1, d), mod, _pack_w_in(w_in[l]), seq)
        y_moba = _moba(proj, batch, seq)
        gate_bias = jnp.concatenate(
            [igate_b[l], fgate_b[l], jnp.zeros((LANES - 2 * MLSTM_HEADS,), F32)]).reshape(1, LANES)
        y_mlstm = _mlstm(proj, conv_w[l], conv_b[l].reshape(1, -1), gate_bias,
                         mlstm_norm_w[l].reshape(1, -1), batch, seq)
        y_swa = _swa(proj, swa_sinks[l], batch, seq)
        j = l // 2
        w_router = None
        if l % 2 == 1:
            w_router = jnp.pad(moe_router[j], ((0, 0), (0, LANES - N_EXPERTS))).astype(BF16)
        outs = _outproj(y_moba, y_mlstm, y_swa, w_out[l].astype(BF16), x2,
                        g_post_mix[l].reshape(1, d), g_pre_ffn[l].reshape(1, d), mod, seq, w_router)
        gpost = g_post_ffn[l].reshape(1, d)
        if l % 2 == 0:
            x2, h = outs
            jobs_fn = None
            if l + 1 < depth:
                jn = (l + 1) // 2
                jobs_fn = functools.partial(_expert_cast_jobs, moe_w_gate[jn], moe_w_up[jn],
                                            moe_w_down[jn])
            x2, moe_bf16 = _ffn(h, _slice_major(ffn_w_gate[j], FFN_TF),
                                _slice_major(ffn_w_up[j], FFN_TF), ffn_w_down[j].astype(BF16),
                                x2, gpost, mod, seq, jobs_fn)
        else:
            x2, h, route, counts = outs
            n_tiles = (TOP_K * n) // MOE_TILE + N_EXPERTS + 2
            p1, p2, fill_start, tile_expert, n_valid = _route_plan(route, counts, n_tiles)
            xs = _dispatch(h, p1, p2, fill_start, n_tiles * MOE_TILE)
            wg, wu, wd = moe_bf16
            ys = _moe_ffn(xs, tile_expert, n_valid, wg, wu, wd.reshape(N_EXPERTS, -1, d))
            x2 = _combine(ys, p1, p2, route, x2, gpost, mod, seq)
    return x2.reshape(batch, seq, d)
```

```python
import functools

import jax
import jax.numpy as jnp
from jax import lax
from jax.experimental import pallas as pl
from jax.experimental.pallas import tpu as pltpu

F32 = jnp.float32
BF16 = jnp.bfloat16

LANES = 128
MXU_W = 256
HEAD_W = 64
MOBA_BLOCK = 256
MOBA_TOPK = 3
MOBA_PAIRS = 4
MLSTM_HEADS = 4
MLSTM_CHUNK = 256
MLSTM_CONV = 4
MLSTM_GATE_CAP = 15.0
SWA_WINDOW = 128
SWA_GROUPS = 2
SWA_GROUP_HEADS = 8
N_EXPERTS = 8
RMS_EPS = 1e-6
NEG_INF = float("-inf")
MASK_NEG = -1e30
MOE_TILE = 512
MOE_TF = 256
MOE_WBUF = 3
FFN_TM, FFN_TF = 512, 512
TOP_K = 2
RT_E1, RT_E2, RT_W1, RT_W2, RT_R1, RT_R2 = range(6)
VMEM_LIMIT = 56 * 1024 * 1024

PB_SWA_Q = 0
PB_MOBA_Q = 8
PB_MOBA_K = 12
PB_MOBA_V = 16
PB_ML_Q = 20
PB_ML_K = 24
PB_ML_V = 28
PB_ML_O = 32
PB_ML_G = 36
PB_SWA_K = 37
PB_SWA_V = 38
PROJ_BLOCKS = 39


def _params(sem):
    return pltpu.CompilerParams(dimension_semantics=sem, vmem_limit_bytes=VMEM_LIMIT)


def _rms(x, g):
    return x * lax.rsqrt(jnp.mean(x * x, axis=-1, keepdims=True) + RMS_EPS) * g


def _dot(a, b):
    return jnp.dot(a, b, preferred_element_type=F32)


def _dot_nt(a, b):
    return lax.dot_general(a, b, (((1,), (1,)), ((), ())), preferred_element_type=F32)


def _adaln_kernel(c_ref, w_ref, b_ref, o_ref):
    c = c_ref[...]
    cond = (c * jax.nn.sigmoid(c)).astype(BF16)
    o_ref[...] = _dot(cond, w_ref[...].astype(BF16)) + b_ref[...]


def _adaln(c, ada_w, ada_b):
    depth, d, n6 = ada_w.shape
    b = c.shape[0]
    tn = 1024
    return pl.pallas_call(
        _adaln_kernel,
        grid=(depth, n6 // tn),
        in_specs=[
            pl.BlockSpec((b, d), lambda l, j: (0, 0)),
            pl.BlockSpec((None, d, tn), lambda l, j: (l, 0, j)),
            pl.BlockSpec((None, 1, tn), lambda l, j: (l, 0, j)),
        ],
        out_specs=pl.BlockSpec((None, b, tn), lambda l, j: (l, 0, j)),
        out_shape=jax.ShapeDtypeStruct((depth, b, n6), F32),
        compiler_params=_params(("arbitrary", "arbitrary")),
        name="adaln",
    )(c, ada_w, ada_b.reshape(depth, 1, n6))


def _inproj_kernel(x_ref, g_ref, sc_ref, sh_ref, w_ref, o_ref, *, tn):
    h = (_rms(x_ref[...], g_ref[...]) * (1.0 + sc_ref[...]) + sh_ref[...]).astype(BF16)
    nc = w_ref.shape[1]
    for c0 in range(0, nc, tn):
        c1 = min(c0 + tn, nc)
        o_ref[:, c0:c1] = _dot(h, w_ref[:, c0:c1])


def _inproj(x2, g, mod, w, seq):
    n, d = x2.shape
    nc = w.shape[1]
    tm, tn = 256, 7 * MXU_W
    per_b = seq // tm
    return pl.pallas_call(
        functools.partial(_inproj_kernel, tn=tn),
        grid=(n // tm,),
        in_specs=[
            pl.BlockSpec((tm, d), lambda i: (i, 0)),
            pl.BlockSpec((1, d), lambda i: (0, 0)),
            pl.BlockSpec((None, None, 1, d), lambda i: (i // per_b, 1, 0, 0)),
            pl.BlockSpec((None, None, 1, d), lambda i: (i // per_b, 0, 0, 0)),
            pl.BlockSpec((d, nc), lambda i: (0, 0), pipeline_mode=pl.Buffered(1)),
        ],
        out_specs=pl.BlockSpec((tm, nc), lambda i: (i, 0)),
        out_shape=jax.ShapeDtypeStruct((n, nc), F32),
        compiler_params=_params(("arbitrary",)),
        name="inproj",
    )(x2, g, mod, mod, w)


def _moba_kernel(q_ref, k_ref, v_ref, o_ref, ka_scr, vb_scr):
    seq = k_ref.shape[0]
    blk = MOBA_BLOCK
    n_blk = seq // blk
    lane = lax.broadcasted_iota(jnp.int32, (1, LANES), 1)
    k = k_ref[...]
    vb_scr[...] = v_ref[...].astype(BF16)
    kmean = jnp.mean(k.reshape(n_blk, blk, LANES), axis=1)
    kblk = lax.shift_right_logical(lax.broadcasted_iota(jnp.int32, (seq, 1), 0), 8)
    ri = lax.broadcasted_iota(jnp.int32, (blk, blk), 0)
    ci = lax.broadcasted_iota(jnp.int32, (blk, blk), 1)
    causal_bias = jnp.where(ri >= ci, 0.0, MASK_NEG)
    scale = HEAD_W ** -0.5
    in_head, slot, kmp = [], [], []
    for hh in range(2):
        base = HEAD_W * (1 - hh)
        in_head.append((lane >= HEAD_W * hh) & (lane < HEAD_W * (hh + 1)))
        slot.append(lane - base)
        ka_scr[hh] = jnp.where(in_head[hh], k, jnp.where(slot[hh] == kblk, 1.0, 0.0)).astype(BF16)
        rows = [jnp.where(in_head[hh], kmean, 0.0), jnp.zeros((LANES - base - n_blk, LANES), F32)]
        if base:
            rows.insert(0, jnp.zeros((base, LANES), F32))
        kmp.append(jnp.concatenate(rows, axis=0).astype(BF16))

    for i in range(n_blk):
        q = q_ref[i * blk:(i + 1) * blk, :]
        outs = []
        for hh in range(2):
            base = HEAD_W * (1 - hh)
            qa = jnp.where(in_head[hh], q * scale, 0.0)
            if i > MOBA_TOPK:
                valid = (slot[hh] >= 0) & (slot[hh] < i)
                qm = jnp.where(in_head[hh], q, 0.0).astype(BF16)
                gate = jnp.where(valid, _dot_nt(qm, kmp[hh]), NEG_INF)
                beaten_by = jnp.zeros((blk, LANES), jnp.int32)
                for j in range(i):
                    gj = gate[:, base + j:base + j + 1]
                    beats = (gj > gate) | ((gj == gate) & (slot[hh] > j))
                    beaten_by = beaten_by + beats.astype(jnp.int32)
                qa = jnp.where(valid & (beaten_by >= MOBA_TOPK), MASK_NEG, qa)
            s = _dot_nt(qa.astype(BF16), ka_scr[hh, 0:(i + 1) * blk, :])
            s_own = s[:, i * blk:] + causal_bias
            m = jnp.max(s_own, axis=-1, keepdims=True)
            if i:
                s_past = s[:, :i * blk]
                m = jnp.maximum(m, jnp.max(s_past, axis=-1, keepdims=True))
            p_own = jnp.exp(s_own - m)
            l = jnp.sum(p_own, axis=-1, keepdims=True)
            acc = _dot(p_own.astype(BF16), vb_scr[i * blk:(i + 1) * blk, :])
            if i:
                p_past = jnp.exp(s_past - m)
                l = l + jnp.sum(p_past, axis=-1, keepdims=True)
                acc = acc + _dot(p_past.astype(BF16), vb_scr[0:i * blk, :])
            outs.append(acc / l)
        o_ref[i * blk:(i + 1) * blk, :] = jnp.where(lane < HEAD_W, outs[0], outs[1]).astype(o_ref.dtype)


def _moba(proj, batch, seq):
    n = proj.shape[0]
    return pl.pallas_call(
        _moba_kernel,
        grid=(batch, MOBA_PAIRS),
        in_specs=[
            pl.BlockSpec((seq, LANES), lambda b, p: (b, PB_MOBA_Q + p)),
            pl.BlockSpec((seq, LANES), lambda b, p: (b, PB_MOBA_K + p)),
            pl.BlockSpec((seq, LANES), lambda b, p: (b, PB_MOBA_V + p)),
        ],
        out_specs=pl.BlockSpec((seq, LANES), lambda b, p: (b, p)),
        out_shape=jax.ShapeDtypeStruct((n, MOBA_PAIRS * LANES), BF16),
        scratch_shapes=[
            pltpu.VMEM((2, seq, LANES), BF16),
            pltpu.VMEM((seq, LANES), BF16),
        ],
        compiler_params=_params(("arbitrary", "arbitrary")),
        name="moba",
    )(proj, proj, proj)


def _causal_conv_silu(x, tail, w, b):
    row = lax.broadcasted_iota(jnp.int32, (8, 1), 0)
    y = b + w[MLSTM_CONV - 1:MLSTM_CONV, :] * x
    for shift in range(1, MLSTM_CONV):
        xr = pltpu.roll(x, shift, axis=0)
        head = jnp.where(row < shift, pltpu.roll(tail, shift, axis=0), xr[0:8, :])
        xs = jnp.concatenate([head, xr[8:, :]], axis=0)
        y = y + w[MLSTM_CONV - 1 - shift:MLSTM_CONV - shift, :] * xs
    return y * jax.nn.sigmoid(y)


def _mlstm_kernel(q_ref, k_ref, v_ref, og_ref, gate_ref, cwq_ref, cwk_ref, cbq_ref, cbk_ref,
                  gb_ref, nw_ref, o_ref, qtail_scr, ktail_scr, c_scr, n_scr, m_scr):
    ts = L = q_ref.shape[0]
    lane = lax.broadcasted_iota(jnp.int32, (1, LANES), 1)
    ri = lax.broadcasted_iota(jnp.int32, (L, L), 0)
    ci = lax.broadcasted_iota(jnp.int32, (L, L), 1)
    causal = ri >= ci
    lower = causal.astype(F32)
    upper = (ri <= ci).astype(F32)

    @pl.when(pl.program_id(1) == 0)
    def _():
        qtail_scr[...] = jnp.zeros_like(qtail_scr)
        ktail_scr[...] = jnp.zeros_like(ktail_scr)
        c_scr[...] = jnp.zeros_like(c_scr)
        n_scr[...] = jnp.zeros_like(n_scr)
        m_scr[...] = jnp.zeros_like(m_scr)

    t = MLSTM_GATE_CAP * jnp.tanh((gate_ref[...] + gb_ref[...]) / MLSTM_GATE_CAP)
    a_col = jnp.where(lane < MLSTM_HEADS, t, jax.nn.log_sigmoid(t))
    a_row = a_col.T[0:8, :]

    xq, xk = q_ref[...], k_ref[...]
    qc = _causal_conv_silu(xq, qtail_scr[...], cwq_ref[...], cbq_ref[...]).astype(BF16)
    kc = (_causal_conv_silu(xk, ktail_scr[...], cwk_ref[...], cbk_ref[...])
          * (LANES ** -0.5)).astype(BF16)
    qtail_scr[...] = xq[ts - 8:, :]
    ktail_scr[...] = xk[ts - 8:, :]

    b_c = jnp.dot(lower, a_col, precision=lax.Precision.HIGHEST, preferred_element_type=F32)
    b_r = jnp.dot(a_row, upper, precision=lax.Precision.HIGHEST, preferred_element_type=F32)

    local = []
    for hd in range(MLSTM_HEADS):
        cols = slice(hd * LANES, (hd + 1) * LANES)
        q, k = qc[:, cols], kc[:, cols]
        v = v_ref[:, cols].astype(BF16)
        fl = hd + MLSTM_HEADS
        b_col, li_col = b_c[:, fl:fl + 1], a_col[:, hd:hd + 1]
        b_row, li_row = b_r[fl:fl + 1, :], a_row[hd:hd + 1, :]
        b_last = b_row[:, L - 1:L]
        d_log = jnp.where(causal, b_col - b_row + li_row, NEG_INF)
        d_max = jnp.max(d_log, axis=1, keepdims=True)
        qk = _dot_nt(q, k) * jnp.exp(d_log - d_max)
        pv = _dot(qk.astype(BF16), v)
        qk_sum = jnp.sum(qk, axis=1, keepdims=True)
        a_max = jnp.max(b_last - b_row + li_row, axis=1, keepdims=True)
        kw = k.astype(F32) * jnp.exp(b_last - b_col + li_col - a_max)
        c_in = _dot(kw.T.astype(BF16), v)
        n_in = jnp.sum(kw, axis=0, keepdims=True)
        local.append((q, b_col, b_last, d_max, pv, qk_sum, a_max, c_in, n_in))

    for hd in range(MLSTM_HEADS):
        cols = slice(hd * LANES, (hd + 1) * LANES)
        q, b_col, b_last, d_max, pv, qk_sum, a_max, c_in, n_in = local[hd]
        c_st, n_st, m_st = c_scr[hd], n_scr[hd], m_scr[hd][:, 0:1]
        inter_log = b_col + m_st
        m_out = jnp.maximum(inter_log, d_max)
        w_inter = jnp.exp(inter_log - m_out)
        w_local = jnp.exp(d_max - m_out)
        num = w_local * pv + w_inter * _dot(q, c_st.astype(BF16))
        den = w_local * qk_sum + w_inter * jnp.sum(q.astype(F32) * n_st, axis=1, keepdims=True)
        hv = num / jnp.maximum(jnp.abs(den), jnp.exp(-m_out))
        hn = _rms(hv, nw_ref[:, cols])
        o_ref[:, cols] = (jax.nn.sigmoid(og_ref[:, cols]) * hn).astype(o_ref.dtype)
        m_new = jnp.maximum(b_last + m_st, a_max)
        decay = jnp.exp(b_last + m_st - m_new)
        inject = jnp.exp(a_max - m_new)
        c_scr[hd] = decay * c_st + inject * c_in
        n_scr[hd] = decay * n_st + inject * n_in
        m_scr[hd] = jnp.broadcast_to(m_new, (1, LANES))


def _mlstm(proj, conv_w, conv_b, gate_bias, norm_w, batch, seq):
    n = proj.shape[0]
    hw = MLSTM_HEADS
    gw = hw * LANES
    ts = MLSTM_CHUNK
    tiles = seq // ts

    def col(base):
        return pl.BlockSpec((ts, gw), lambda b, t: (b * tiles + t, base // hw))

    def vec(rows, blk):
        return pl.BlockSpec((rows, gw), lambda b, t: (0, blk))

    return pl.pallas_call(
        _mlstm_kernel,
        grid=(batch, tiles),
        in_specs=[
            col(PB_ML_Q), col(PB_ML_K), col(PB_ML_V), col(PB_ML_O),
            pl.BlockSpec((ts, LANES), lambda b, t: (b * tiles + t, PB_ML_G)),
            vec(MLSTM_CONV, 0), vec(MLSTM_CONV, 1), vec(1, 0), vec(1, 1),
            pl.BlockSpec((1, LANES), lambda b, t: (0, 0)),
            vec(1, 0),
        ],
        out_specs=pl.BlockSpec((ts, gw), lambda b, t: (b * tiles + t, 0)),
        out_shape=jax.ShapeDtypeStruct((n, gw), BF16),
        scratch_shapes=[
            pltpu.VMEM((8, gw), F32),
            pltpu.VMEM((8, gw), F32),
            pltpu.VMEM((hw, LANES, LANES), F32),
            pltpu.VMEM((hw, 1, LANES), F32),
            pltpu.VMEM((hw, 1, LANES), F32),
        ],
        compiler_params=_params(("arbitrary", "arbitrary")),
        name="mlstm",
    )(proj, proj, proj, proj, proj, conv_w, conv_w, conv_b, conv_b, gate_bias, norm_w)


def _swa_kernel(sink_ref, q_ref, k_ref, v_ref, o_ref, kb_scr, vb_scr):
    g = pl.program_id(1)
    seq = q_ref.shape[0]
    W = SWA_WINDOW
    n_pairs = SWA_GROUP_HEADS // 2
    lane = lax.broadcasted_iota(jnp.int32, (1, LANES), 1)
    low = lane < HEAD_W
    keep = low == (g == 0)
    for src, dst in ((k_ref, kb_scr), (v_ref, vb_scr)):
        both = src[...]
        dst[...] = jnp.where(keep, both, pltpu.roll(both, HEAD_W, axis=1)).astype(BF16)
    scale = HEAD_W ** -0.5
    sinks = [sink_ref[g * SWA_GROUP_HEADS + hd] for hd in range(SWA_GROUP_HEADS)]
    r = lax.broadcasted_iota(jnp.int32, (W, 2 * W), 0)
    c = lax.broadcasted_iota(jnp.int32, (W, 2 * W), 1)
    bias_first = jnp.where(c <= r, 0.0, MASK_NEG)
    bias_rest = jnp.where((c > r) & (c <= r + W), 0.0, MASK_NEG)

    def block(nb, carry):
        start = pl.multiple_of(jnp.maximum(nb - 1, 0) * W, W)
        r0 = pl.multiple_of(nb * W, W)
        qn = q_ref[pl.ds(r0, W), :] * scale
        kband = kb_scr[pl.ds(start, 2 * W), :]
        vband = vb_scr[pl.ds(start, 2 * W), :]
        parts = []
        for p in range(n_pairs):
            qp = qn[:, p * LANES:(p + 1) * LANES]
            parts.append(jnp.where(low, qp, 0.0).astype(BF16))
            parts.append(jnp.where(low, 0.0, qp).astype(BF16))
        s_all = _dot_nt(jnp.concatenate(parts, axis=0), kband)
        bias = jnp.where(nb == 0, bias_first, bias_rest)
        probs, inv_l = [], []
        for hd in range(SWA_GROUP_HEADS):
            s = s_all[hd * W:(hd + 1) * W, :] + bias
            m = jnp.maximum(jnp.max(s, axis=-1, keepdims=True), sinks[hd])
            e = jnp.exp(s - m)
            inv_l.append(1.0 / (jnp.sum(e, axis=-1, keepdims=True) + jnp.exp(sinks[hd] - m)))
            probs.append(e.astype(BF16))
        o_all = _dot(jnp.concatenate(probs, axis=0), vband)
        outs = []
        for p in range(n_pairs):
            lo = o_all[(2 * p) * W:(2 * p + 1) * W, :] * inv_l[2 * p]
            hi = o_all[(2 * p + 1) * W:(2 * p + 2) * W, :] * inv_l[2 * p + 1]
            outs.append(jnp.where(low, lo, hi))
        o_ref[pl.ds(r0, W), :] = jnp.concatenate(outs, axis=1).astype(o_ref.dtype)
        return carry

    lax.fori_loop(0, seq // W, block, 0)


def _swa(proj, sinks, batch, seq):
    n = proj.shape[0]
    gw = SWA_GROUP_HEADS * HEAD_W
    gb = gw // LANES
    return pl.pallas_call(
        _swa_kernel,
        grid=(batch, SWA_GROUPS),
        in_specs=[
            pl.BlockSpec(memory_space=pltpu.SMEM),
            pl.BlockSpec((seq, gw), lambda b, g: (b, PB_SWA_Q // gb + g)),
            pl.BlockSpec((seq, LANES), lambda b, g: (b, PB_SWA_K)),
            pl.BlockSpec((seq, LANES), lambda b, g: (b, PB_SWA_V)),
        ],
        out_specs=pl.BlockSpec((seq, gw), lambda b, g: (b, g)),
        out_shape=jax.ShapeDtypeStruct((n, SWA_GROUPS * gw), BF16),
        scratch_shapes=[pltpu.VMEM((seq, LANES), BF16), pltpu.VMEM((seq, LANES), BF16)],
        compiler_params=_params(("arbitrary", "arbitrary")),
        name="swa",
    )(sinks, proj, proj, proj)


def _outproj_kernel(*refs, with_router, sub):
    if with_router:
        (ya_ref, yb_ref, yc_ref, w_ref, x_ref, gpost_ref, gt_ref, gpre_ref, sc_ref, sh_ref,
         wr_ref, xo_ref, h_ref, route_ref, count_ref, count_scr) = refs
    else:
        (ya_ref, yb_ref, yc_ref, w_ref, x_ref, gpost_ref, gt_ref, gpre_ref, sc_ref, sh_ref,
         xo_ref, h_ref) = refs
    wa = ya_ref.shape[1]
    wb = yb_ref.shape[1]
    if with_router:
        @pl.when(pl.program_id(0) == 0)
        def _():
            count_scr[...] = jnp.zeros_like(count_scr)

    for r0 in range(0, x_ref.shape[0], sub):
        rows = slice(r0, r0 + sub)
        y = _dot(ya_ref[rows, :], w_ref[0:wa, :])
        y = y + _dot(yb_ref[rows, :], w_ref[wa:wa + wb, :])
        y = y + _dot(yc_ref[rows, :], w_ref[wa + wb:, :])
        xn = x_ref[rows, :] + gt_ref[...] * _rms(y, gpost_ref[...])
        xo_ref[rows, :] = xn
        hb = (_rms(xn, gpre_ref[...]) * (1.0 + sc_ref[...]) + sh_ref[...]).astype(BF16)
        h_ref[rows, :] = hb.astype(h_ref.dtype)
        if not with_router:
            continue
        tm = sub
        lane = lax.broadcasted_iota(jnp.int32, (1, LANES), 1)
        logits = jnp.where(lane < N_EXPERTS, _dot(hb, wr_ref[...]), NEG_INF)
        m1 = jnp.max(logits, axis=-1, keepdims=True)
        i1 = jnp.min(jnp.where(logits == m1, lane, LANES), axis=-1, keepdims=True)
        rest = jnp.where(lane == i1, NEG_INF, logits)
        m2 = jnp.max(rest, axis=-1, keepdims=True)
        i2 = jnp.min(jnp.where(rest == m2, lane, LANES), axis=-1, keepdims=True)
        e2 = jnp.exp(m2 - m1)
        w1 = 1.0 / (1.0 + e2)
        picked = jnp.where((lane == i1) | (lane == i2), 1.0, 0.0)
        ri = lax.broadcasted_iota(jnp.int32, (tm, tm), 0)
        ci = lax.broadcasted_iota(jnp.int32, (tm, tm), 1)
        before = jnp.where(ri > ci, 1.0, 0.0).astype(BF16)
        rank = _dot(before, picked.astype(BF16)) + count_scr[...]
        r1 = jnp.sum(jnp.where(lane == i1, rank, 0.0), axis=-1, keepdims=True)
        r2 = jnp.sum(jnp.where(lane == i2, rank, 0.0), axis=-1, keepdims=True)
        count_scr[...] += jnp.sum(picked, axis=0, keepdims=True)
        rec = jnp.where(lane == RT_E1, i1.astype(F32), 0.0)
        for slot_lane, val in ((RT_E2, i2.astype(F32)), (RT_W1, w1), (RT_W2, e2 * w1),
                               (RT_R1, r1), (RT_R2, r2)):
            rec = jnp.where(lane == slot_lane, val, rec)
        route_ref[rows, :] = rec
    if with_router:
        count_ref[...] = jnp.broadcast_to(count_scr[...], count_ref.shape)


def _outproj(ya, yb, yc, w, x2, gpost, gpre, mod, seq, w_router=None):
    n, d = x2.shape
    tm, sub = 512, 256
    per_b = seq // tm
    with_router = w_router is not None

    def rows(width):
        return pl.BlockSpec((tm, width), lambda i: (i, 0))

    def vec():
        return pl.BlockSpec((1, d), lambda i: (0, 0))

    def modrow(k):
        return pl.BlockSpec((None, None, 1, d), lambda i: (i // per_b, k, 0, 0))

    in_specs = [rows(ya.shape[1]), rows(yb.shape[1]), rows(yc.shape[1]),
                pl.BlockSpec((d, d), lambda i: (0, 0), pipeline_mode=pl.Buffered(1)), rows(d),
                vec(), modrow(2), vec(), modrow(4), modrow(3)]
    args = [ya, yb, yc, w, x2, gpost, mod, gpre, mod, mod]
    out_specs = [rows(d), rows(d)]
    out_shape = [jax.ShapeDtypeStruct((n, d), F32),
                 jax.ShapeDtypeStruct((n, d), F32 if with_router else BF16)]
    scratch = []
    if with_router:
        in_specs.append(pl.BlockSpec((d, LANES), lambda i: (0, 0)))
        args.append(w_router)
        out_specs += [rows(LANES), pl.BlockSpec((8, LANES), lambda i: (0, 0))]
        out_shape += [jax.ShapeDtypeStruct((n, LANES), F32), jax.ShapeDtypeStruct((8, LANES), F32)]
        scratch.append(pltpu.VMEM((1, LANES), F32))
    return pl.pallas_call(
        functools.partial(_outproj_kernel, with_router=with_router, sub=sub),
        grid=(n // tm,),
        in_specs=in_specs,
        out_specs=out_specs,
        out_shape=out_shape,
        scratch_shapes=scratch,
        compiler_params=_params(("arbitrary",)),
        name="outproj",
    )(*args)


def _swiglu_step(h, wg_ref, wu_ref, wd_ref):
    gate = _dot(h, wg_ref[...])
    a = gate * jax.nn.sigmoid(gate) * _dot(h, wu_ref[...])
    return _dot(a.astype(BF16), wd_ref[...])


def _ffn_kernel(*refs, n_cast):
    h_ref, wg_ref, wu_ref, wd_ref, x_ref, gpost_ref, gt_ref = refs[:7]
    cast_in = refs[7:7 + n_cast]
    o_ref = refs[7 + n_cast]
    cast_out = refs[8 + n_cast:]
    f = pl.program_id(1)

    @pl.when(f == 0)
    def _():
        o_ref[...] = jnp.zeros_like(o_ref)

    o_ref[...] += _swiglu_step(h_ref[...], wg_ref, wu_ref, wd_ref)

    @pl.when(f == pl.num_programs(1) - 1)
    def _():
        o_ref[...] = x_ref[...] + gt_ref[...] * _rms(o_ref[...], gpost_ref[...])

    for src, dst in zip(cast_in, cast_out):
        dst[...] = src[...].astype(BF16)


def _expert_cast_jobs(w_gate, w_up, w_down, gi, gf):
    n_e, d, dff = w_gate.shape
    rows = n_e * d // gi
    per_e = d // rows
    assert rows * gi == n_e * d and per_e * rows == d and rows % 16 == 0
    assert dff == gf * MOE_TF
    drows = n_e * dff // (gi * gf)
    assert drows * gi * gf == n_e * dff and drows % 16 == 0
    gu_in = pl.BlockSpec((rows, MOE_TF), lambda i, f: (i, f))
    gu_out = pl.BlockSpec((None, rows, MOE_TF), lambda i, f: (i // per_e * gf + f, i % per_e, 0))
    gu_shape = jax.ShapeDtypeStruct((n_e * gf, d, MOE_TF), BF16)
    dn_spec = pl.BlockSpec((drows, d), lambda i, f: (i * gf + f, 0))
    dn_shape = jax.ShapeDtypeStruct((n_e * dff, d), BF16)
    return [(w_gate.reshape(n_e * d, dff), gu_in, gu_shape, gu_out),
            (w_up.reshape(n_e * d, dff), gu_in, gu_shape, gu_out),
            (w_down.reshape(n_e * dff, d), dn_spec, dn_shape, dn_spec)]


def _ffn(h, wg, wu, wd, x2, gpost, mod, seq, cast_jobs_fn=None):
    n, d = x2.shape
    tm, tf = FFN_TM, FFN_TF
    per_b = seq // tm
    gi, gf = n // tm, wg.shape[1] // tf
    row = lambda i, f: (i, 0)
    jobs = cast_jobs_fn(gi, gf) if cast_jobs_fn else []
    outs = pl.pallas_call(
        functools.partial(_ffn_kernel, n_cast=len(jobs)),
        grid=(gi, gf),
        in_specs=[
            pl.BlockSpec((tm, d), row),
            pl.BlockSpec((d, tf), lambda i, f: (0, f)),
            pl.BlockSpec((d, tf), lambda i, f: (0, f)),
            pl.BlockSpec((tf, d), lambda i, f: (f, 0)),
            pl.BlockSpec((tm, d), row),
            pl.BlockSpec((1, d), lambda i, f: (0, 0)),
            pl.BlockSpec((None, None, 1, d), lambda i, f: (i // per_b, 5, 0, 0)),
        ] + [job[1] for job in jobs],
        out_specs=[pl.BlockSpec((tm, d), row)] + [job[3] for job in jobs],
        out_shape=[jax.ShapeDtypeStruct((n, d), F32)] + [job[2] for job in jobs],
        compiler_params=_params(("arbitrary", "arbitrary")),
        name="ffn",
    )(h, wg, wu, wd, x2, gpost, mod, *[job[0] for job in jobs])
    return outs[0], tuple(outs[1:])


def _dispatch_kernel(p1_ref, p2_ref, fill_ref, h_ref, xs_ref, zero_scr, sems, fill_sem, *, n_rows):
    tm = h_ref.shape[0]
    base = pl.program_id(0) * tm

    @pl.when(pl.program_id(0) == 0)
    def _():
        zero_scr[...] = jnp.zeros_like(zero_scr)
        tail = [pltpu.make_async_copy(zero_scr.at[pl.ds(0, MOE_TILE)],
                                      xs_ref.at[pl.ds(t0, MOE_TILE)], fill_sem)
                for t0 in range(n_rows, xs_ref.shape[0], MOE_TILE)]
        for cp in tail:
            cp.start()
        for cp in tail:
            cp.wait()
        for e in range(N_EXPERTS):
            cp = pltpu.make_async_copy(
                zero_scr, xs_ref.at[pl.ds(pl.multiple_of(fill_ref[e], 8), zero_scr.shape[0])],
                fill_sem)
            cp.start()
            cp.wait()

    def issue(r, carry):
        row = h_ref.at[pl.ds(r, 1)]
        pltpu.make_async_copy(row, xs_ref.at[pl.ds(p1_ref[base + r], 1)], sems.at[0]).start()
        pltpu.make_async_copy(row, xs_ref.at[pl.ds(p2_ref[base + r], 1)], sems.at[1]).start()
        return carry

    lax.fori_loop(0, tm, issue, 0, unroll=8)
    for k in range(TOP_K):
        pltpu.make_async_copy(h_ref, xs_ref.at[pl.ds(0, tm)], sems.at[k]).wait()


def _dispatch(h, p1, p2, fill_start, n_slots):
    n, d = h.shape
    tm = 512
    return pl.pallas_call(
        functools.partial(_dispatch_kernel, n_rows=TOP_K * n),
        grid_spec=pltpu.PrefetchScalarGridSpec(
            num_scalar_prefetch=3,
            grid=(n // tm,),
            in_specs=[pl.BlockSpec((tm, d), lambda i, p1, p2, fs: (i, 0))],
            out_specs=pl.BlockSpec(memory_space=pl.ANY),
            scratch_shapes=[pltpu.VMEM((MOE_TILE + 8, d), F32),
                            pltpu.SemaphoreType.DMA((TOP_K,)),
                            pltpu.SemaphoreType.DMA(())],
        ),
        out_shape=jax.ShapeDtypeStruct((n_slots, d), F32),
        compiler_params=_params(("arbitrary",)),
        name="moe_dispatch",
    )(p1, p2, fill_start, h)


def _moe_ffn_kernel(te_ref, nv_ref, x_ref, wg_hbm, wu_hbm, wd_hbm, y_ref, h_scr,
                    wg_buf, wu_buf, wd_buf, sems, *, tf):
    j = pl.program_id(0)
    n_used = nv_ref[0]
    nf = wd_hbm.shape[1] // tf

    def slices(tile, f, slot):
        e = te_ref[tile]
        c0 = pl.multiple_of(f * tf, tf)
        return (
            pltpu.make_async_copy(wg_hbm.at[e * nf + f], wg_buf.at[slot], sems.at[0, slot]),
            pltpu.make_async_copy(wu_hbm.at[e * nf + f], wu_buf.at[slot], sems.at[1, slot]),
            pltpu.make_async_copy(wd_hbm.at[e, pl.ds(c0, tf), :], wd_buf.at[slot], sems.at[2, slot]),
        )

    y_ref[...] = jnp.zeros_like(y_ref)

    ahead = MOE_WBUF - 1

    @pl.when(j < n_used)
    def _():
        @pl.when(j == 0)
        def _():
            for f0 in range(ahead):
                for cp in slices(0, f0, f0):
                    cp.start()

        h_scr[...] = x_ref[...].astype(BF16)

        def step(f, carry):
            count = j * nf + f
            slot = lax.rem(count, MOE_WBUF)
            for cp in slices(j, f, slot):
                cp.wait()

            wrap = f + ahead >= nf
            nxt_tile = jnp.where(wrap, j + 1, j)
            nxt_f = jnp.where(wrap, f + ahead - nf, f + ahead)

            @pl.when(nxt_tile < n_used)
            def _():
                for cp in slices(nxt_tile, nxt_f, lax.rem(count + ahead, MOE_WBUF)):
                    cp.start()

            y_ref[...] += _swiglu_step(h_scr[...], wg_buf.at[slot], wu_buf.at[slot],
                                       wd_buf.at[slot])
            return carry

        lax.fori_loop(0, nf, step, 0)


def _moe_ffn(xs, tile_expert, n_valid, wg, wu, wd):
    n_slots, d = xs.shape
    tm, tf = MOE_TILE, MOE_TF

    return pl.pallas_call(
        functools.partial(_moe_ffn_kernel, tf=tf),
        grid_spec=pltpu.PrefetchScalarGridSpec(
            num_scalar_prefetch=2,
            grid=(n_slots // tm,),
            in_specs=[
                pl.BlockSpec((tm, d), lambda j, te, nv: (jnp.minimum(j, nv[0] - 1), 0)),
                pl.BlockSpec(memory_space=pl.ANY),
                pl.BlockSpec(memory_space=pl.ANY),
                pl.BlockSpec(memory_space=pl.ANY),
            ],
            out_specs=pl.BlockSpec((tm, d), lambda j, te, nv: (j, 0)),
            scratch_shapes=[
                pltpu.VMEM((tm, d), BF16),
                pltpu.VMEM((MOE_WBUF, d, tf), BF16),
                pltpu.VMEM((MOE_WBUF, d, tf), BF16),
                pltpu.VMEM((MOE_WBUF, tf, d), BF16),
                pltpu.SemaphoreType.DMA((3, MOE_WBUF)),
            ],
        ),
        out_shape=jax.ShapeDtypeStruct((n_slots, d), F32),
        compiler_params=_params(("arbitrary",)),
        name="moe_ffn",
    )(tile_expert, n_valid, xs, wg, wu, wd)


def _combine_kernel(p1_ref, p2_ref, ys_ref, route_ref, x_ref, gpost_ref, gt_ref, o_ref,
                    y1_scr, y2_scr, sems):
    tm = x_ref.shape[0]
    base = pl.program_id(0) * tm

    def issue(r, carry):
        pltpu.make_async_copy(ys_ref.at[pl.ds(p1_ref[base + r], 1)], y1_scr.at[pl.ds(r, 1)],
                              sems.at[0]).start()
        pltpu.make_async_copy(ys_ref.at[pl.ds(p2_ref[base + r], 1)], y2_scr.at[pl.ds(r, 1)],
                              sems.at[1]).start()
        return carry


    lax.fori_loop(0, tm, issue, 0, unroll=8)
    pltpu.make_async_copy(ys_ref.at[pl.ds(0, tm)], y1_scr, sems.at[0]).wait()
    pltpu.make_async_copy(ys_ref.at[pl.ds(0, tm)], y2_scr, sems.at[1]).wait()
    lane = lax.broadcasted_iota(jnp.int32, (1, LANES), 1)
    route = route_ref[...]
    w1 = jnp.sum(jnp.where(lane == RT_W1, route, 0.0), axis=-1, keepdims=True)
    w2 = jnp.sum(jnp.where(lane == RT_W2, route, 0.0), axis=-1, keepdims=True)
    y = w1 * y1_scr[...] + w2 * y2_scr[...]
    o_ref[...] = x_ref[...] + gt_ref[...] * _rms(y, gpost_ref[...])


def _combine(ys, p1, p2, route, x2, gpost, mod, seq):
    n, d = x2.shape
    tm = 512
    per_b = seq // tm
    row = lambda i, p1, p2: (i, 0)
    return pl.pallas_call(
        _combine_kernel,
        grid_spec=pltpu.PrefetchScalarGridSpec(
            num_scalar_prefetch=2,
            grid=(n // tm,),
            in_specs=[
                pl.BlockSpec(memory_space=pl.ANY),
                pl.BlockSpec((tm, LANES), row),
                pl.BlockSpec((tm, d), row),
                pl.BlockSpec((1, d), lambda i, p1, p2: (0, 0)),
                pl.BlockSpec((None, None, 1, d), lambda i, p1, p2: (i // per_b, 5, 0, 0)),
            ],
            out_specs=pl.BlockSpec((tm, d), row),
            scratch_shapes=[pltpu.VMEM((tm, d), F32), pltpu.VMEM((tm, d), F32),
                            pltpu.SemaphoreType.DMA((TOP_K,))],
        ),
        out_shape=jax.ShapeDtypeStruct((n, d), F32),
        compiler_params=_params(("arbitrary",)),
        name="moe_combine",
    )(p1, p2, ys, route, x2, gpost, mod)


def _route_plan(route, counts, n_tiles):
    e1 = route[:, RT_E1].astype(jnp.int32)
    e2 = route[:, RT_E2].astype(jnp.int32)
    cnt = counts[0, :N_EXPERTS].astype(jnp.int32)
    size = (cnt + MOE_TILE - 1) // MOE_TILE * MOE_TILE
    end = jnp.cumsum(size)
    start = end - size
    p1 = start[e1] + route[:, RT_R1].astype(jnp.int32)
    p2 = start[e2] + route[:, RT_R2].astype(jnp.int32)
    n_valid = end[-1] // MOE_TILE
    tile_start = jnp.minimum(jnp.arange(n_tiles), n_valid - 1) * MOE_TILE
    tile_expert = jnp.sum(tile_start[:, None] >= end[None, :], axis=1).astype(jnp.int32)
    fill_start = (start + cnt) // 8 * 8
    return p1, p2, fill_start, tile_expert, n_valid.reshape(1).astype(jnp.int32)


def _pack_kernel(w_ref, o_ref):
    gw = 4 * LANES
    n_gate = 2 * MLSTM_HEADS
    src_gate = 7 * gw
    src_swq = src_gate + n_gate
    w_swq = 2 * SWA_GROUP_HEADS * HEAD_W
    lane = lax.broadcasted_iota(jnp.int32, (1, LANES), 1)
    o_ref[:, PB_SWA_Q * LANES:PB_SWA_Q * LANES + w_swq] = (
        w_ref[:, src_swq:src_swq + w_swq].astype(BF16))
    o_ref[:, PB_MOBA_Q * LANES:PB_ML_G * LANES] = w_ref[:, 0:src_gate].astype(BF16)
    gates = w_ref[:, src_gate:src_gate + LANES]
    o_ref[:, PB_ML_G * LANES:(PB_ML_G + 1) * LANES] = jnp.where(lane < n_gate, gates, 0.0).astype(BF16)
    src_kv = src_swq + w_swq
    o_ref[:, PB_SWA_K * LANES:(PB_SWA_V + 1) * LANES] = (
        w_ref[:, src_kv:src_kv + 2 * LANES].astype(BF16))


def _pack_w_in(w_in, l):
    _, d, n_in = w_in.shape
    tr = 256
    return pl.pallas_call(
        _pack_kernel,
        grid=(d // tr,),
        in_specs=[pl.BlockSpec((None, tr, n_in), lambda i: (l, i, 0))],
        out_specs=pl.BlockSpec((tr, PROJ_BLOCKS * LANES), lambda i: (i, 0)),
        out_shape=jax.ShapeDtypeStruct((d, PROJ_BLOCKS * LANES), BF16),
        compiler_params=_params(("arbitrary",)),
        name="pack_w_in",
    )(w_in)


def kernel(x, c, ada_w, ada_b, g_pre_mix, g_post_mix, g_pre_ffn, g_post_ffn, w_in, w_out, conv_w,
           conv_b, igate_b, fgate_b, mlstm_norm_w, swa_sinks, ffn_w_gate, ffn_w_up, ffn_w_down,
           moe_router, moe_w_gate, moe_w_up, moe_w_down):
    batch, seq, d = x.shape
    depth = ada_w.shape[0]
    n = batch * seq
    x2 = x.reshape(n, d)
    mod_all = _adaln(c, ada_w, ada_b).reshape(depth, batch, 6, 1, d)
    for l in range(depth):
        mod = mod_all[l]
        proj = _inproj(x2, g_pre_mix[l].reshape(1, d), mod, _pack_w_in(w_in, l), seq)
        y_moba = _moba(proj, batch, seq)
        gate_bias = jnp.concatenate(
            [igate_b[l], fgate_b[l], jnp.zeros((LANES - 2 * MLSTM_HEADS,), F32)]).reshape(1, LANES)
        y_mlstm = _mlstm(proj, conv_w[l], conv_b[l].reshape(1, -1), gate_bias,
                         mlstm_norm_w[l].reshape(1, -1), batch, seq)
        y_swa = _swa(proj, swa_sinks[l], batch, seq)
        j = l // 2
        w_router = None
        if l % 2 == 1:
            w_router = jnp.pad(moe_router[j], ((0, 0), (0, LANES - N_EXPERTS))).astype(BF16)
        outs = _outproj(y_moba, y_mlstm, y_swa, w_out[l].astype(BF16), x2,
                        g_post_mix[l].reshape(1, d), g_pre_ffn[l].reshape(1, d), mod, seq, w_router)
        gpost = g_post_ffn[l].reshape(1, d)
        if l % 2 == 0:
            x2, h = outs
            jobs_fn = None
            if l + 1 < depth:
                jn = (l + 1) // 2
                jobs_fn = functools.partial(_expert_cast_jobs, moe_w_gate[jn], moe_w_up[jn],
                                            moe_w_down[jn])
            x2, moe_bf16 = _ffn(h, ffn_w_gate[j].astype(BF16), ffn_w_up[j].astype(BF16),
                                ffn_w_down[j].astype(BF16), x2, gpost, mod, seq, jobs_fn)
        else:
            x2, h, route, counts = outs
            n_tiles = (TOP_K * n) // MOE_TILE + N_EXPERTS + 2
            p1, p2, fill_start, tile_expert, n_valid = _route_plan(route, counts, n_tiles)
            xs = _dispatch(h, p1, p2, fill_start, n_tiles * MOE_TILE)
            wg, wu, wd = moe_bf16
            ys = _moe_ffn(xs, tile_expert, n_valid, wg, wu, wd.reshape(N_EXPERTS, -1, d))
            x2 = _combine(ys, p1, p2, route, x2, gpost, mod, seq)
    return x2.reshape(batch, seq, d)
```

```python
import functools

import jax
import jax.numpy as jnp
from jax import lax
from jax.experimental import pallas as pl
from jax.experimental.pallas import tpu as pltpu

F32 = jnp.float32
BF16 = jnp.bfloat16

LANES = 128
MXU_W = 256
HEAD_W = 64
MOBA_BLOCK = 256
MOBA_TOPK = 3
MOBA_PAIRS = 4
MLSTM_HEADS = 4
MLSTM_CHUNK = 256
MLSTM_CONV = 4
MLSTM_GATE_CAP = 15.0
SWA_WINDOW = 128
SWA_GROUPS = 2
SWA_GROUP_HEADS = 8
N_EXPERTS = 8
RMS_EPS = 1e-6
NEG_INF = float("-inf")
MASK_NEG = -1e30
LOG2_E = 1.4426950408889634
MOE_TILE = 512
MOE_TF = 256
MOE_WBUF = 3
FFN_TM, FFN_TF = 512, 512
TOP_K = 2
RT_E1, RT_E2, RT_W1, RT_W2, RT_R1, RT_R2 = range(6)
VMEM_LIMIT = 56 * 1024 * 1024

PB_SWA_Q = 0
PB_MOBA_Q = 8
PB_MOBA_K = 12
PB_MOBA_V = 16
PB_ML_Q = 20
PB_ML_K = 24
PB_ML_V = 28
PB_ML_O = 32
PB_ML_G = 36
PB_SWA_K = 37
PB_SWA_V = 38
PROJ_BLOCKS = 39


def _params(sem):
    return pltpu.CompilerParams(dimension_semantics=sem, vmem_limit_bytes=VMEM_LIMIT)


def _rms(x, g):
    return x * lax.rsqrt(jnp.mean(x * x, axis=-1, keepdims=True) + RMS_EPS) * g


def _dot(a, b):
    return jnp.dot(a, b, preferred_element_type=F32)


def _dot_nt(a, b):
    return lax.dot_general(a, b, (((1,), (1,)), ((), ())), preferred_element_type=F32)


def _adaln_kernel(c_ref, w_ref, b_ref, o_ref):
    c = c_ref[...]
    cond = (c * jax.nn.sigmoid(c)).astype(BF16)
    o_ref[...] = _dot(cond, w_ref[...].astype(BF16)) + b_ref[...]


def _adaln(c, ada_w, ada_b):
    depth, d, n6 = ada_w.shape
    b = c.shape[0]
    tn = 1024
    return pl.pallas_call(
        _adaln_kernel,
        grid=(depth, n6 // tn),
        in_specs=[
            pl.BlockSpec((b, d), lambda l, j: (0, 0)),
            pl.BlockSpec((None, d, tn), lambda l, j: (l, 0, j)),
            pl.BlockSpec((None, 1, tn), lambda l, j: (l, 0, j)),
        ],
        out_specs=pl.BlockSpec((None, b, tn), lambda l, j: (l, 0, j)),
        out_shape=jax.ShapeDtypeStruct((depth, b, n6), F32),
        compiler_params=_params(("arbitrary", "arbitrary")),
        name="adaln",
    )(c, ada_w, ada_b.reshape(depth, 1, n6))


def _inproj_kernel(x_ref, g_ref, sc_ref, sh_ref, w_ref, o_ref, *, tn):
    h = (_rms(x_ref[...], g_ref[...]) * (1.0 + sc_ref[...]) + sh_ref[...]).astype(BF16)
    nc = w_ref.shape[1]
    for c0 in range(0, nc, tn):
        c1 = min(c0 + tn, nc)
        o_ref[:, c0:c1] = _dot(h, w_ref[:, c0:c1])


def _inproj(x2, g, mod, w, seq):
    n, d = x2.shape
    nc = w.shape[1]
    tm, tn = 256, 7 * MXU_W
    per_b = seq // tm
    return pl.pallas_call(
        functools.partial(_inproj_kernel, tn=tn),
        grid=(n // tm,),
        in_specs=[
            pl.BlockSpec((tm, d), lambda i: (i, 0)),
            pl.BlockSpec((1, d), lambda i: (0, 0)),
            pl.BlockSpec((None, None, 1, d), lambda i: (i // per_b, 1, 0, 0)),
            pl.BlockSpec((None, None, 1, d), lambda i: (i // per_b, 0, 0, 0)),
            pl.BlockSpec((d, nc), lambda i: (0, 0), pipeline_mode=pl.Buffered(1)),
        ],
        out_specs=pl.BlockSpec((tm, nc), lambda i: (i, 0)),
        out_shape=jax.ShapeDtypeStruct((n, nc), F32),
        compiler_params=_params(("arbitrary",)),
        name="inproj",
    )(x2, g, mod, mod, w)


def _moba_kernel(q_ref, k_ref, v_ref, o_ref, ka_scr, vb_scr):
    seq = k_ref.shape[0]
    blk = MOBA_BLOCK
    n_blk = seq // blk
    lane = lax.broadcasted_iota(jnp.int32, (1, LANES), 1)
    k = k_ref[...]
    vb_scr[...] = v_ref[...].astype(BF16)
    kmean = jnp.mean(k.reshape(n_blk, blk, LANES), axis=1)
    kblk = lax.shift_right_logical(lax.broadcasted_iota(jnp.int32, (seq, 1), 0), 8)
    ri = lax.broadcasted_iota(jnp.int32, (blk, blk), 0)
    ci = lax.broadcasted_iota(jnp.int32, (blk, blk), 1)
    causal_bias = jnp.where(ri >= ci, 0.0, MASK_NEG)
    scale = HEAD_W ** -0.5 * LOG2_E
    blk_id = lax.broadcasted_iota(jnp.int32, (n_blk, 1), 0)
    in_head, kmh = [], []
    for hh in range(2):
        base = HEAD_W * (1 - hh)
        in_head.append((lane >= HEAD_W * hh) & (lane < HEAD_W * (hh + 1)))
        ka_scr[hh] = jnp.where(in_head[hh], k, jnp.where(lane - base == kblk, 1.0, 0.0)).astype(BF16)
        kmh.append(jnp.where(in_head[hh], kmean, 0.0).astype(BF16))

    for i in range(n_blk):
        q = q_ref[i * blk:(i + 1) * blk, :]
        outs = []
        for hh in range(2):
            base = HEAD_W * (1 - hh)
            qa = jnp.where(in_head[hh], q * scale, 0.0)
            if i > MOBA_TOPK:
                qm = jnp.where(in_head[hh], q, 0.0).astype(BF16)
                valid = blk_id < i
                gate = jnp.where(valid, _dot_nt(kmh[hh], qm), NEG_INF)
                beaten_by = jnp.zeros((n_blk, blk), jnp.int32)
                for j in range(i):
                    gj = gate[j:j + 1, :]
                    beats = (gj > gate) | ((gj == gate) & (blk_id > j))
                    beaten_by = beaten_by + beats.astype(jnp.int32)
                drop = jnp.where(valid & (beaten_by >= MOBA_TOPK), MASK_NEG, 0.0)
                rows = [drop, jnp.zeros((LANES - base - n_blk, blk), F32)]
                if base:
                    rows.insert(0, jnp.zeros((base, blk), F32))
                qa = qa + jnp.concatenate(rows, axis=0).T
            s = _dot_nt(qa.astype(BF16), ka_scr[hh, 0:(i + 1) * blk, :])
            s_own = s[:, i * blk:] + causal_bias
            m = jnp.max(s_own, axis=-1, keepdims=True)
            if i:
                s_past = s[:, :i * blk]
                m = jnp.maximum(m, jnp.max(s_past, axis=-1, keepdims=True))
            p_own = jnp.exp2(s_own - m)
            l = jnp.sum(p_own, axis=-1, keepdims=True)
            acc = _dot(p_own.astype(BF16), vb_scr[i * blk:(i + 1) * blk, :])
            if i:
                p_past = jnp.exp2(s_past - m)
                l = l + jnp.sum(p_past, axis=-1, keepdims=True)
                acc = acc + _dot(p_past.astype(BF16), vb_scr[0:i * blk, :])
            outs.append(acc / l)
        o_ref[i * blk:(i + 1) * blk, :] = jnp.where(lane < HEAD_W, outs[0], outs[1]).astype(o_ref.dtype)


def _moba(proj, batch, seq):
    n = proj.shape[0]
    return pl.pallas_call(
        _moba_kernel,
        grid=(batch, MOBA_PAIRS),
        in_specs=[
            pl.BlockSpec((seq, LANES), lambda b, p: (b, PB_MOBA_Q + p)),
            pl.BlockSpec((seq, LANES), lambda b, p: (b, PB_MOBA_K + p)),
            pl.BlockSpec((seq, LANES), lambda b, p: (b, PB_MOBA_V + p)),
        ],
        out_specs=pl.BlockSpec((seq, LANES), lambda b, p: (b, p)),
        out_shape=jax.ShapeDtypeStruct((n, MOBA_PAIRS * LANES), BF16),
        scratch_shapes=[
            pltpu.VMEM((2, seq, LANES), BF16),
            pltpu.VMEM((seq, LANES), BF16),
        ],
        compiler_params=_params(("arbitrary", "arbitrary")),
        name="moba",
    )(proj, proj, proj)


def _causal_conv_silu(x, tail, w, b):
    row = lax.broadcasted_iota(jnp.int32, (8, 1), 0)
    y = b + w[MLSTM_CONV - 1:MLSTM_CONV, :] * x
    for shift in range(1, MLSTM_CONV):
        xr = pltpu.roll(x, shift, axis=0)
        head = jnp.where(row < shift, pltpu.roll(tail, shift, axis=0), xr[0:8, :])
        xs = jnp.concatenate([head, xr[8:, :]], axis=0)
        y = y + w[MLSTM_CONV - 1 - shift:MLSTM_CONV - shift, :] * xs
    return y * jax.nn.sigmoid(y)


def _mlstm_kernel(q_ref, k_ref, v_ref, og_ref, gate_ref, cwq_ref, cwk_ref, cbq_ref, cbk_ref,
                  gb_ref, nw_ref, o_ref, qtail_scr, ktail_scr, c_scr, n_scr, m_scr):
    ts = L = q_ref.shape[0]
    lane = lax.broadcasted_iota(jnp.int32, (1, LANES), 1)
    ri = lax.broadcasted_iota(jnp.int32, (L, L), 0)
    ci = lax.broadcasted_iota(jnp.int32, (L, L), 1)
    causal = ri >= ci
    lower = causal.astype(F32)
    upper = (ri <= ci).astype(F32)

    @pl.when(pl.program_id(1) == 0)
    def _():
        qtail_scr[...] = jnp.zeros_like(qtail_scr)
        ktail_scr[...] = jnp.zeros_like(ktail_scr)
        c_scr[...] = jnp.zeros_like(c_scr)
        n_scr[...] = jnp.zeros_like(n_scr)
        m_scr[...] = jnp.zeros_like(m_scr)

    t = MLSTM_GATE_CAP * jnp.tanh((gate_ref[...] + gb_ref[...]) / MLSTM_GATE_CAP)
    a_col = jnp.where(lane < MLSTM_HEADS, t, jax.nn.log_sigmoid(t))
    a_row = a_col.T[0:8, :]

    xq, xk = q_ref[...], k_ref[...]
    qc = _causal_conv_silu(xq, qtail_scr[...], cwq_ref[...], cbq_ref[...]).astype(BF16)
    kc = (_causal_conv_silu(xk, ktail_scr[...], cwk_ref[...], cbk_ref[...])
          * (LANES ** -0.5)).astype(BF16)
    qtail_scr[...] = xq[ts - 8:, :]
    ktail_scr[...] = xk[ts - 8:, :]

    b_c = jnp.dot(lower, a_col, precision=lax.Precision.HIGHEST, preferred_element_type=F32)
    b_r = jnp.dot(a_row, upper, precision=lax.Precision.HIGHEST, preferred_element_type=F32)

    local = []
    for hd in range(MLSTM_HEADS):
        cols = slice(hd * LANES, (hd + 1) * LANES)
        q, k = qc[:, cols], kc[:, cols]
        v = v_ref[:, cols].astype(BF16)
        fl = hd + MLSTM_HEADS
        b_col, li_col = b_c[:, fl:fl + 1], a_col[:, hd:hd + 1]
        b_row, li_row = b_r[fl:fl + 1, :], a_row[hd:hd + 1, :]
        b_last = b_row[:, L - 1:L]
        d_log = jnp.where(causal, b_col - b_row + li_row, NEG_INF)
        d_max = jnp.max(d_log, axis=1, keepdims=True)
        qk = _dot_nt(q, k) * jnp.exp(d_log - d_max)
        pv = _dot(qk.astype(BF16), v)
        qk_sum = jnp.sum(qk, axis=1, keepdims=True)
        a_max = jnp.max(b_last - b_row + li_row, axis=1, keepdims=True)
        kw = k.astype(F32) * jnp.exp(b_last - b_col + li_col - a_max)
        c_in = _dot(kw.T.astype(BF16), v)
        n_in = jnp.sum(kw, axis=0, keepdims=True)
        local.append((q, b_col, b_last, d_max, pv, qk_sum, a_max, c_in, n_in))

    for hd in range(MLSTM_HEADS):
        cols = slice(hd * LANES, (hd + 1) * LANES)
        q, b_col, b_last, d_max, pv, qk_sum, a_max, c_in, n_in = local[hd]
        c_st, n_st, m_st = c_scr[hd], n_scr[hd], m_scr[hd][:, 0:1]
        inter_log = b_col + m_st
        m_out = jnp.maximum(inter_log, d_max)
        w_inter = jnp.exp(inter_log - m_out)
        w_local = jnp.exp(d_max - m_out)
        num = w_local * pv + w_inter * _dot(q, c_st.astype(BF16))
        den = w_local * qk_sum + w_inter * jnp.sum(q.astype(F32) * n_st, axis=1, keepdims=True)
        hv = num / jnp.maximum(jnp.abs(den), jnp.exp(-m_out))
        hn = _rms(hv, nw_ref[:, cols])
        o_ref[:, cols] = (jax.nn.sigmoid(og_ref[:, cols]) * hn).astype(o_ref.dtype)
        m_new = jnp.maximum(b_last + m_st, a_max)
        decay = jnp.exp(b_last + m_st - m_new)
        inject = jnp.exp(a_max - m_new)
        c_scr[hd] = decay * c_st + inject * c_in
        n_scr[hd] = decay * n_st + inject * n_in
        m_scr[hd] = jnp.broadcast_to(m_new, (1, LANES))


def _mlstm(proj, conv_w, conv_b, gate_bias, norm_w, batch, seq):
    n = proj.shape[0]
    hw = MLSTM_HEADS
    gw = hw * LANES
    ts = MLSTM_CHUNK
    tiles = seq // ts

    def col(base):
        return pl.BlockSpec((ts, gw), lambda b, t: (b * tiles + t, base // hw))

    def vec(rows, blk):
        return pl.BlockSpec((rows, gw), lambda b, t: (0, blk))

    return pl.pallas_call(
        _mlstm_kernel,
        grid=(batch, tiles),
        in_specs=[
            col(PB_ML_Q), col(PB_ML_K), col(PB_ML_V), col(PB_ML_O),
            pl.BlockSpec((ts, LANES), lambda b, t: (b * tiles + t, PB_ML_G)),
            vec(MLSTM_CONV, 0), vec(MLSTM_CONV, 1), vec(1, 0), vec(1, 1),
            pl.BlockSpec((1, LANES), lambda b, t: (0, 0)),
            vec(1, 0),
        ],
        out_specs=pl.BlockSpec((ts, gw), lambda b, t: (b * tiles + t, 0)),
        out_shape=jax.ShapeDtypeStruct((n, gw), BF16),
        scratch_shapes=[
            pltpu.VMEM((8, gw), F32),
            pltpu.VMEM((8, gw), F32),
            pltpu.VMEM((hw, LANES, LANES), F32),
            pltpu.VMEM((hw, 1, LANES), F32),
            pltpu.VMEM((hw, 1, LANES), F32),
        ],
        compiler_params=_params(("arbitrary", "arbitrary")),
        name="mlstm",
    )(proj, proj, proj, proj, proj, conv_w, conv_w, conv_b, conv_b, gate_bias, norm_w)


def _swa_kernel(sink_ref, q_ref, k_ref, v_ref, o_ref, kb_scr, vb_scr):
    g = pl.program_id(1)
    seq = q_ref.shape[0]
    W = SWA_WINDOW
    n_pairs = SWA_GROUP_HEADS // 2
    lane = lax.broadcasted_iota(jnp.int32, (1, LANES), 1)
    low = lane < HEAD_W
    keep = low == (g == 0)
    for src, dst in ((k_ref, kb_scr), (v_ref, vb_scr)):
        both = src[...]
        dst[...] = jnp.where(keep, both, pltpu.roll(both, HEAD_W, axis=1)).astype(BF16)
    scale = HEAD_W ** -0.5 * LOG2_E
    sinks = [sink_ref[g * SWA_GROUP_HEADS + hd] * LOG2_E for hd in range(SWA_GROUP_HEADS)]
    r = lax.broadcasted_iota(jnp.int32, (W, 2 * W), 0)
    c = lax.broadcasted_iota(jnp.int32, (W, 2 * W), 1)
    bias_first = jnp.where(c <= r, 0.0, MASK_NEG)
    bias_rest = jnp.where((c > r) & (c <= r + W), 0.0, MASK_NEG)

    def block(nb, carry):
        start = pl.multiple_of(jnp.maximum(nb - 1, 0) * W, W)
        r0 = pl.multiple_of(nb * W, W)
        qn = q_ref[pl.ds(r0, W), :] * scale
        kband = kb_scr[pl.ds(start, 2 * W), :]
        vband = vb_scr[pl.ds(start, 2 * W), :]
        parts = []
        for p in range(n_pairs):
            qp = qn[:, p * LANES:(p + 1) * LANES]
            parts.append(jnp.where(low, qp, 0.0).astype(BF16))
            parts.append(jnp.where(low, 0.0, qp).astype(BF16))
        s_all = _dot_nt(jnp.concatenate(parts, axis=0), kband)
        bias = jnp.where(nb == 0, bias_first, bias_rest)
        probs, inv_l = [], []
        for hd in range(SWA_GROUP_HEADS):
            s = s_all[hd * W:(hd + 1) * W, :] + bias
            m = jnp.maximum(jnp.max(s, axis=-1, keepdims=True), sinks[hd])
            e = jnp.exp2(s - m)
            inv_l.append(1.0 / (jnp.sum(e, axis=-1, keepdims=True) + jnp.exp2(sinks[hd] - m)))
            probs.append(e.astype(BF16))
        o_all = _dot(jnp.concatenate(probs, axis=0), vband)
        outs = []
        for p in range(n_pairs):
            lo = o_all[(2 * p) * W:(2 * p + 1) * W, :] * inv_l[2 * p]
            hi = o_all[(2 * p + 1) * W:(2 * p + 2) * W, :] * inv_l[2 * p + 1]
            outs.append(jnp.where(low, lo, hi))
        o_ref[pl.ds(r0, W), :] = jnp.concatenate(outs, axis=1).astype(o_ref.dtype)
        return carry

    lax.fori_loop(0, seq // W, block, 0)


def _swa(proj, sinks, batch, seq):
    n = proj.shape[0]
    gw = SWA_GROUP_HEADS * HEAD_W
    gb = gw // LANES
    return pl.pallas_call(
        _swa_kernel,
        grid=(batch, SWA_GROUPS),
        in_specs=[
            pl.BlockSpec(memory_space=pltpu.SMEM),
            pl.BlockSpec((seq, gw), lambda b, g: (b, PB_SWA_Q // gb + g)),
            pl.BlockSpec((seq, LANES), lambda b, g: (b, PB_SWA_K)),
            pl.BlockSpec((seq, LANES), lambda b, g: (b, PB_SWA_V)),
        ],
        out_specs=pl.BlockSpec((seq, gw), lambda b, g: (b, g)),
        out_shape=jax.ShapeDtypeStruct((n, SWA_GROUPS * gw), BF16),
        scratch_shapes=[pltpu.VMEM((seq, LANES), BF16), pltpu.VMEM((seq, LANES), BF16)],
        compiler_params=_params(("arbitrary", "arbitrary")),
        name="swa",
    )(sinks, proj, proj, proj)


def _outproj_kernel(*refs, with_router, sub):
    if with_router:
        (ya_ref, yb_ref, yc_ref, w_ref, x_ref, gpost_ref, gt_ref, gpre_ref, sc_ref, sh_ref,
         wr_ref, xo_ref, h_ref, route_ref, count_ref, count_scr) = refs
    else:
        (ya_ref, yb_ref, yc_ref, w_ref, x_ref, gpost_ref, gt_ref, gpre_ref, sc_ref, sh_ref,
         xo_ref, h_ref) = refs
    wa = ya_ref.shape[1]
    wb = yb_ref.shape[1]
    if with_router:
        @pl.when(pl.program_id(0) == 0)
        def _():
            count_scr[...] = jnp.zeros_like(count_scr)

    for r0 in range(0, x_ref.shape[0], sub):
        rows = slice(r0, r0 + sub)
        y = _dot(ya_ref[rows, :], w_ref[0:wa, :])
        y = y + _dot(yb_ref[rows, :], w_ref[wa:wa + wb, :])
        y = y + _dot(yc_ref[rows, :], w_ref[wa + wb:, :])
        xn = x_ref[rows, :] + gt_ref[...] * _rms(y, gpost_ref[...])
        xo_ref[rows, :] = xn
        hb = (_rms(xn, gpre_ref[...]) * (1.0 + sc_ref[...]) + sh_ref[...]).astype(BF16)
        h_ref[rows, :] = hb.astype(h_ref.dtype)
        if not with_router:
            continue
        tm = sub
        lane = lax.broadcasted_iota(jnp.int32, (1, LANES), 1)
        logits = jnp.where(lane < N_EXPERTS, _dot(hb, wr_ref[...]), NEG_INF)
        m1 = jnp.max(logits, axis=-1, keepdims=True)
        i1 = jnp.min(jnp.where(logits == m1, lane, LANES), axis=-1, keepdims=True)
        rest = jnp.where(lane == i1, NEG_INF, logits)
        m2 = jnp.max(rest, axis=-1, keepdims=True)
        i2 = jnp.min(jnp.where(rest == m2, lane, LANES), axis=-1, keepdims=True)
        e2 = jnp.exp(m2 - m1)
        w1 = 1.0 / (1.0 + e2)
        picked = jnp.where((lane == i1) | (lane == i2), 1.0, 0.0)
        ri = lax.broadcasted_iota(jnp.int32, (tm, tm), 0)
        ci = lax.broadcasted_iota(jnp.int32, (tm, tm), 1)
        before = jnp.where(ri > ci, 1.0, 0.0).astype(BF16)
        rank = _dot(before, picked.astype(BF16)) + count_scr[...]
        r1 = jnp.sum(jnp.where(lane == i1, rank, 0.0), axis=-1, keepdims=True)
        r2 = jnp.sum(jnp.where(lane == i2, rank, 0.0), axis=-1, keepdims=True)
        count_scr[...] += jnp.sum(picked, axis=0, keepdims=True)
        rec = jnp.where(lane == RT_E1, i1.astype(F32), 0.0)
        for slot_lane, val in ((RT_E2, i2.astype(F32)), (RT_W1, w1), (RT_W2, e2 * w1),
                               (RT_R1, r1), (RT_R2, r2)):
            rec = jnp.where(lane == slot_lane, val, rec)
        route_ref[rows, :] = rec
    if with_router:
        count_ref[...] = jnp.broadcast_to(count_scr[...], count_ref.shape)


def _outproj(ya, yb, yc, w, x2, gpost, gpre, mod, seq, w_router=None):
    n, d = x2.shape
    tm, sub = 512, 256
    per_b = seq // tm
    with_router = w_router is not None

    def rows(width):
        return pl.BlockSpec((tm, width), lambda i: (i, 0))

    def vec():
        return pl.BlockSpec((1, d), lambda i: (0, 0))

    def modrow(k):
        return pl.BlockSpec((None, None, 1, d), lambda i: (i // per_b, k, 0, 0))

    in_specs = [rows(ya.shape[1]), rows(yb.shape[1]), rows(yc.shape[1]),
                pl.BlockSpec((d, d), lambda i: (0, 0), pipeline_mode=pl.Buffered(1)), rows(d),
                vec(), modrow(2), vec(), modrow(4), modrow(3)]
    args = [ya, yb, yc, w, x2, gpost, mod, gpre, mod, mod]
    out_specs = [rows(d), rows(d)]
    out_shape = [jax.ShapeDtypeStruct((n, d), F32),
                 jax.ShapeDtypeStruct((n, d), F32 if with_router else BF16)]
    scratch = []
    if with_router:
        in_specs.append(pl.BlockSpec((d, LANES), lambda i: (0, 0)))
        args.append(w_router)
        out_specs += [rows(LANES), pl.BlockSpec((8, LANES), lambda i: (0, 0))]
        out_shape += [jax.ShapeDtypeStruct((n, LANES), F32), jax.ShapeDtypeStruct((8, LANES), F32)]
        scratch.append(pltpu.VMEM((1, LANES), F32))
    return pl.pallas_call(
        functools.partial(_outproj_kernel, with_router=with_router, sub=sub),
        grid=(n // tm,),
        in_specs=in_specs,
        out_specs=out_specs,
        out_shape=out_shape,
        scratch_shapes=scratch,
        compiler_params=_params(("arbitrary",)),
        name="outproj",
    )(*args)


def _swiglu_step(h, wg_ref, wu_ref, wd_ref):
    gate = _dot(h, wg_ref[...])
    a = gate * jax.nn.sigmoid(gate) * _dot(h, wu_ref[...])
    return _dot(a.astype(BF16), wd_ref[...])


def _ffn_kernel(*refs, n_cast):
    h_ref, wg_ref, wu_ref, wd_ref, x_ref, gpost_ref, gt_ref = refs[:7]
    cast_in = refs[7:7 + n_cast]
    o_ref = refs[7 + n_cast]
    cast_out = refs[8 + n_cast:]
    f = pl.program_id(1)

    @pl.when(f == 0)
    def _():
        o_ref[...] = jnp.zeros_like(o_ref)

    o_ref[...] += _swiglu_step(h_ref[...], wg_ref, wu_ref, wd_ref)

    @pl.when(f == pl.num_programs(1) - 1)
    def _():
        o_ref[...] = x_ref[...] + gt_ref[...] * _rms(o_ref[...], gpost_ref[...])

    for src, dst in zip(cast_in, cast_out):
        dst[...] = src[...].astype(BF16)


def _expert_cast_jobs(w_gate, w_up, w_down, gi, gf):
    n_e, d, dff = w_gate.shape
    rows = n_e * d // gi
    per_e = d // rows
    assert rows * gi == n_e * d and per_e * rows == d and rows % 16 == 0
    assert dff == gf * MOE_TF
    drows = n_e * dff // (gi * gf)
    assert drows * gi * gf == n_e * dff and drows % 16 == 0
    gu_in = pl.BlockSpec((rows, MOE_TF), lambda i, f: (i, f))
    gu_out = pl.BlockSpec((None, rows, MOE_TF), lambda i, f: (i // per_e * gf + f, i % per_e, 0))
    gu_shape = jax.ShapeDtypeStruct((n_e * gf, d, MOE_TF), BF16)
    dn_spec = pl.BlockSpec((drows, d), lambda i, f: (i * gf + f, 0))
    dn_shape = jax.ShapeDtypeStruct((n_e * dff, d), BF16)
    return [(w_gate.reshape(n_e * d, dff), gu_in, gu_shape, gu_out),
            (w_up.reshape(n_e * d, dff), gu_in, gu_shape, gu_out),
            (w_down.reshape(n_e * dff, d), dn_spec, dn_shape, dn_spec)]


def _ffn(h, wg, wu, wd, x2, gpost, mod, seq, cast_jobs_fn=None):
    n, d = x2.shape
    tm, tf = FFN_TM, FFN_TF
    per_b = seq // tm
    gi, gf = n // tm, wg.shape[1] // tf
    row = lambda i, f: (i, 0)
    jobs = cast_jobs_fn(gi, gf) if cast_jobs_fn else []
    outs = pl.pallas_call(
        functools.partial(_ffn_kernel, n_cast=len(jobs)),
        grid=(gi, gf),
        in_specs=[
            pl.BlockSpec((tm, d), row),
            pl.BlockSpec((d, tf), lambda i, f: (0, f)),
            pl.BlockSpec((d, tf), lambda i, f: (0, f)),
            pl.BlockSpec((tf, d), lambda i, f: (f, 0)),
            pl.BlockSpec((tm, d), row),
            pl.BlockSpec((1, d), lambda i, f: (0, 0)),
            pl.BlockSpec((None, None, 1, d), lambda i, f: (i // per_b, 5, 0, 0)),
        ] + [job[1] for job in jobs],
        out_specs=[pl.BlockSpec((tm, d), row)] + [job[3] for job in jobs],
        out_shape=[jax.ShapeDtypeStruct((n, d), F32)] + [job[2] for job in jobs],
        compiler_params=_params(("arbitrary", "arbitrary")),
        name="ffn",
    )(h, wg, wu, wd, x2, gpost, mod, *[job[0] for job in jobs])
    return outs[0], tuple(outs[1:])


def _dispatch_kernel(p1_ref, p2_ref, fill_ref, h_ref, xs_ref, zero_scr, sems, fill_sem, *, n_rows):
    tm = h_ref.shape[0]
    base = pl.program_id(0) * tm

    @pl.when(pl.program_id(0) == 0)
    def _():
        zero_scr[...] = jnp.zeros_like(zero_scr)
        tail = [pltpu.make_async_copy(zero_scr.at[pl.ds(0, MOE_TILE)],
                                      xs_ref.at[pl.ds(t0, MOE_TILE)], fill_sem)
                for t0 in range(n_rows, xs_ref.shape[0], MOE_TILE)]
        for cp in tail:
            cp.start()
        for cp in tail:
            cp.wait()
        for e in range(N_EXPERTS):
            cp = pltpu.make_async_copy(
                zero_scr, xs_ref.at[pl.ds(pl.multiple_of(fill_ref[e], 8), zero_scr.shape[0])],
                fill_sem)
            cp.start()
            cp.wait()

    def issue(r, carry):
        row = h_ref.at[pl.ds(r, 1)]
        pltpu.make_async_copy(row, xs_ref.at[pl.ds(p1_ref[base + r], 1)], sems.at[0]).start()
        pltpu.make_async_copy(row, xs_ref.at[pl.ds(p2_ref[base + r], 1)],
                              sems.at[1]).start(priority=1)
        return carry

    lax.fori_loop(0, tm, issue, 0, unroll=8)
    for k in range(TOP_K):
        pltpu.make_async_copy(h_ref, xs_ref.at[pl.ds(0, tm)], sems.at[k]).wait()


def _dispatch(h, p1, p2, fill_start, n_slots):
    n, d = h.shape
    tm = 512
    return pl.pallas_call(
        functools.partial(_dispatch_kernel, n_rows=TOP_K * n),
        grid_spec=pltpu.PrefetchScalarGridSpec(
            num_scalar_prefetch=3,
            grid=(n // tm,),
            in_specs=[pl.BlockSpec((tm, d), lambda i, p1, p2, fs: (i, 0))],
            out_specs=pl.BlockSpec(memory_space=pl.ANY),
            scratch_shapes=[pltpu.VMEM((MOE_TILE + 8, d), F32),
                            pltpu.SemaphoreType.DMA((TOP_K,)),
                            pltpu.SemaphoreType.DMA(())],
        ),
        out_shape=jax.ShapeDtypeStruct((n_slots, d), F32),
        compiler_params=_params(("arbitrary",)),
        name="moe_dispatch",
    )(p1, p2, fill_start, h)


def _moe_ffn_kernel(te_ref, nv_ref, x_ref, wg_hbm, wu_hbm, wd_hbm, y_ref, h_scr,
                    wg_buf, wu_buf, wd_buf, sems, *, tf):
    j = pl.program_id(0)
    n_used = nv_ref[0]
    nf = wd_hbm.shape[1] // tf

    def slices(tile, f, slot):
        e = te_ref[tile]
        c0 = pl.multiple_of(f * tf, tf)
        return (
            pltpu.make_async_copy(wg_hbm.at[e * nf + f], wg_buf.at[slot], sems.at[0, slot]),
            pltpu.make_async_copy(wu_hbm.at[e * nf + f], wu_buf.at[slot], sems.at[1, slot]),
            pltpu.make_async_copy(wd_hbm.at[e, pl.ds(c0, tf), :], wd_buf.at[slot], sems.at[2, slot]),
        )

    y_ref[...] = jnp.zeros_like(y_ref)

    ahead = MOE_WBUF - 1

    @pl.when(j < n_used)
    def _():
        @pl.when(j == 0)
        def _():
            for f0 in range(ahead):
                for cp in slices(0, f0, f0):
                    cp.start()

        h_scr[...] = x_ref[...].astype(BF16)

        def step(f, carry):
            count = j * nf + f
            slot = lax.rem(count, MOE_WBUF)
            for cp in slices(j, f, slot):
                cp.wait()

            wrap = f + ahead >= nf
            nxt_tile = jnp.where(wrap, j + 1, j)
            nxt_f = jnp.where(wrap, f + ahead - nf, f + ahead)

            @pl.when(nxt_tile < n_used)
            def _():
                for cp in slices(nxt_tile, nxt_f, lax.rem(count + ahead, MOE_WBUF)):
                    cp.start()

            y_ref[...] += _swiglu_step(h_scr[...], wg_buf.at[slot], wu_buf.at[slot],
                                       wd_buf.at[slot])
            return carry

        lax.fori_loop(0, nf, step, 0)


def _moe_ffn(xs, tile_expert, n_valid, wg, wu, wd):
    n_slots, d = xs.shape
    tm, tf = MOE_TILE, MOE_TF

    return pl.pallas_call(
        functools.partial(_moe_ffn_kernel, tf=tf),
        grid_spec=pltpu.PrefetchScalarGridSpec(
            num_scalar_prefetch=2,
            grid=(n_slots // tm,),
            in_specs=[
                pl.BlockSpec((tm, d), lambda j, te, nv: (jnp.minimum(j, nv[0] - 1), 0)),
                pl.BlockSpec(memory_space=pl.ANY),
                pl.BlockSpec(memory_space=pl.ANY),
                pl.BlockSpec(memory_space=pl.ANY),
            ],
            out_specs=pl.BlockSpec((tm, d), lambda j, te, nv: (j, 0)),
            scratch_shapes=[
                pltpu.VMEM((tm, d), BF16),
                pltpu.VMEM((MOE_WBUF, d, tf), BF16),
                pltpu.VMEM((MOE_WBUF, d, tf), BF16),
                pltpu.VMEM((MOE_WBUF, tf, d), BF16),
                pltpu.SemaphoreType.DMA((3, MOE_WBUF)),
            ],
        ),
        out_shape=jax.ShapeDtypeStruct((n_slots, d), F32),
        compiler_params=_params(("arbitrary",)),
        name="moe_ffn",
    )(tile_expert, n_valid, xs, wg, wu, wd)


def _combine_kernel(p1_ref, p2_ref, ys_ref, route_ref, x_ref, gpost_ref, gt_ref, o_ref,
                    y1_scr, y2_scr, sems):
    tm = x_ref.shape[0]
    base = pl.program_id(0) * tm

    def issue(r, carry):
        pltpu.make_async_copy(ys_ref.at[pl.ds(p1_ref[base + r], 1)], y1_scr.at[pl.ds(r, 1)],
                              sems.at[0]).start()
        pltpu.make_async_copy(ys_ref.at[pl.ds(p2_ref[base + r], 1)], y2_scr.at[pl.ds(r, 1)],
                              sems.at[1]).start(priority=1)
        return carry


    lax.fori_loop(0, tm, issue, 0, unroll=8)
    pltpu.make_async_copy(ys_ref.at[pl.ds(0, tm)], y1_scr, sems.at[0]).wait()
    pltpu.make_async_copy(ys_ref.at[pl.ds(0, tm)], y2_scr, sems.at[1]).wait()
    lane = lax.broadcasted_iota(jnp.int32, (1, LANES), 1)
    route = route_ref[...]
    w1 = jnp.sum(jnp.where(lane == RT_W1, route, 0.0), axis=-1, keepdims=True)
    w2 = jnp.sum(jnp.where(lane == RT_W2, route, 0.0), axis=-1, keepdims=True)
    y = w1 * y1_scr[...] + w2 * y2_scr[...]
    o_ref[...] = x_ref[...] + gt_ref[...] * _rms(y, gpost_ref[...])


def _combine(ys, p1, p2, route, x2, gpost, mod, seq):
    n, d = x2.shape
    tm = 512
    per_b = seq // tm
    row = lambda i, p1, p2: (i, 0)
    return pl.pallas_call(
        _combine_kernel,
        grid_spec=pltpu.PrefetchScalarGridSpec(
            num_scalar_prefetch=2,
            grid=(n // tm,),
            in_specs=[
                pl.BlockSpec(memory_space=pl.ANY),
                pl.BlockSpec((tm, LANES), row),
                pl.BlockSpec((tm, d), row),
                pl.BlockSpec((1, d), lambda i, p1, p2: (0, 0)),
                pl.BlockSpec((None, None, 1, d), lambda i, p1, p2: (i // per_b, 5, 0, 0)),
            ],
            out_specs=pl.BlockSpec((tm, d), row),
            scratch_shapes=[pltpu.VMEM((tm, d), F32), pltpu.VMEM((tm, d), F32),
                            pltpu.SemaphoreType.DMA((TOP_K,))],
        ),
        out_shape=jax.ShapeDtypeStruct((n, d), F32),
        compiler_params=_params(("arbitrary",)),
        name="moe_combine",
    )(p1, p2, ys, route, x2, gpost, mod)


def _route_plan(route, counts, n_tiles):
    e1 = route[:, RT_E1].astype(jnp.int32)
    e2 = route[:, RT_E2].astype(jnp.int32)
    cnt = counts[0, :N_EXPERTS].astype(jnp.int32)
    size = (cnt + MOE_TILE - 1) // MOE_TILE * MOE_TILE
    end = jnp.cumsum(size)
    start = end - size
    p1 = start[e1] + route[:, RT_R1].astype(jnp.int32)
    p2 = start[e2] + route[:, RT_R2].astype(jnp.int32)
    n_valid = end[-1] // MOE_TILE
    tile_start = jnp.minimum(jnp.arange(n_tiles), n_valid - 1) * MOE_TILE
    tile_expert = jnp.sum(tile_start[:, None] >= end[None, :], axis=1).astype(jnp.int32)
    fill_start = (start + cnt) // 8 * 8
    return p1, p2, fill_start, tile_expert, n_valid.reshape(1).astype(jnp.int32)


def _pack_kernel(w_ref, o_ref):
    gw = 4 * LANES
    n_gate = 2 * MLSTM_HEADS
    src_gate = 7 * gw
    src_swq = src_gate + n_gate
    w_swq = 2 * SWA_GROUP_HEADS * HEAD_W
    lane = lax.broadcasted_iota(jnp.int32, (1, LANES), 1)
    o_ref[:, PB_SWA_Q * LANES:PB_SWA_Q * LANES + w_swq] = (
        w_ref[:, src_swq:src_swq + w_swq].astype(BF16))
    o_ref[:, PB_MOBA_Q * LANES:PB_ML_G * LANES] = w_ref[:, 0:src_gate].astype(BF16)
    gates = w_ref[:, src_gate:src_gate + LANES]
    o_ref[:, PB_ML_G * LANES:(PB_ML_G + 1) * LANES] = jnp.where(lane < n_gate, gates, 0.0).astype(BF16)
    src_kv = src_swq + w_swq
    o_ref[:, PB_SWA_K * LANES:(PB_SWA_V + 1) * LANES] = (
        w_ref[:, src_kv:src_kv + 2 * LANES].astype(BF16))


def _pack_w_in(w_in, l):
    _, d, n_in = w_in.shape
    tr = 256
    return pl.pallas_call(
        _pack_kernel,
        grid=(d // tr,),
        in_specs=[pl.BlockSpec((None, tr, n_in), lambda i: (l, i, 0))],
        out_specs=pl.BlockSpec((tr, PROJ_BLOCKS * LANES), lambda i: (i, 0)),
        out_shape=jax.ShapeDtypeStruct((d, PROJ_BLOCKS * LANES), BF16),
        compiler_params=_params(("arbitrary",)),
        name="pack_w_in",
    )(w_in)


def kernel(x, c, ada_w, ada_b, g_pre_mix, g_post_mix, g_pre_ffn, g_post_ffn, w_in, w_out, conv_w,
           conv_b, igate_b, fgate_b, mlstm_norm_w, swa_sinks, ffn_w_gate, ffn_w_up, ffn_w_down,
           moe_router, moe_w_gate, moe_w_up, moe_w_down):
    batch, seq, d = x.shape
    depth = ada_w.shape[0]
    n = batch * seq
    x2 = x.reshape(n, d)
    mod_all = _adaln(c, ada_w, ada_b).reshape(depth, batch, 6, 1, d)
    for l in range(depth):
        mod = mod_all[l]
        proj = _inproj(x2, g_pre_mix[l].reshape(1, d), mod, _pack_w_in(w_in, l), seq)
        y_moba = _moba(proj, batch, seq)
        gate_bias = jnp.concatenate(
            [igate_b[l], fgate_b[l], jnp.zeros((LANES - 2 * MLSTM_HEADS,), F32)]).reshape(1, LANES)
        y_mlstm = _mlstm(proj, conv_w[l], conv_b[l].reshape(1, -1), gate_bias,
                         mlstm_norm_w[l].reshape(1, -1), batch, seq)
        y_swa = _swa(proj, swa_sinks[l], batch, seq)
        j = l // 2
        w_router = None
        if l % 2 == 1:
            w_router = jnp.pad(moe_router[j], ((0, 0), (0, LANES - N_EXPERTS))).astype(BF16)
        outs = _outproj(y_moba, y_mlstm, y_swa, w_out[l].astype(BF16), x2,
                        g_post_mix[l].reshape(1, d), g_pre_ffn[l].reshape(1, d), mod, seq, w_router)
        gpost = g_post_ffn[l].reshape(1, d)
        if l % 2 == 0:
            x2, h = outs
            jobs_fn = None
            if l + 1 < depth:
                jn = (l + 1) // 2
                jobs_fn = functools.partial(_expert_cast_jobs, moe_w_gate[jn], moe_w_up[jn],
                                            moe_w_down[jn])
            x2, moe_bf16 = _ffn(h, ffn_w_gate[j].astype(BF16), ffn_w_up[j].astype(BF16),
                                ffn_w_down[j].astype(BF16), x2, gpost, mod, seq, jobs_fn)
        else:
            x2, h, route, counts = outs
            n_tiles = (TOP_K * n) // MOE_TILE + N_EXPERTS + 2
            p1, p2, fill_start, tile_expert, n_valid = _route_plan(route, counts, n_tiles)
            xs = _dispatch(h, p1, p2, fill_start, n_tiles * MOE_TILE)
            wg, wu, wd = moe_bf16
            ys = _moe_ffn(xs, tile_expert, n_valid, wg, wu, wd.reshape(N_EXPERTS, -1, d))
            x2 = _combine(ys, p1, p2, route, x2, gpost, mod, seq)
    return x2.reshape(batch, seq, d)
```

```python
import functools

import jax
import jax.numpy as jnp
from jax import lax
from jax.experimental import pallas as pl
from jax.experimental.pallas import tpu as pltpu

F32 = jnp.float32
BF16 = jnp.bfloat16

LANES = 128
MXU_W = 256
HEAD_W = 64
MOBA_BLOCK = 256
MOBA_TOPK = 3
MOBA_PAIRS = 4
MLSTM_HEADS = 4
MLSTM_CHUNK = 512
MLSTM_CONV = 4
MLSTM_GATE_CAP = 15.0
SWA_WINDOW = 128
SWA_GROUPS = 2
SWA_GROUP_HEADS = 8
N_EXPERTS = 8
RMS_EPS = 1e-6
NEG_INF = float("-inf")
MASK_NEG = -1e30
LOG2_E = 1.4426950408889634
MOE_TILE = 512
MOE_TF = 256
MOE_WBUF = 3
FFN_TM, FFN_TF = 512, 512
TOP_K = 2
RT_E1, RT_E2, RT_W1, RT_W2, RT_R1, RT_R2 = range(6)
VMEM_LIMIT = 56 * 1024 * 1024

PB_SWA_Q = 0
PB_MOBA_Q = 8
PB_MOBA_K = 12
PB_MOBA_V = 16
PB_ML_Q = 20
PB_ML_K = 24
PB_ML_V = 28
PB_ML_O = 32
PB_ML_G = 36
PB_SWA_K = 37
PB_SWA_V = 38
PROJ_BLOCKS = 39


def _params(sem):
    return pltpu.CompilerParams(dimension_semantics=sem, vmem_limit_bytes=VMEM_LIMIT)


def _rms(x, g):
    return x * lax.rsqrt(jnp.mean(x * x, axis=-1, keepdims=True) + RMS_EPS) * g


def _dot(a, b):
    return jnp.dot(a, b, preferred_element_type=F32)


def _dot_nt(a, b):
    return lax.dot_general(a, b, (((1,), (1,)), ((), ())), preferred_element_type=F32)


def _adaln_kernel(c_ref, w_ref, b_ref, o_ref):
    c = c_ref[...]
    cond = (c * jax.nn.sigmoid(c)).astype(BF16)
    o_ref[...] = _dot(cond, w_ref[...].astype(BF16)) + b_ref[...]


def _adaln(c, ada_w, ada_b):
    depth, d, n6 = ada_w.shape
    b = c.shape[0]
    tn = 1024
    return pl.pallas_call(
        _adaln_kernel,
        grid=(depth, n6 // tn),
        in_specs=[
            pl.BlockSpec((b, d), lambda l, j: (0, 0)),
            pl.BlockSpec((None, d, tn), lambda l, j: (l, 0, j)),
            pl.BlockSpec((None, 1, tn), lambda l, j: (l, 0, j)),
        ],
        out_specs=pl.BlockSpec((None, b, tn), lambda l, j: (l, 0, j)),
        out_shape=jax.ShapeDtypeStruct((depth, b, n6), F32),
        compiler_params=_params(("arbitrary", "arbitrary")),
        name="adaln",
    )(c, ada_w, ada_b.reshape(depth, 1, n6))


def _inproj_kernel(*refs, tn, n_cast):
    x_ref, g_ref, sc_ref, sh_ref, w_ref = refs[:5]
    cast_in = refs[5:5 + n_cast]
    o_ref = refs[5 + n_cast]
    cast_out = refs[6 + n_cast:]
    h = (_rms(x_ref[...], g_ref[...]) * (1.0 + sc_ref[...]) + sh_ref[...]).astype(BF16)
    nc = w_ref.shape[1]
    for c0 in range(0, nc, tn):
        c1 = min(c0 + tn, nc)
        o_ref[:, c0:c1] = _dot(h, w_ref[:, c0:c1])
    for src, dst in zip(cast_in, cast_out):
        dst[...] = src[...].astype(BF16)


def _row_cast_job(w3, idx, steps):
    _, rows, cols = w3.shape
    share = 1 if rows % (16 * steps) == 0 else 2
    assert steps % share == 0 and rows % (16 * (steps // share)) == 0, w3.shape
    blk = rows // (steps // share)
    return (w3, pl.BlockSpec((None, blk, cols), lambda i: (idx, i // share, 0)),
            jax.ShapeDtypeStruct((rows, cols), BF16),
            pl.BlockSpec((blk, cols), lambda i: (i // share, 0)))


def _inproj(x2, g, mod, w, seq, cast=()):
    n, d = x2.shape
    nc = w.shape[1]
    tm, tn = 256, 7 * MXU_W
    per_b = seq // tm
    jobs = [_row_cast_job(w3, idx, n // tm) for w3, idx in cast]
    outs = pl.pallas_call(
        functools.partial(_inproj_kernel, tn=tn, n_cast=len(jobs)),
        grid=(n // tm,),
        in_specs=[
            pl.BlockSpec((tm, d), lambda i: (i, 0)),
            pl.BlockSpec((1, d), lambda i: (0, 0)),
            pl.BlockSpec((None, None, 1, d), lambda i: (i // per_b, 1, 0, 0)),
            pl.BlockSpec((None, None, 1, d), lambda i: (i // per_b, 0, 0, 0)),
            pl.BlockSpec((d, nc), lambda i: (0, 0), pipeline_mode=pl.Buffered(1)),
        ] + [job[1] for job in jobs],
        out_specs=[pl.BlockSpec((tm, nc), lambda i: (i, 0))] + [job[3] for job in jobs],
        out_shape=[jax.ShapeDtypeStruct((n, nc), F32)] + [job[2] for job in jobs],
        compiler_params=_params(("arbitrary",)),
        name="inproj",
    )(x2, g, mod, mod, w, *[job[0] for job in jobs])
    return outs[0], tuple(outs[1:])


def _moba_kernel(q_ref, k_ref, v_ref, o_ref, ka_scr, vb_scr):
    seq = k_ref.shape[0]
    blk = MOBA_BLOCK
    n_blk = seq // blk
    lane = lax.broadcasted_iota(jnp.int32, (1, LANES), 1)
    k = k_ref[...]
    vb_scr[...] = v_ref[...].astype(BF16)
    kmean = jnp.mean(k.reshape(n_blk, blk, LANES), axis=1)
    kblk = lax.shift_right_logical(lax.broadcasted_iota(jnp.int32, (seq, 1), 0), 8)
    ri = lax.broadcasted_iota(jnp.int32, (blk, blk), 0)
    ci = lax.broadcasted_iota(jnp.int32, (blk, blk), 1)
    causal_bias = jnp.where(ri >= ci, 0.0, MASK_NEG)
    scale = HEAD_W ** -0.5 * LOG2_E
    blk_id = lax.broadcasted_iota(jnp.int32, (n_blk, 1), 0)
    in_head, kmh = [], []
    for hh in range(2):
        base = HEAD_W * (1 - hh)
        in_head.append((lane >= HEAD_W * hh) & (lane < HEAD_W * (hh + 1)))
        ka_scr[hh] = jnp.where(in_head[hh], k, jnp.where(lane - base == kblk, 1.0, 0.0)).astype(BF16)
        kmh.append(jnp.where(in_head[hh], kmean, 0.0).astype(BF16))

    for i in range(n_blk):
        q = q_ref[i * blk:(i + 1) * blk, :]
        outs = []
        for hh in range(2):
            base = HEAD_W * (1 - hh)
            qa = jnp.where(in_head[hh], q * scale, 0.0)
            if i > MOBA_TOPK:
                qm = jnp.where(in_head[hh], q, 0.0).astype(BF16)
                valid = blk_id < i
                gate = jnp.where(valid, _dot_nt(kmh[hh], qm), NEG_INF)
                beaten_by = jnp.zeros((n_blk, blk), jnp.int32)
                for j in range(i):
                    gj = gate[j:j + 1, :]
                    beats = (gj > gate) | ((gj == gate) & (blk_id > j))
                    beaten_by = beaten_by + beats.astype(jnp.int32)
                drop = jnp.where(valid & (beaten_by >= MOBA_TOPK), MASK_NEG, 0.0)
                rows = [drop, jnp.zeros((LANES - base - n_blk, blk), F32)]
                if base:
                    rows.insert(0, jnp.zeros((base, blk), F32))
                qa = qa + jnp.concatenate(rows, axis=0).T
            s = _dot_nt(qa.astype(BF16), ka_scr[hh, 0:(i + 1) * blk, :])
            s_own = s[:, i * blk:] + causal_bias
            m = jnp.max(s_own, axis=-1, keepdims=True)
            if i:
                s_past = s[:, :i * blk]
                m = jnp.maximum(m, jnp.max(s_past, axis=-1, keepdims=True))
            p_own = jnp.exp2(s_own - m)
            l = jnp.sum(p_own, axis=-1, keepdims=True)
            acc = _dot(p_own.astype(BF16), vb_scr[i * blk:(i + 1) * blk, :])
            if i:
                p_past = jnp.exp2(s_past - m)
                l = l + jnp.sum(p_past, axis=-1, keepdims=True)
                acc = acc + _dot(p_past.astype(BF16), vb_scr[0:i * blk, :])
            outs.append(acc / l)
        o_ref[i * blk:(i + 1) * blk, :] = jnp.where(lane < HEAD_W, outs[0], outs[1]).astype(o_ref.dtype)


def _moba(proj, batch, seq):
    n = proj.shape[0]
    return pl.pallas_call(
        _moba_kernel,
        grid=(batch, MOBA_PAIRS),
        in_specs=[
            pl.BlockSpec((seq, LANES), lambda b, p: (b, PB_MOBA_Q + p)),
            pl.BlockSpec((seq, LANES), lambda b, p: (b, PB_MOBA_K + p)),
            pl.BlockSpec((seq, LANES), lambda b, p: (b, PB_MOBA_V + p)),
        ],
        out_specs=pl.BlockSpec((seq, LANES), lambda b, p: (b, p)),
        out_shape=jax.ShapeDtypeStruct((n, MOBA_PAIRS * LANES), BF16),
        scratch_shapes=[
            pltpu.VMEM((2, seq, LANES), BF16),
            pltpu.VMEM((seq, LANES), BF16),
        ],
        compiler_params=_params(("arbitrary", "arbitrary")),
        name="moba",
    )(proj, proj, proj)


def _causal_conv_silu(x, tail, w, b):
    row = lax.broadcasted_iota(jnp.int32, (8, 1), 0)
    y = b + w[MLSTM_CONV - 1:MLSTM_CONV, :] * x
    for shift in range(1, MLSTM_CONV):
        xr = pltpu.roll(x, shift, axis=0)
        head = jnp.where(row < shift, pltpu.roll(tail, shift, axis=0), xr[0:8, :])
        xs = jnp.concatenate([head, xr[8:, :]], axis=0)
        y = y + w[MLSTM_CONV - 1 - shift:MLSTM_CONV - shift, :] * xs
    return y * jax.nn.sigmoid(y)


def _mlstm_kernel(q_ref, k_ref, v_ref, og_ref, gate_ref, cwq_ref, cwk_ref, cbq_ref, cbk_ref,
                  gb_ref, nw_ref, o_ref, qtail_scr, ktail_scr, c_scr, n_scr, m_scr):
    ts = L = q_ref.shape[0]
    lane = lax.broadcasted_iota(jnp.int32, (1, LANES), 1)
    ri = lax.broadcasted_iota(jnp.int32, (L, L), 0)
    ci = lax.broadcasted_iota(jnp.int32, (L, L), 1)
    causal = ri >= ci
    lower = causal.astype(F32)
    upper = (ri <= ci).astype(F32)

    @pl.when(pl.program_id(1) == 0)
    def _():
        qtail_scr[...] = jnp.zeros_like(qtail_scr)
        ktail_scr[...] = jnp.zeros_like(ktail_scr)
        c_scr[...] = jnp.zeros_like(c_scr)
        n_scr[...] = jnp.zeros_like(n_scr)
        m_scr[...] = jnp.zeros_like(m_scr)

    t = MLSTM_GATE_CAP * jnp.tanh((gate_ref[...] + gb_ref[...]) / MLSTM_GATE_CAP)
    a_col = jnp.where(lane < MLSTM_HEADS, t, jax.nn.log_sigmoid(t))
    a_row = a_col.T[0:8, :]

    xq, xk = q_ref[...], k_ref[...]
    qc = _causal_conv_silu(xq, qtail_scr[...], cwq_ref[...], cbq_ref[...]).astype(BF16)
    kc = (_causal_conv_silu(xk, ktail_scr[...], cwk_ref[...], cbk_ref[...])
          * (LANES ** -0.5)).astype(BF16)
    qtail_scr[...] = xq[ts - 8:, :]
    ktail_scr[...] = xk[ts - 8:, :]

    b_c = jnp.dot(lower, a_col, precision=lax.Precision.HIGHEST, preferred_element_type=F32)
    b_r = jnp.dot(a_row, upper, precision=lax.Precision.HIGHEST, preferred_element_type=F32)

    local = []
    for hd in range(MLSTM_HEADS):
        cols = slice(hd * LANES, (hd + 1) * LANES)
        q, k = qc[:, cols], kc[:, cols]
        v = v_ref[:, cols].astype(BF16)
        fl = hd + MLSTM_HEADS
        b_col, li_col = b_c[:, fl:fl + 1], a_col[:, hd:hd + 1]
        b_row, li_row = b_r[fl:fl + 1, :], a_row[hd:hd + 1, :]
        b_last = b_row[:, L - 1:L]
        d_log = jnp.where(causal, b_col - b_row + li_row, NEG_INF)
        d_max = jnp.max(d_log, axis=1, keepdims=True)
        qk = _dot_nt(q, k) * jnp.exp(d_log - d_max)
        pv = _dot(qk.astype(BF16), v)
        qk_sum = jnp.sum(qk, axis=1, keepdims=True)
        a_max = jnp.max(b_last - b_row + li_row, axis=1, keepdims=True)
        kw = k.astype(F32) * jnp.exp(b_last - b_col + li_col - a_max)
        c_in = _dot(kw.T.astype(BF16), v)
        n_in = jnp.sum(kw, axis=0, keepdims=True)
        local.append((q, b_col, b_last, d_max, pv, qk_sum, a_max, c_in, n_in))

    for hd in range(MLSTM_HEADS):
        cols = slice(hd * LANES, (hd + 1) * LANES)
        q, b_col, b_last, d_max, pv, qk_sum, a_max, c_in, n_in = local[hd]
        c_st, n_st, m_st = c_scr[hd], n_scr[hd], m_scr[hd][:, 0:1]
        inter_log = b_col + m_st
        m_out = jnp.maximum(inter_log, d_max)
        w_inter = jnp.exp(inter_log - m_out)
        w_local = jnp.exp(d_max - m_out)
        num = w_local * pv + w_inter * _dot(q, c_st.astype(BF16))
        den = w_local * qk_sum + w_inter * jnp.sum(q.astype(F32) * n_st, axis=1, keepdims=True)
        hv = num / jnp.maximum(jnp.abs(den), jnp.exp(-m_out))
        hn = _rms(hv, nw_ref[:, cols])
        o_ref[:, cols] = (jax.nn.sigmoid(og_ref[:, cols]) * hn).astype(o_ref.dtype)
        m_new = jnp.maximum(b_last + m_st, a_max)
        decay = jnp.exp(b_last + m_st - m_new)
        inject = jnp.exp(a_max - m_new)
        c_scr[hd] = decay * c_st + inject * c_in
        n_scr[hd] = decay * n_st + inject * n_in
        m_scr[hd] = jnp.broadcast_to(m_new, (1, LANES))


def _mlstm(proj, conv_w, conv_b, gate_bias, norm_w, batch, seq):
    n = proj.shape[0]
    hw = MLSTM_HEADS
    gw = hw * LANES
    ts = MLSTM_CHUNK
    tiles = seq // ts

    def col(base):
        return pl.BlockSpec((ts, gw), lambda b, t: (b * tiles + t, base // hw))

    def vec(rows, blk):
        return pl.BlockSpec((rows, gw), lambda b, t: (0, blk))

    return pl.pallas_call(
        _mlstm_kernel,
        grid=(batch, tiles),
        in_specs=[
            col(PB_ML_Q), col(PB_ML_K), col(PB_ML_V), col(PB_ML_O),
            pl.BlockSpec((ts, LANES), lambda b, t: (b * tiles + t, PB_ML_G)),
            vec(MLSTM_CONV, 0), vec(MLSTM_CONV, 1), vec(1, 0), vec(1, 1),
            pl.BlockSpec((1, LANES), lambda b, t: (0, 0)),
            vec(1, 0),
        ],
        out_specs=pl.BlockSpec((ts, gw), lambda b, t: (b * tiles + t, 0)),
        out_shape=jax.ShapeDtypeStruct((n, gw), BF16),
        scratch_shapes=[
            pltpu.VMEM((8, gw), F32),
            pltpu.VMEM((8, gw), F32),
            pltpu.VMEM((hw, LANES, LANES), F32),
            pltpu.VMEM((hw, 1, LANES), F32),
            pltpu.VMEM((hw, 1, LANES), F32),
        ],
        compiler_params=_params(("arbitrary", "arbitrary")),
        name="mlstm",
    )(proj, proj, proj, proj, proj, conv_w, conv_w, conv_b, conv_b, gate_bias, norm_w)


def _swa_kernel(sink_ref, q_ref, k_ref, v_ref, o_ref, kb_scr, vb_scr):
    g = pl.program_id(1)
    seq = q_ref.shape[0]
    W = SWA_WINDOW
    n_pairs = SWA_GROUP_HEADS // 2
    lane = lax.broadcasted_iota(jnp.int32, (1, LANES), 1)
    low = lane < HEAD_W
    keep = low == (g == 0)
    for src, dst in ((k_ref, kb_scr), (v_ref, vb_scr)):
        both = src[...]
        dst[...] = jnp.where(keep, both, pltpu.roll(both, HEAD_W, axis=1)).astype(BF16)
    scale = HEAD_W ** -0.5 * LOG2_E
    sinks = [sink_ref[g * SWA_GROUP_HEADS + hd] * LOG2_E for hd in range(SWA_GROUP_HEADS)]
    r = lax.broadcasted_iota(jnp.int32, (W, 2 * W), 0)
    c = lax.broadcasted_iota(jnp.int32, (W, 2 * W), 1)
    bias_first = jnp.where(c <= r, 0.0, MASK_NEG)
    bias_rest = jnp.where((c > r) & (c <= r + W), 0.0, MASK_NEG)

    def block(nb, carry):
        start = pl.multiple_of(jnp.maximum(nb - 1, 0) * W, W)
        r0 = pl.multiple_of(nb * W, W)
        qn = q_ref[pl.ds(r0, W), :] * scale
        kband = kb_scr[pl.ds(start, 2 * W), :]
        vband = vb_scr[pl.ds(start, 2 * W), :]
        parts = []
        for p in range(n_pairs):
            qp = qn[:, p * LANES:(p + 1) * LANES]
            parts.append(jnp.where(low, qp, 0.0).astype(BF16))
            parts.append(jnp.where(low, 0.0, qp).astype(BF16))
        s_all = _dot_nt(jnp.concatenate(parts, axis=0), kband)
        bias = jnp.where(nb == 0, bias_first, bias_rest)
        probs, inv_l = [], []
        for hd in range(SWA_GROUP_HEADS):
            s = s_all[hd * W:(hd + 1) * W, :] + bias
            m = jnp.maximum(jnp.max(s, axis=-1, keepdims=True), sinks[hd])
            e = jnp.exp2(s - m)
            inv_l.append(1.0 / (jnp.sum(e, axis=-1, keepdims=True) + jnp.exp2(sinks[hd] - m)))
            probs.append(e.astype(BF16))
        o_all = _dot(jnp.concatenate(probs, axis=0), vband)
        outs = []
        for p in range(n_pairs):
            lo = o_all[(2 * p) * W:(2 * p + 1) * W, :] * inv_l[2 * p]
            hi = o_all[(2 * p + 1) * W:(2 * p + 2) * W, :] * inv_l[2 * p + 1]
            outs.append(jnp.where(low, lo, hi))
        o_ref[pl.ds(r0, W), :] = jnp.concatenate(outs, axis=1).astype(o_ref.dtype)
        return carry

    lax.fori_loop(0, seq // W, block, 0)


def _swa(proj, sinks, batch, seq):
    n = proj.shape[0]
    gw = SWA_GROUP_HEADS * HEAD_W
    gb = gw // LANES
    return pl.pallas_call(
        _swa_kernel,
        grid=(batch, SWA_GROUPS),
        in_specs=[
            pl.BlockSpec(memory_space=pltpu.SMEM),
            pl.BlockSpec((seq, gw), lambda b, g: (b, PB_SWA_Q // gb + g)),
            pl.BlockSpec((seq, LANES), lambda b, g: (b, PB_SWA_K)),
            pl.BlockSpec((seq, LANES), lambda b, g: (b, PB_SWA_V)),
        ],
        out_specs=pl.BlockSpec((seq, gw), lambda b, g: (b, g)),
        out_shape=jax.ShapeDtypeStruct((n, SWA_GROUPS * gw), BF16),
        scratch_shapes=[pltpu.VMEM((seq, LANES), BF16), pltpu.VMEM((seq, LANES), BF16)],
        compiler_params=_params(("arbitrary", "arbitrary")),
        name="swa",
    )(sinks, proj, proj, proj)


def _outproj_kernel(*refs, with_router, sub):
    if with_router:
        (ya_ref, yb_ref, yc_ref, w_ref, x_ref, gpost_ref, gt_ref, gpre_ref, sc_ref, sh_ref,
         wr_ref, xo_ref, h_ref, route_ref, count_ref, count_scr) = refs
    else:
        (ya_ref, yb_ref, yc_ref, w_ref, x_ref, gpost_ref, gt_ref, gpre_ref, sc_ref, sh_ref,
         xo_ref, h_ref) = refs
    wa = ya_ref.shape[1]
    wb = yb_ref.shape[1]
    if with_router:
        @pl.when(pl.program_id(0) == 0)
        def _():
            count_scr[...] = jnp.zeros_like(count_scr)

    for r0 in range(0, x_ref.shape[0], sub):
        rows = slice(r0, r0 + sub)
        y = _dot(ya_ref[rows, :], w_ref[0:wa, :])
        y = y + _dot(yb_ref[rows, :], w_ref[wa:wa + wb, :])
        y = y + _dot(yc_ref[rows, :], w_ref[wa + wb:, :])
        xn = x_ref[rows, :] + gt_ref[...] * _rms(y, gpost_ref[...])
        xo_ref[rows, :] = xn
        hb = (_rms(xn, gpre_ref[...]) * (1.0 + sc_ref[...]) + sh_ref[...]).astype(BF16)
        h_ref[rows, :] = hb.astype(h_ref.dtype)
        if not with_router:
            continue
        tm = sub
        lane = lax.broadcasted_iota(jnp.int32, (1, LANES), 1)
        logits = jnp.where(lane < N_EXPERTS, _dot(hb, wr_ref[...]), NEG_INF)
        m1 = jnp.max(logits, axis=-1, keepdims=True)
        i1 = jnp.min(jnp.where(logits == m1, lane, LANES), axis=-1, keepdims=True)
        rest = jnp.where(lane == i1, NEG_INF, logits)
        m2 = jnp.max(rest, axis=-1, keepdims=True)
        i2 = jnp.min(jnp.where(rest == m2, lane, LANES), axis=-1, keepdims=True)
        e2 = jnp.exp(m2 - m1)
        w1 = 1.0 / (1.0 + e2)
        picked = jnp.where((lane == i1) | (lane == i2), 1.0, 0.0)
        ri = lax.broadcasted_iota(jnp.int32, (tm, tm), 0)
        ci = lax.broadcasted_iota(jnp.int32, (tm, tm), 1)
        before = jnp.where(ri > ci, 1.0, 0.0).astype(BF16)
        rank = _dot(before, picked.astype(BF16)) + count_scr[...]
        r1 = jnp.sum(jnp.where(lane == i1, rank, 0.0), axis=-1, keepdims=True)
        r2 = jnp.sum(jnp.where(lane == i2, rank, 0.0), axis=-1, keepdims=True)
        count_scr[...] += jnp.sum(picked, axis=0, keepdims=True)
        rec = jnp.where(lane == RT_E1, i1.astype(F32), 0.0)
        for slot_lane, val in ((RT_E2, i2.astype(F32)), (RT_W1, w1), (RT_W2, e2 * w1),
                               (RT_R1, r1), (RT_R2, r2)):
            rec = jnp.where(lane == slot_lane, val, rec)
        route_ref[rows, :] = rec
    if with_router:
        count_ref[...] = jnp.broadcast_to(count_scr[...], count_ref.shape)


def _outproj(ya, yb, yc, w, x2, gpost, gpre, mod, seq, w_router=None):
    n, d = x2.shape
    tm, sub = 512, 256
    per_b = seq // tm
    with_router = w_router is not None

    def rows(width):
        return pl.BlockSpec((tm, width), lambda i: (i, 0))

    def vec():
        return pl.BlockSpec((1, d), lambda i: (0, 0))

    def modrow(k):
        return pl.BlockSpec((None, None, 1, d), lambda i: (i // per_b, k, 0, 0))

    in_specs = [rows(ya.shape[1]), rows(yb.shape[1]), rows(yc.shape[1]),
                pl.BlockSpec((d, d), lambda i: (0, 0), pipeline_mode=pl.Buffered(1)), rows(d),
                vec(), modrow(2), vec(), modrow(4), modrow(3)]
    args = [ya, yb, yc, w, x2, gpost, mod, gpre, mod, mod]
    out_specs = [rows(d), rows(d)]
    out_shape = [jax.ShapeDtypeStruct((n, d), F32),
                 jax.ShapeDtypeStruct((n, d), F32 if with_router else BF16)]
    scratch = []
    if with_router:
        in_specs.append(pl.BlockSpec((d, LANES), lambda i: (0, 0)))
        args.append(w_router)
        out_specs += [rows(LANES), pl.BlockSpec((8, LANES), lambda i: (0, 0))]
        out_shape += [jax.ShapeDtypeStruct((n, LANES), F32), jax.ShapeDtypeStruct((8, LANES), F32)]
        scratch.append(pltpu.VMEM((1, LANES), F32))
    return pl.pallas_call(
        functools.partial(_outproj_kernel, with_router=with_router, sub=sub),
        grid=(n // tm,),
        in_specs=in_specs,
        out_specs=out_specs,
        out_shape=out_shape,
        scratch_shapes=scratch,
        compiler_params=_params(("arbitrary",)),
        name="outproj",
    )(*args)


def _swiglu_step(h, wg_ref, wu_ref, wd_ref):
    gate = _dot(h, wg_ref[...])
    a = gate * jax.nn.sigmoid(gate) * _dot(h, wu_ref[...])
    return _dot(a.astype(BF16), wd_ref[...])


def _ffn_kernel(*refs, n_cast):
    h_ref, wg_ref, wu_ref, wd_ref, x_ref, gpost_ref, gt_ref = refs[:7]
    cast_in = refs[7:7 + n_cast]
    o_ref = refs[7 + n_cast]
    cast_out = refs[8 + n_cast:]
    f = pl.program_id(1)

    @pl.when(f == 0)
    def _():
        o_ref[...] = jnp.zeros_like(o_ref)

    o_ref[...] += _swiglu_step(h_ref[...], wg_ref, wu_ref, wd_ref)

    @pl.when(f == pl.num_programs(1) - 1)
    def _():
        o_ref[...] = x_ref[...] + gt_ref[...] * _rms(o_ref[...], gpost_ref[...])

    for src, dst in zip(cast_in, cast_out):
        dst[...] = src[...].astype(BF16)


def _expert_cast_jobs(w_gate, w_up, w_down, gi, gf):
    n_e, d, dff = w_gate.shape
    rows = n_e * d // gi
    per_e = d // rows
    assert rows * gi == n_e * d and per_e * rows == d and rows % 16 == 0
    assert dff == gf * MOE_TF
    drows = n_e * dff // (gi * gf)
    assert drows * gi * gf == n_e * dff and drows % 16 == 0
    gu_in = pl.BlockSpec((rows, MOE_TF), lambda i, f: (i, f))
    gu_out = pl.BlockSpec((None, rows, MOE_TF), lambda i, f: (i // per_e * gf + f, i % per_e, 0))
    gu_shape = jax.ShapeDtypeStruct((n_e * gf, d, MOE_TF), BF16)
    dn_spec = pl.BlockSpec((drows, d), lambda i, f: (i * gf + f, 0))
    dn_shape = jax.ShapeDtypeStruct((n_e * dff, d), BF16)
    return [(w_gate.reshape(n_e * d, dff), gu_in, gu_shape, gu_out),
            (w_up.reshape(n_e * d, dff), gu_in, gu_shape, gu_out),
            (w_down.reshape(n_e * dff, d), dn_spec, dn_shape, dn_spec)]


def _ffn(h, wg, wu, wd, x2, gpost, mod, seq, cast_jobs_fn=None):
    n, d = x2.shape
    tm, tf = FFN_TM, FFN_TF
    per_b = seq // tm
    gi, gf = n // tm, wg.shape[1] // tf
    row = lambda i, f: (i, 0)
    jobs = cast_jobs_fn(gi, gf) if cast_jobs_fn else []
    outs = pl.pallas_call(
        functools.partial(_ffn_kernel, n_cast=len(jobs)),
        grid=(gi, gf),
        in_specs=[
            pl.BlockSpec((tm, d), row),
            pl.BlockSpec((d, tf), lambda i, f: (0, f)),
            pl.BlockSpec((d, tf), lambda i, f: (0, f)),
            pl.BlockSpec((tf, d), lambda i, f: (f, 0)),
            pl.BlockSpec((tm, d), row),
            pl.BlockSpec((1, d), lambda i, f: (0, 0)),
            pl.BlockSpec((None, None, 1, d), lambda i, f: (i // per_b, 5, 0, 0)),
        ] + [job[1] for job in jobs],
        out_specs=[pl.BlockSpec((tm, d), row)] + [job[3] for job in jobs],
        out_shape=[jax.ShapeDtypeStruct((n, d), F32)] + [job[2] for job in jobs],
        compiler_params=_params(("arbitrary", "arbitrary")),
        name="ffn",
    )(h, wg, wu, wd, x2, gpost, mod, *[job[0] for job in jobs])
    return outs[0], tuple(outs[1:])


def _dispatch_kernel(p1_ref, p2_ref, fill_ref, h_ref, xs_ref, zero_scr, sems, fill_sem, *, n_rows):
    tm = h_ref.shape[0]
    base = pl.program_id(0) * tm

    @pl.when(pl.program_id(0) == 0)
    def _():
        zero_scr[...] = jnp.zeros_like(zero_scr)
        tail = [pltpu.make_async_copy(zero_scr.at[pl.ds(0, MOE_TILE)],
                                      xs_ref.at[pl.ds(t0, MOE_TILE)], fill_sem)
                for t0 in range(n_rows, xs_ref.shape[0], MOE_TILE)]
        for cp in tail:
            cp.start()
        for cp in tail:
            cp.wait()
        for e in range(N_EXPERTS):
            cp = pltpu.make_async_copy(
                zero_scr, xs_ref.at[pl.ds(pl.multiple_of(fill_ref[e], 8), zero_scr.shape[0])],
                fill_sem)
            cp.start()
            cp.wait()

    def issue(r, carry):
        row = h_ref.at[pl.ds(r, 1)]
        pltpu.make_async_copy(row, xs_ref.at[pl.ds(p1_ref[base + r], 1)], sems.at[0]).start()
        pltpu.make_async_copy(row, xs_ref.at[pl.ds(p2_ref[base + r], 1)],
                              sems.at[1]).start(priority=1)
        return carry

    lax.fori_loop(0, tm, issue, 0, unroll=8)
    for k in range(TOP_K):
        pltpu.make_async_copy(h_ref, xs_ref.at[pl.ds(0, tm)], sems.at[k]).wait()


def _dispatch(h, p1, p2, fill_start, n_slots):
    n, d = h.shape
    tm = 512
    return pl.pallas_call(
        functools.partial(_dispatch_kernel, n_rows=TOP_K * n),
        grid_spec=pltpu.PrefetchScalarGridSpec(
            num_scalar_prefetch=3,
            grid=(n // tm,),
            in_specs=[pl.BlockSpec((tm, d), lambda i, p1, p2, fs: (i, 0))],
            out_specs=pl.BlockSpec(memory_space=pl.ANY),
            scratch_shapes=[pltpu.VMEM((MOE_TILE + 8, d), F32),
                            pltpu.SemaphoreType.DMA((TOP_K,)),
                            pltpu.SemaphoreType.DMA(())],
        ),
        out_shape=jax.ShapeDtypeStruct((n_slots, d), F32),
        compiler_params=_params(("arbitrary",)),
        name="moe_dispatch",
    )(p1, p2, fill_start, h)


def _moe_ffn_kernel(te_ref, nv_ref, x_ref, wg_hbm, wu_hbm, wd_hbm, y_ref, h_scr,
                    wg_buf, wu_buf, wd_buf, sems, *, tf):
    j = pl.program_id(0)
    n_used = nv_ref[0]
    nf = wd_hbm.shape[1] // tf

    def slices(tile, f, slot):
        e = te_ref[tile]
        c0 = pl.multiple_of(f * tf, tf)
        return (
            pltpu.make_async_copy(wg_hbm.at[e * nf + f], wg_buf.at[slot], sems.at[0, slot]),
            pltpu.make_async_copy(wu_hbm.at[e * nf + f], wu_buf.at[slot], sems.at[1, slot]),
            pltpu.make_async_copy(wd_hbm.at[e, pl.ds(c0, tf), :], wd_buf.at[slot], sems.at[2, slot]),
        )

    y_ref[...] = jnp.zeros_like(y_ref)

    ahead = MOE_WBUF - 1

    @pl.when(j < n_used)
    def _():
        @pl.when(j == 0)
        def _():
            for f0 in range(ahead):
                for cp in slices(0, f0, f0):
                    cp.start()

        h_scr[...] = x_ref[...].astype(BF16)

        def step(f, carry):
            count = j * nf + f
            slot = lax.rem(count, MOE_WBUF)
            for cp in slices(j, f, slot):
                cp.wait()

            wrap = f + ahead >= nf
            nxt_tile = jnp.where(wrap, j + 1, j)
            nxt_f = jnp.where(wrap, f + ahead - nf, f + ahead)

            @pl.when(nxt_tile < n_used)
            def _():
                for cp in slices(nxt_tile, nxt_f, lax.rem(count + ahead, MOE_WBUF)):
                    cp.start()

            y_ref[...] += _swiglu_step(h_scr[...], wg_buf.at[slot], wu_buf.at[slot],
                                       wd_buf.at[slot])
            return carry

        lax.fori_loop(0, nf, step, 0)


def _moe_ffn(xs, tile_expert, n_valid, wg, wu, wd):
    n_slots, d = xs.shape
    tm, tf = MOE_TILE, MOE_TF

    return pl.pallas_call(
        functools.partial(_moe_ffn_kernel, tf=tf),
        grid_spec=pltpu.PrefetchScalarGridSpec(
            num_scalar_prefetch=2,
            grid=(n_slots // tm,),
            in_specs=[
                pl.BlockSpec((tm, d), lambda j, te, nv: (jnp.minimum(j, nv[0] - 1), 0)),
                pl.BlockSpec(memory_space=pl.ANY),
                pl.BlockSpec(memory_space=pl.ANY),
                pl.BlockSpec(memory_space=pl.ANY),
            ],
            out_specs=pl.BlockSpec((tm, d), lambda j, te, nv: (j, 0)),
            scratch_shapes=[
                pltpu.VMEM((tm, d), BF16),
                pltpu.VMEM((MOE_WBUF, d, tf), BF16),
                pltpu.VMEM((MOE_WBUF, d, tf), BF16),
                pltpu.VMEM((MOE_WBUF, tf, d), BF16),
                pltpu.SemaphoreType.DMA((3, MOE_WBUF)),
            ],
        ),
        out_shape=jax.ShapeDtypeStruct((n_slots, d), F32),
        compiler_params=_params(("arbitrary",)),
        name="moe_ffn",
    )(tile_expert, n_valid, xs, wg, wu, wd)


def _combine_kernel(p1_ref, p2_ref, ys_ref, route_ref, x_ref, gpost_ref, gt_ref, o_ref,
                    y1_scr, y2_scr, sems):
    tm = x_ref.shape[0]
    base = pl.program_id(0) * tm

    def issue(r, carry):
        pltpu.make_async_copy(ys_ref.at[pl.ds(p1_ref[base + r], 1)], y1_scr.at[pl.ds(r, 1)],
                              sems.at[0]).start()
        pltpu.make_async_copy(ys_ref.at[pl.ds(p2_ref[base + r], 1)], y2_scr.at[pl.ds(r, 1)],
                              sems.at[1]).start(priority=1)
        return carry


    lax.fori_loop(0, tm, issue, 0, unroll=8)
    pltpu.make_async_copy(ys_ref.at[pl.ds(0, tm)], y1_scr, sems.at[0]).wait()
    pltpu.make_async_copy(ys_ref.at[pl.ds(0, tm)], y2_scr, sems.at[1]).wait()
    lane = lax.broadcasted_iota(jnp.int32, (1, LANES), 1)
    route = route_ref[...]
    w1 = jnp.sum(jnp.where(lane == RT_W1, route, 0.0), axis=-1, keepdims=True)
    w2 = jnp.sum(jnp.where(lane == RT_W2, route, 0.0), axis=-1, keepdims=True)
    y = w1 * y1_scr[...] + w2 * y2_scr[...]
    o_ref[...] = x_ref[...] + gt_ref[...] * _rms(y, gpost_ref[...])


def _combine(ys, p1, p2, route, x2, gpost, mod, seq):
    n, d = x2.shape
    tm = 512
    per_b = seq // tm
    row = lambda i, p1, p2: (i, 0)
    return pl.pallas_call(
        _combine_kernel,
        grid_spec=pltpu.PrefetchScalarGridSpec(
            num_scalar_prefetch=2,
            grid=(n // tm,),
            in_specs=[
                pl.BlockSpec(memory_space=pl.ANY),
                pl.BlockSpec((tm, LANES), row),
                pl.BlockSpec((tm, d), row),
                pl.BlockSpec((1, d), lambda i, p1, p2: (0, 0)),
                pl.BlockSpec((None, None, 1, d), lambda i, p1, p2: (i // per_b, 5, 0, 0)),
            ],
            out_specs=pl.BlockSpec((tm, d), row),
            scratch_shapes=[pltpu.VMEM((tm, d), F32), pltpu.VMEM((tm, d), F32),
                            pltpu.SemaphoreType.DMA((TOP_K,))],
        ),
        out_shape=jax.ShapeDtypeStruct((n, d), F32),
        compiler_params=_params(("arbitrary",)),
        name="moe_combine",
    )(p1, p2, ys, route, x2, gpost, mod)


def _route_plan(route, counts, n_tiles):
    e1 = route[:, RT_E1].astype(jnp.int32)
    e2 = route[:, RT_E2].astype(jnp.int32)
    cnt = counts[0, :N_EXPERTS].astype(jnp.int32)
    size = (cnt + MOE_TILE - 1) // MOE_TILE * MOE_TILE
    end = jnp.cumsum(size)
    start = end - size
    p1 = start[e1] + route[:, RT_R1].astype(jnp.int32)
    p2 = start[e2] + route[:, RT_R2].astype(jnp.int32)
    n_valid = end[-1] // MOE_TILE
    tile_start = jnp.minimum(jnp.arange(n_tiles), n_valid - 1) * MOE_TILE
    tile_expert = jnp.sum(tile_start[:, None] >= end[None, :], axis=1).astype(jnp.int32)
    fill_start = (start + cnt) // 8 * 8
    return p1, p2, fill_start, tile_expert, n_valid.reshape(1).astype(jnp.int32)


def _pack_kernel(w_ref, o_ref):
    gw = 4 * LANES
    n_gate = 2 * MLSTM_HEADS
    src_gate = 7 * gw
    src_swq = src_gate + n_gate
    w_swq = 2 * SWA_GROUP_HEADS * HEAD_W
    lane = lax.broadcasted_iota(jnp.int32, (1, LANES), 1)
    o_ref[:, PB_SWA_Q * LANES:PB_SWA_Q * LANES + w_swq] = (
        w_ref[:, src_swq:src_swq + w_swq].astype(BF16))
    o_ref[:, PB_MOBA_Q * LANES:PB_ML_G * LANES] = w_ref[:, 0:src_gate].astype(BF16)
    gates = w_ref[:, src_gate:src_gate + LANES]
    o_ref[:, PB_ML_G * LANES:(PB_ML_G + 1) * LANES] = jnp.where(lane < n_gate, gates, 0.0).astype(BF16)
    src_kv = src_swq + w_swq
    o_ref[:, PB_SWA_K * LANES:(PB_SWA_V + 1) * LANES] = (
        w_ref[:, src_kv:src_kv + 2 * LANES].astype(BF16))


def _pack_w_in(w_in, l):
    _, d, n_in = w_in.shape
    tr = 256
    return pl.pallas_call(
        _pack_kernel,
        grid=(d // tr,),
        in_specs=[pl.BlockSpec((None, tr, n_in), lambda i: (l, i, 0))],
        out_specs=pl.BlockSpec((tr, PROJ_BLOCKS * LANES), lambda i: (i, 0)),
        out_shape=jax.ShapeDtypeStruct((d, PROJ_BLOCKS * LANES), BF16),
        compiler_params=_params(("arbitrary",)),
        name="pack_w_in",
    )(w_in.astype(BF16))


def kernel(x, c, ada_w, ada_b, g_pre_mix, g_post_mix, g_pre_ffn, g_post_ffn, w_in, w_out, conv_w,
           conv_b, igate_b, fgate_b, mlstm_norm_w, swa_sinks, ffn_w_gate, ffn_w_up, ffn_w_down,
           moe_router, moe_w_gate, moe_w_up, moe_w_down):
    batch, seq, d = x.shape
    depth = ada_w.shape[0]
    n = batch * seq
    x2 = x.reshape(n, d)
    mod_all = _adaln(c, ada_w, ada_b).reshape(depth, batch, 6, 1, d)
    for l in range(depth):
        mod = mod_all[l]
        j = l // 2
        cast = [(w_out, l)]
        if l % 2 == 0:
            cast += [(ffn_w_gate, j), (ffn_w_up, j), (ffn_w_down, j)]
        proj, casted = _inproj(x2, g_pre_mix[l].reshape(1, d), mod, _pack_w_in(w_in, l), seq, cast)
        y_moba = _moba(proj, batch, seq)
        gate_bias = jnp.concatenate(
            [igate_b[l], fgate_b[l], jnp.zeros((LANES - 2 * MLSTM_HEADS,), F32)]).reshape(1, LANES)
        y_mlstm = _mlstm(proj, conv_w[l], conv_b[l].reshape(1, -1), gate_bias,
                         mlstm_norm_w[l].reshape(1, -1), batch, seq)
        y_swa = _swa(proj, swa_sinks[l], batch, seq)
        w_router = None
        if l % 2 == 1:
            w_router = jnp.pad(moe_router[j], ((0, 0), (0, LANES - N_EXPERTS))).astype(BF16)
        outs = _outproj(y_moba, y_mlstm, y_swa, casted[0], x2,
                        g_post_mix[l].reshape(1, d), g_pre_ffn[l].reshape(1, d), mod, seq, w_router)
        gpost = g_post_ffn[l].reshape(1, d)
        if l % 2 == 0:
            x2, h = outs
            jobs_fn = None
            if l + 1 < depth:
                jn = (l + 1) // 2
                jobs_fn = functools.partial(_expert_cast_jobs, moe_w_gate[jn], moe_w_up[jn],
                                            moe_w_down[jn])
            x2, moe_bf16 = _ffn(h, casted[1], casted[2], casted[3], x2, gpost, mod, seq, jobs_fn)
        else:
            x2, h, route, counts = outs
            n_tiles = (TOP_K * n) // MOE_TILE + N_EXPERTS + 2
            p1, p2, fill_start, tile_expert, n_valid = _route_plan(route, counts, n_tiles)
            xs = _dispatch(h, p1, p2, fill_start, n_tiles * MOE_TILE)
            wg, wu, wd = moe_bf16
            ys = _moe_ffn(xs, tile_expert, n_valid, wg, wu, wd.reshape(N_EXPERTS, -1, d))
            x2 = _combine(ys, p1, p2, route, x2, gpost, mod, seq)
    return x2.reshape(batch, seq, d)
```

```python
import functools

import jax
import jax.numpy as jnp
from jax import lax
from jax.experimental import pallas as pl
from jax.experimental.pallas import tpu as pltpu

F32 = jnp.float32
BF16 = jnp.bfloat16

LANES = 128
MXU_W = 256
HEAD_W = 64
MOBA_BLOCK = 256
MOBA_TOPK = 3
MOBA_PAIRS = 4
MLSTM_HEADS = 4
MLSTM_CHUNK = 512
MLSTM_CONV = 4
MLSTM_GATE_CAP = 15.0
SWA_WINDOW = 128
SWA_GROUPS = 2
SWA_GROUP_HEADS = 8
N_EXPERTS = 8
RMS_EPS = 1e-6
NEG_INF = float("-inf")
MASK_NEG = -1e30
LOG2_E = 1.4426950408889634
MOE_TILE = 512
MOE_TF = 256
MOE_WBUF = 3
FFN_TM, FFN_TF = 512, 512
TOP_K = 2
RT_E1, RT_E2, RT_W1, RT_W2, RT_R1, RT_R2 = range(6)
VMEM_LIMIT = 56 * 1024 * 1024

PB_SWA_Q = 0
PB_MOBA_Q = 8
PB_MOBA_K = 12
PB_MOBA_V = 16
PB_ML_Q = 20
PB_ML_K = 24
PB_ML_V = 28
PB_ML_O = 32
PB_ML_G = 36
PB_SWA_K = 37
PB_SWA_V = 38
PROJ_BLOCKS = 39


def _params(sem):
    return pltpu.CompilerParams(dimension_semantics=sem, vmem_limit_bytes=VMEM_LIMIT)


def _rms(x, g):
    return x * lax.rsqrt(jnp.mean(x * x, axis=-1, keepdims=True) + RMS_EPS) * g


def _dot(a, b):
    return jnp.dot(a, b, preferred_element_type=F32)


def _bf16_parts(x):
    hi = x.astype(BF16)
    r1 = x - hi.astype(F32)
    mid = r1.astype(BF16)
    return hi, mid, (r1 - mid.astype(F32)).astype(BF16)


def _dot_nt(a, b):
    return lax.dot_general(a, b, (((1,), (1,)), ((), ())), preferred_element_type=F32)


def _adaln_kernel(c_ref, w_ref, b_ref, o_ref):
    c = c_ref[...]
    cond = (c * jax.nn.sigmoid(c)).astype(BF16)
    o_ref[...] = _dot(cond, w_ref[...].astype(BF16)) + b_ref[...]


def _adaln(c, ada_w, ada_b):
    depth, d, n6 = ada_w.shape
    b = c.shape[0]
    tn = 1024
    return pl.pallas_call(
        _adaln_kernel,
        grid=(depth, n6 // tn),
        in_specs=[
            pl.BlockSpec((b, d), lambda l, j: (0, 0)),
            pl.BlockSpec((None, d, tn), lambda l, j: (l, 0, j)),
            pl.BlockSpec((None, 1, tn), lambda l, j: (l, 0, j)),
        ],
        out_specs=pl.BlockSpec((None, b, tn), lambda l, j: (l, 0, j)),
        out_shape=jax.ShapeDtypeStruct((depth, b, n6), F32),
        compiler_params=_params(("arbitrary", "arbitrary")),
        name="adaln",
    )(c, ada_w, ada_b.reshape(depth, 1, n6))


def _inproj_kernel(*refs, tn, n_cast):
    x_ref, g_ref, sc_ref, sh_ref, w_ref = refs[:5]
    cast_in = refs[5:5 + n_cast]
    o_ref = refs[5 + n_cast]
    cast_out = refs[6 + n_cast:]
    h = (_rms(x_ref[...], g_ref[...]) * (1.0 + sc_ref[...]) + sh_ref[...]).astype(BF16)
    nc = w_ref.shape[1]
    for c0 in range(0, nc, tn):
        c1 = min(c0 + tn, nc)
        o_ref[:, c0:c1] = _dot(h, w_ref[:, c0:c1])
    for src, dst in zip(cast_in, cast_out):
        dst[...] = src[...].astype(BF16)


def _row_cast_job(w3, idx, steps):
    _, rows, cols = w3.shape
    share = 1 if rows % (16 * steps) == 0 else 2
    assert steps % share == 0 and rows % (16 * (steps // share)) == 0, w3.shape
    blk = rows // (steps // share)
    return (w3, pl.BlockSpec((None, blk, cols), lambda i: (idx, i // share, 0)),
            jax.ShapeDtypeStruct((rows, cols), BF16),
            pl.BlockSpec((blk, cols), lambda i: (i // share, 0)))


def _inproj(x2, g, mod, w, seq, cast=()):
    n, d = x2.shape
    nc = w.shape[1]
    tm, tn = 256, 7 * MXU_W
    per_b = seq // tm
    jobs = [_row_cast_job(w3, idx, n // tm) for w3, idx in cast]
    outs = pl.pallas_call(
        functools.partial(_inproj_kernel, tn=tn, n_cast=len(jobs)),
        grid=(n // tm,),
        in_specs=[
            pl.BlockSpec((tm, d), lambda i: (i, 0)),
            pl.BlockSpec((1, d), lambda i: (0, 0)),
            pl.BlockSpec((None, None, 1, d), lambda i: (i // per_b, 1, 0, 0)),
            pl.BlockSpec((None, None, 1, d), lambda i: (i // per_b, 0, 0, 0)),
            pl.BlockSpec((d, nc), lambda i: (0, 0), pipeline_mode=pl.Buffered(1)),
        ] + [job[1] for job in jobs],
        out_specs=[pl.BlockSpec((tm, nc), lambda i: (i, 0))] + [job[3] for job in jobs],
        out_shape=[jax.ShapeDtypeStruct((n, nc), F32)] + [job[2] for job in jobs],
        compiler_params=_params(("arbitrary",)),
        name="inproj",
    )(x2, g, mod, mod, w, *[job[0] for job in jobs])
    return outs[0], tuple(outs[1:])


def _moba_kernel(q_ref, k_ref, v_ref, o_ref, ka_scr, vb_scr):
    seq = k_ref.shape[0]
    blk = MOBA_BLOCK
    n_blk = seq // blk
    lane = lax.broadcasted_iota(jnp.int32, (1, LANES), 1)
    k = k_ref[...]
    vb_scr[...] = v_ref[...].astype(BF16)
    kmean = jnp.mean(k.reshape(n_blk, blk, LANES), axis=1)
    kblk = lax.shift_right_logical(lax.broadcasted_iota(jnp.int32, (seq, 1), 0), 8)
    ri = lax.broadcasted_iota(jnp.int32, (blk, blk), 0)
    ci = lax.broadcasted_iota(jnp.int32, (blk, blk), 1)
    causal_bias = jnp.where(ri >= ci, 0.0, MASK_NEG)
    scale = HEAD_W ** -0.5 * LOG2_E
    blk_id = lax.broadcasted_iota(jnp.int32, (n_blk, 1), 0)
    in_head, kmh = [], []
    for hh in range(2):
        base = HEAD_W * (1 - hh)
        in_head.append((lane >= HEAD_W * hh) & (lane < HEAD_W * (hh + 1)))
        ka_scr[hh] = jnp.where(in_head[hh], k, jnp.where(lane - base == kblk, 1.0, 0.0)).astype(BF16)
        kmh.append(jnp.where(in_head[hh], kmean, 0.0).astype(BF16))

    for i in range(n_blk):
        q = q_ref[i * blk:(i + 1) * blk, :]
        outs = []
        for hh in range(2):
            base = HEAD_W * (1 - hh)
            qa = jnp.where(in_head[hh], q * scale, 0.0)
            if i > MOBA_TOPK:
                qm = jnp.where(in_head[hh], q, 0.0).astype(BF16)
                valid = blk_id < i
                gate = jnp.where(valid, _dot_nt(kmh[hh], qm), NEG_INF)
                beaten_by = jnp.zeros((n_blk, blk), jnp.int32)
                for j in range(i):
                    gj = gate[j:j + 1, :]
                    beats = (gj > gate) | ((gj == gate) & (blk_id > j))
                    beaten_by = beaten_by + beats.astype(jnp.int32)
                drop = jnp.where(valid & (beaten_by >= MOBA_TOPK), MASK_NEG, 0.0)
                rows = [drop, jnp.zeros((LANES - base - n_blk, blk), F32)]
                if base:
                    rows.insert(0, jnp.zeros((base, blk), F32))
                qa = qa + jnp.concatenate(rows, axis=0).T
            s = _dot_nt(qa.astype(BF16), ka_scr[hh, 0:(i + 1) * blk, :])
            s_own = s[:, i * blk:] + causal_bias
            m = jnp.max(s_own, axis=-1, keepdims=True)
            if i:
                s_past = s[:, :i * blk]
                m = jnp.maximum(m, jnp.max(s_past, axis=-1, keepdims=True))
            p_own = jnp.exp2(s_own - m)
            l = jnp.sum(p_own, axis=-1, keepdims=True)
            acc = _dot(p_own.astype(BF16), vb_scr[i * blk:(i + 1) * blk, :])
            if i:
                p_past = jnp.exp2(s_past - m)
                l = l + jnp.sum(p_past, axis=-1, keepdims=True)
                acc = acc + _dot(p_past.astype(BF16), vb_scr[0:i * blk, :])
            outs.append(acc / l)
        o_ref[i * blk:(i + 1) * blk, :] = jnp.where(lane < HEAD_W, outs[0], outs[1]).astype(o_ref.dtype)


def _moba(proj, batch, seq):
    n = proj.shape[0]
    return pl.pallas_call(
        _moba_kernel,
        grid=(batch, MOBA_PAIRS),
        in_specs=[
            pl.BlockSpec((seq, LANES), lambda b, p: (b, PB_MOBA_Q + p)),
            pl.BlockSpec((seq, LANES), lambda b, p: (b, PB_MOBA_K + p)),
            pl.BlockSpec((seq, LANES), lambda b, p: (b, PB_MOBA_V + p)),
        ],
        out_specs=pl.BlockSpec((seq, LANES), lambda b, p: (b, p)),
        out_shape=jax.ShapeDtypeStruct((n, MOBA_PAIRS * LANES), BF16),
        scratch_shapes=[
            pltpu.VMEM((2, seq, LANES), BF16),
            pltpu.VMEM((seq, LANES), BF16),
        ],
        compiler_params=_params(("arbitrary", "arbitrary")),
        name="moba",
    )(proj, proj, proj)


def _causal_conv_silu(x, tail, w, b):
    row = lax.broadcasted_iota(jnp.int32, (8, 1), 0)
    y = b + w[MLSTM_CONV - 1:MLSTM_CONV, :] * x
    for shift in range(1, MLSTM_CONV):
        xr = pltpu.roll(x, shift, axis=0)
        head = jnp.where(row < shift, pltpu.roll(tail, shift, axis=0), xr[0:8, :])
        xs = jnp.concatenate([head, xr[8:, :]], axis=0)
        y = y + w[MLSTM_CONV - 1 - shift:MLSTM_CONV - shift, :] * xs
    return y * jax.nn.sigmoid(y)


def _mlstm_kernel(q_ref, k_ref, v_ref, og_ref, gate_ref, cwq_ref, cwk_ref, cbq_ref, cbk_ref,
                  gb_ref, nw_ref, o_ref, qtail_scr, ktail_scr, c_scr, n_scr, m_scr):
    ts = L = q_ref.shape[0]
    lane = lax.broadcasted_iota(jnp.int32, (1, LANES), 1)
    ri = lax.broadcasted_iota(jnp.int32, (L, L), 0)
    ci = lax.broadcasted_iota(jnp.int32, (L, L), 1)
    causal = ri >= ci
    lower = causal.astype(BF16)
    upper = (ri <= ci).astype(BF16)

    @pl.when(pl.program_id(1) == 0)
    def _():
        qtail_scr[...] = jnp.zeros_like(qtail_scr)
        ktail_scr[...] = jnp.zeros_like(ktail_scr)
        c_scr[...] = jnp.zeros_like(c_scr)
        n_scr[...] = jnp.zeros_like(n_scr)
        m_scr[...] = jnp.zeros_like(m_scr)

    t = MLSTM_GATE_CAP * jnp.tanh((gate_ref[...] + gb_ref[...]) / MLSTM_GATE_CAP)
    a_col = jnp.where(lane < MLSTM_HEADS, t, jax.nn.log_sigmoid(t))
    a_row = a_col.T[0:8, :]

    xq, xk = q_ref[...], k_ref[...]
    qc = _causal_conv_silu(xq, qtail_scr[...], cwq_ref[...], cbq_ref[...]).astype(BF16)
    kc = (_causal_conv_silu(xk, ktail_scr[...], cwk_ref[...], cbk_ref[...])
          * (LANES ** -0.5)).astype(BF16)
    qtail_scr[...] = xq[ts - 8:, :]
    ktail_scr[...] = xk[ts - 8:, :]

    b_c = sum(_dot(lower, part) for part in _bf16_parts(a_col))
    b_r = sum(_dot(part, upper) for part in _bf16_parts(a_row))

    local = []
    for hd in range(MLSTM_HEADS):
        cols = slice(hd * LANES, (hd + 1) * LANES)
        q, k = qc[:, cols], kc[:, cols]
        v = v_ref[:, cols].astype(BF16)
        fl = hd + MLSTM_HEADS
        b_col, li_col = b_c[:, fl:fl + 1], a_col[:, hd:hd + 1]
        b_row, li_row = b_r[fl:fl + 1, :], a_row[hd:hd + 1, :]
        b_last = b_row[:, L - 1:L]
        d_log = jnp.where(causal, b_col - b_row + li_row, NEG_INF)
        d_max = jnp.max(d_log, axis=1, keepdims=True)
        qk = _dot_nt(q, k) * jnp.exp(d_log - d_max)
        pv = _dot(qk.astype(BF16), v)
        qk_sum = jnp.sum(qk, axis=1, keepdims=True)
        a_max = jnp.max(b_last - b_row + li_row, axis=1, keepdims=True)
        kw = k.astype(F32) * jnp.exp(b_last - b_col + li_col - a_max)
        c_in = _dot(kw.T.astype(BF16), v)
        n_in = jnp.sum(kw, axis=0, keepdims=True)
        local.append((q, b_col, b_last, d_max, pv, qk_sum, a_max, c_in, n_in))

    for hd in range(MLSTM_HEADS):
        cols = slice(hd * LANES, (hd + 1) * LANES)
        q, b_col, b_last, d_max, pv, qk_sum, a_max, c_in, n_in = local[hd]
        c_st, n_st, m_st = c_scr[hd], n_scr[hd], m_scr[hd][:, 0:1]
        inter_log = b_col + m_st
        m_out = jnp.maximum(inter_log, d_max)
        w_inter = jnp.exp(inter_log - m_out)
        w_local = jnp.exp(d_max - m_out)
        num = w_local * pv + w_inter * _dot(q, c_st.astype(BF16))
        den = w_local * qk_sum + w_inter * jnp.sum(q.astype(F32) * n_st, axis=1, keepdims=True)
        hv = num / jnp.maximum(jnp.abs(den), jnp.exp(-m_out))
        hn = _rms(hv, nw_ref[:, cols])
        o_ref[:, cols] = (jax.nn.sigmoid(og_ref[:, cols]) * hn).astype(o_ref.dtype)
        m_new = jnp.maximum(b_last + m_st, a_max)
        decay = jnp.exp(b_last + m_st - m_new)
        inject = jnp.exp(a_max - m_new)
        c_scr[hd] = decay * c_st + inject * c_in
        n_scr[hd] = decay * n_st + inject * n_in
        m_scr[hd] = jnp.broadcast_to(m_new, (1, LANES))


def _mlstm(proj, conv_w, conv_b, gate_bias, norm_w, batch, seq):
    n = proj.shape[0]
    hw = MLSTM_HEADS
    gw = hw * LANES
    ts = MLSTM_CHUNK
    tiles = seq // ts

    def col(base):
        return pl.BlockSpec((ts, gw), lambda b, t: (b * tiles + t, base // hw))

    def vec(rows, blk):
        return pl.BlockSpec((rows, gw), lambda b, t: (0, blk))

    return pl.pallas_call(
        _mlstm_kernel,
        grid=(batch, tiles),
        in_specs=[
            col(PB_ML_Q), col(PB_ML_K), col(PB_ML_V), col(PB_ML_O),
            pl.BlockSpec((ts, LANES), lambda b, t: (b * tiles + t, PB_ML_G)),
            vec(MLSTM_CONV, 0), vec(MLSTM_CONV, 1), vec(1, 0), vec(1, 1),
            pl.BlockSpec((1, LANES), lambda b, t: (0, 0)),
            vec(1, 0),
        ],
        out_specs=pl.BlockSpec((ts, gw), lambda b, t: (b * tiles + t, 0)),
        out_shape=jax.ShapeDtypeStruct((n, gw), BF16),
        scratch_shapes=[
            pltpu.VMEM((8, gw), F32),
            pltpu.VMEM((8, gw), F32),
            pltpu.VMEM((hw, LANES, LANES), F32),
            pltpu.VMEM((hw, 1, LANES), F32),
            pltpu.VMEM((hw, 1, LANES), F32),
        ],
        compiler_params=_params(("arbitrary", "arbitrary")),
        name="mlstm",
    )(proj, proj, proj, proj, proj, conv_w, conv_w, conv_b, conv_b, gate_bias, norm_w)


def _swa_kernel(sink_ref, q_ref, k_ref, v_ref, o_ref, kb_scr, vb_scr):
    g = pl.program_id(1)
    seq = q_ref.shape[0]
    W = SWA_WINDOW
    n_pairs = SWA_GROUP_HEADS // 2
    lane = lax.broadcasted_iota(jnp.int32, (1, LANES), 1)
    low = lane < HEAD_W
    keep = low == (g == 0)
    for src, dst in ((k_ref, kb_scr), (v_ref, vb_scr)):
        both = src[...]
        dst[...] = jnp.where(keep, both, pltpu.roll(both, HEAD_W, axis=1)).astype(BF16)
    scale = HEAD_W ** -0.5 * LOG2_E
    sinks = [sink_ref[g * SWA_GROUP_HEADS + hd] * LOG2_E for hd in range(SWA_GROUP_HEADS)]
    r = lax.broadcasted_iota(jnp.int32, (W, 2 * W), 0)
    c = lax.broadcasted_iota(jnp.int32, (W, 2 * W), 1)
    bias_first = jnp.where(c <= r, 0.0, MASK_NEG)
    bias_rest = jnp.where((c > r) & (c <= r + W), 0.0, MASK_NEG)

    def block(nb, carry):
        start = pl.multiple_of(jnp.maximum(nb - 1, 0) * W, W)
        r0 = pl.multiple_of(nb * W, W)
        qn = q_ref[pl.ds(r0, W), :] * scale
        kband = kb_scr[pl.ds(start, 2 * W), :]
        vband = vb_scr[pl.ds(start, 2 * W), :]
        parts = []
        for p in range(n_pairs):
            qp = qn[:, p * LANES:(p + 1) * LANES]
            parts.append(jnp.where(low, qp, 0.0).astype(BF16))
            parts.append(jnp.where(low, 0.0, qp).astype(BF16))
        s_all = _dot_nt(jnp.concatenate(parts, axis=0), kband)
        bias = jnp.where(nb == 0, bias_first, bias_rest)
        probs, inv_l = [], []
        for hd in range(SWA_GROUP_HEADS):
            s = s_all[hd * W:(hd + 1) * W, :] + bias
            m = jnp.maximum(jnp.max(s, axis=-1, keepdims=True), sinks[hd])
            e = jnp.exp2(s - m)
            inv_l.append(1.0 / (jnp.sum(e, axis=-1, keepdims=True) + jnp.exp2(sinks[hd] - m)))
            probs.append(e.astype(BF16))
        o_all = _dot(jnp.concatenate(probs, axis=0), vband)
        outs = []
        for p in range(n_pairs):
            lo = o_all[(2 * p) * W:(2 * p + 1) * W, :] * inv_l[2 * p]
            hi = o_all[(2 * p + 1) * W:(2 * p + 2) * W, :] * inv_l[2 * p + 1]
            outs.append(jnp.where(low, lo, hi))
        o_ref[pl.ds(r0, W), :] = jnp.concatenate(outs, axis=1).astype(o_ref.dtype)
        return carry

    lax.fori_loop(0, seq // W, block, 0)


def _swa(proj, sinks, batch, seq):
    n = proj.shape[0]
    gw = SWA_GROUP_HEADS * HEAD_W
    gb = gw // LANES
    return pl.pallas_call(
        _swa_kernel,
        grid=(batch, SWA_GROUPS),
        in_specs=[
            pl.BlockSpec(memory_space=pltpu.SMEM),
            pl.BlockSpec((seq, gw), lambda b, g: (b, PB_SWA_Q // gb + g)),
            pl.BlockSpec((seq, LANES), lambda b, g: (b, PB_SWA_K)),
            pl.BlockSpec((seq, LANES), lambda b, g: (b, PB_SWA_V)),
        ],
        out_specs=pl.BlockSpec((seq, gw), lambda b, g: (b, g)),
        out_shape=jax.ShapeDtypeStruct((n, SWA_GROUPS * gw), BF16),
        scratch_shapes=[pltpu.VMEM((seq, LANES), BF16), pltpu.VMEM((seq, LANES), BF16)],
        compiler_params=_params(("arbitrary", "arbitrary")),
        name="swa",
    )(sinks, proj, proj, proj)


def _outproj_kernel(*refs, with_router, sub):
    if with_router:
        (ya_ref, yb_ref, yc_ref, w_ref, x_ref, gpost_ref, gt_ref, gpre_ref, sc_ref, sh_ref,
         wr_ref, xo_ref, h_ref, route_ref, count_ref, count_scr) = refs
    else:
        (ya_ref, yb_ref, yc_ref, w_ref, x_ref, gpost_ref, gt_ref, gpre_ref, sc_ref, sh_ref,
         xo_ref, h_ref) = refs
    wa = ya_ref.shape[1]
    wb = yb_ref.shape[1]
    if with_router:
        @pl.when(pl.program_id(0) == 0)
        def _():
            count_scr[...] = jnp.zeros_like(count_scr)

    for r0 in range(0, x_ref.shape[0], sub):
        rows = slice(r0, r0 + sub)
        y = _dot(ya_ref[rows, :], w_ref[0:wa, :])
        y = y + _dot(yb_ref[rows, :], w_ref[wa:wa + wb, :])
        y = y + _dot(yc_ref[rows, :], w_ref[wa + wb:, :])
        xn = x_ref[rows, :] + gt_ref[...] * _rms(y, gpost_ref[...])
        xo_ref[rows, :] = xn
        hb = (_rms(xn, gpre_ref[...]) * (1.0 + sc_ref[...]) + sh_ref[...]).astype(BF16)
        h_ref[rows, :] = hb.astype(h_ref.dtype)
        if not with_router:
            continue
        tm = sub
        lane = lax.broadcasted_iota(jnp.int32, (1, LANES), 1)
        logits = jnp.where(lane < N_EXPERTS, _dot(hb, wr_ref[...]), NEG_INF)
        m1 = jnp.max(logits, axis=-1, keepdims=True)
        i1 = jnp.min(jnp.where(logits == m1, lane, LANES), axis=-1, keepdims=True)
        rest = jnp.where(lane == i1, NEG_INF, logits)
        m2 = jnp.max(rest, axis=-1, keepdims=True)
        i2 = jnp.min(jnp.where(rest == m2, lane, LANES), axis=-1, keepdims=True)
        e2 = jnp.exp(m2 - m1)
        w1 = 1.0 / (1.0 + e2)
        picked = jnp.where((lane == i1) | (lane == i2), 1.0, 0.0)
        ri = lax.broadcasted_iota(jnp.int32, (tm, tm), 0)
        ci = lax.broadcasted_iota(jnp.int32, (tm, tm), 1)
        before = jnp.where(ri > ci, 1.0, 0.0).astype(BF16)
        rank = _dot(before, picked.astype(BF16)) + count_scr[...]
        r1 = jnp.sum(jnp.where(lane == i1, rank, 0.0), axis=-1, keepdims=True)
        r2 = jnp.sum(jnp.where(lane == i2, rank, 0.0), axis=-1, keepdims=True)
        count_scr[...] += jnp.sum(picked, axis=0, keepdims=True)
        rec = jnp.where(lane == RT_E1, i1.astype(F32), 0.0)
        for slot_lane, val in ((RT_E2, i2.astype(F32)), (RT_W1, w1), (RT_W2, e2 * w1),
                               (RT_R1, r1), (RT_R2, r2)):
            rec = jnp.where(lane == slot_lane, val, rec)
        route_ref[rows, :] = rec
    if with_router:
        count_ref[...] = jnp.broadcast_to(count_scr[...], count_ref.shape)


def _outproj(ya, yb, yc, w, x2, gpost, gpre, mod, seq, w_router=None):
    n, d = x2.shape
    tm, sub = 512, 256
    per_b = seq // tm
    with_router = w_router is not None

    def rows(width):
        return pl.BlockSpec((tm, width), lambda i: (i, 0))

    def vec():
        return pl.BlockSpec((1, d), lambda i: (0, 0))

    def modrow(k):
        return pl.BlockSpec((None, None, 1, d), lambda i: (i // per_b, k, 0, 0))

    in_specs = [rows(ya.shape[1]), rows(yb.shape[1]), rows(yc.shape[1]),
                pl.BlockSpec((d, d), lambda i: (0, 0), pipeline_mode=pl.Buffered(1)), rows(d),
                vec(), modrow(2), vec(), modrow(4), modrow(3)]
    args = [ya, yb, yc, w, x2, gpost, mod, gpre, mod, mod]
    out_specs = [rows(d), rows(d)]
    out_shape = [jax.ShapeDtypeStruct((n, d), F32),
                 jax.ShapeDtypeStruct((n, d), F32 if with_router else BF16)]
    scratch = []
    if with_router:
        in_specs.append(pl.BlockSpec((d, LANES), lambda i: (0, 0)))
        args.append(w_router)
        out_specs += [rows(LANES), pl.BlockSpec((8, LANES), lambda i: (0, 0))]
        out_shape += [jax.ShapeDtypeStruct((n, LANES), F32), jax.ShapeDtypeStruct((8, LANES), F32)]
        scratch.append(pltpu.VMEM((1, LANES), F32))
    return pl.pallas_call(
        functools.partial(_outproj_kernel, with_router=with_router, sub=sub),
        grid=(n // tm,),
        in_specs=in_specs,
        out_specs=out_specs,
        out_shape=out_shape,
        scratch_shapes=scratch,
        compiler_params=_params(("arbitrary",)),
        name="outproj",
    )(*args)


def _swiglu_step(h, wg_ref, wu_ref, wd_ref):
    gate = _dot(h, wg_ref[...])
    a = gate * jax.nn.sigmoid(gate) * _dot(h, wu_ref[...])
    return _dot(a.astype(BF16), wd_ref[...])


def _ffn_kernel(*refs, n_cast):
    h_ref, wg_ref, wu_ref, wd_ref, x_ref, gpost_ref, gt_ref = refs[:7]
    cast_in = refs[7:7 + n_cast]
    o_ref = refs[7 + n_cast]
    cast_out = refs[8 + n_cast:]
    f = pl.program_id(1)

    @pl.when(f == 0)
    def _():
        o_ref[...] = jnp.zeros_like(o_ref)

    o_ref[...] += _swiglu_step(h_ref[...], wg_ref, wu_ref, wd_ref)

    @pl.when(f == pl.num_programs(1) - 1)
    def _():
        o_ref[...] = x_ref[...] + gt_ref[...] * _rms(o_ref[...], gpost_ref[...])

    for src, dst in zip(cast_in, cast_out):
        dst[...] = src[...].astype(BF16)


def _expert_cast_jobs(w_gate, w_up, w_down, gi, gf):
    n_e, d, dff = w_gate.shape
    rows = n_e * d // gi
    per_e = d // rows
    assert rows * gi == n_e * d and per_e * rows == d and rows % 16 == 0
    assert dff == gf * MOE_TF
    drows = n_e * dff // (gi * gf)
    assert drows * gi * gf == n_e * dff and drows % 16 == 0
    gu_in = pl.BlockSpec((rows, MOE_TF), lambda i, f: (i, f))
    gu_out = pl.BlockSpec((None, rows, MOE_TF), lambda i, f: (i // per_e * gf + f, i % per_e, 0))
    gu_shape = jax.ShapeDtypeStruct((n_e * gf, d, MOE_TF), BF16)
    dn_spec = pl.BlockSpec((drows, d), lambda i, f: (i * gf + f, 0))
    dn_shape = jax.ShapeDtypeStruct((n_e * dff, d), BF16)
    return [(w_gate.reshape(n_e * d, dff), gu_in, gu_shape, gu_out),
            (w_up.reshape(n_e * d, dff), gu_in, gu_shape, gu_out),
            (w_down.reshape(n_e * dff, d), dn_spec, dn_shape, dn_spec)]


def _ffn(h, wg, wu, wd, x2, gpost, mod, seq, cast_jobs_fn=None):
    n, d = x2.shape
    tm, tf = FFN_TM, FFN_TF
    per_b = seq // tm
    gi, gf = n // tm, wg.shape[1] // tf
    row = lambda i, f: (i, 0)
    jobs = cast_jobs_fn(gi, gf) if cast_jobs_fn else []
    outs = pl.pallas_call(
        functools.partial(_ffn_kernel, n_cast=len(jobs)),
        grid=(gi, gf),
        in_specs=[
            pl.BlockSpec((tm, d), row),
            pl.BlockSpec((d, tf), lambda i, f: (0, f)),
            pl.BlockSpec((d, tf), lambda i, f: (0, f)),
            pl.BlockSpec((tf, d), lambda i, f: (f, 0)),
            pl.BlockSpec((tm, d), row),
            pl.BlockSpec((1, d), lambda i, f: (0, 0)),
            pl.BlockSpec((None, None, 1, d), lambda i, f: (i // per_b, 5, 0, 0)),
        ] + [job[1] for job in jobs],
        out_specs=[pl.BlockSpec((tm, d), row)] + [job[3] for job in jobs],
        out_shape=[jax.ShapeDtypeStruct((n, d), F32)] + [job[2] for job in jobs],
        compiler_params=_params(("arbitrary", "arbitrary")),
        name="ffn",
    )(h, wg, wu, wd, x2, gpost, mod, *[job[0] for job in jobs])
    return outs[0], tuple(outs[1:])


def _dispatch_kernel(p1_ref, p2_ref, fill_ref, h_ref, xs_ref, zero_scr, sems, fill_sem, *, n_rows):
    tm = h_ref.shape[0]
    base = pl.program_id(0) * tm

    @pl.when(pl.program_id(0) == 0)
    def _():
        zero_scr[...] = jnp.zeros_like(zero_scr)
        tail = [pltpu.make_async_copy(zero_scr.at[pl.ds(0, MOE_TILE)],
                                      xs_ref.at[pl.ds(t0, MOE_TILE)], fill_sem)
                for t0 in range(n_rows, xs_ref.shape[0], MOE_TILE)]
        for cp in tail:
            cp.start()
        for cp in tail:
            cp.wait()
        for e in range(N_EXPERTS):
            cp = pltpu.make_async_copy(
                zero_scr, xs_ref.at[pl.ds(pl.multiple_of(fill_ref[e], 8), zero_scr.shape[0])],
                fill_sem)
            cp.start()
            cp.wait()

    def issue(r, carry):
        row = h_ref.at[pl.ds(r, 1)]
        pltpu.make_async_copy(row, xs_ref.at[pl.ds(p1_ref[base + r], 1)], sems.at[0]).start()
        pltpu.make_async_copy(row, xs_ref.at[pl.ds(p2_ref[base + r], 1)],
                              sems.at[1]).start(priority=1)
        return carry

    lax.fori_loop(0, tm, issue, 0, unroll=8)
    for k in range(TOP_K):
        pltpu.make_async_copy(h_ref, xs_ref.at[pl.ds(0, tm)], sems.at[k]).wait()


def _dispatch(h, p1, p2, fill_start, n_slots):
    n, d = h.shape
    tm = 512
    return pl.pallas_call(
        functools.partial(_dispatch_kernel, n_rows=TOP_K * n),
        grid_spec=pltpu.PrefetchScalarGridSpec(
            num_scalar_prefetch=3,
            grid=(n // tm,),
            in_specs=[pl.BlockSpec((tm, d), lambda i, p1, p2, fs: (i, 0))],
            out_specs=pl.BlockSpec(memory_space=pl.ANY),
            scratch_shapes=[pltpu.VMEM((MOE_TILE + 8, d), F32),
                            pltpu.SemaphoreType.DMA((TOP_K,)),
                            pltpu.SemaphoreType.DMA(())],
        ),
        out_shape=jax.ShapeDtypeStruct((n_slots, d), F32),
        compiler_params=_params(("arbitrary",)),
        name="moe_dispatch",
    )(p1, p2, fill_start, h)


def _moe_ffn_kernel(te_ref, nv_ref, x_ref, wg_hbm, wu_hbm, wd_hbm, y_ref, h_scr,
                    wg_buf, wu_buf, wd_buf, sems, *, tf):
    j = pl.program_id(0)
    n_used = nv_ref[0]
    nf = wd_hbm.shape[1] // tf

    def slices(tile, f, slot):
        e = te_ref[tile]
        c0 = pl.multiple_of(f * tf, tf)
        return (
            pltpu.make_async_copy(wg_hbm.at[e * nf + f], wg_buf.at[slot], sems.at[0, slot]),
            pltpu.make_async_copy(wu_hbm.at[e * nf + f], wu_buf.at[slot], sems.at[1, slot]),
            pltpu.make_async_copy(wd_hbm.at[e, pl.ds(c0, tf), :], wd_buf.at[slot], sems.at[2, slot]),
        )

    y_ref[...] = jnp.zeros_like(y_ref)

    ahead = MOE_WBUF - 1

    @pl.when(j < n_used)
    def _():
        @pl.when(j == 0)
        def _():
            for f0 in range(ahead):
                for cp in slices(0, f0, f0):
                    cp.start()

        h_scr[...] = x_ref[...].astype(BF16)

        def step(f, carry):
            count = j * nf + f
            slot = lax.rem(count, MOE_WBUF)
            for cp in slices(j, f, slot):
                cp.wait()

            wrap = f + ahead >= nf
            nxt_tile = jnp.where(wrap, j + 1, j)
            nxt_f = jnp.where(wrap, f + ahead - nf, f + ahead)

            @pl.when(nxt_tile < n_used)
            def _():
                for cp in slices(nxt_tile, nxt_f, lax.rem(count + ahead, MOE_WBUF)):
                    cp.start()

            y_ref[...] += _swiglu_step(h_scr[...], wg_buf.at[slot], wu_buf.at[slot],
                                       wd_buf.at[slot])
            return carry

        lax.fori_loop(0, nf, step, 0)


def _moe_ffn(xs, tile_expert, n_valid, wg, wu, wd):
    n_slots, d = xs.shape
    tm, tf = MOE_TILE, MOE_TF

    return pl.pallas_call(
        functools.partial(_moe_ffn_kernel, tf=tf),
        grid_spec=pltpu.PrefetchScalarGridSpec(
            num_scalar_prefetch=2,
            grid=(n_slots // tm,),
            in_specs=[
                pl.BlockSpec((tm, d), lambda j, te, nv: (jnp.minimum(j, nv[0] - 1), 0)),
                pl.BlockSpec(memory_space=pl.ANY),
                pl.BlockSpec(memory_space=pl.ANY),
                pl.BlockSpec(memory_space=pl.ANY),
            ],
            out_specs=pl.BlockSpec((tm, d), lambda j, te, nv: (j, 0)),
            scratch_shapes=[
                pltpu.VMEM((tm, d), BF16),
                pltpu.VMEM((MOE_WBUF, d, tf), BF16),
                pltpu.VMEM((MOE_WBUF, d, tf), BF16),
                pltpu.VMEM((MOE_WBUF, tf, d), BF16),
                pltpu.SemaphoreType.DMA((3, MOE_WBUF)),
            ],
        ),
        out_shape=jax.ShapeDtypeStruct((n_slots, d), F32),
        compiler_params=_params(("arbitrary",)),
        name="moe_ffn",
    )(tile_expert, n_valid, xs, wg, wu, wd)


def _combine_kernel(p1_ref, p2_ref, ys_ref, route_ref, x_ref, gpost_ref, gt_ref, o_ref,
                    y1_scr, y2_scr, sems):
    tm = x_ref.shape[0]
    i = pl.program_id(0)
    slot = lax.rem(i, 2)

    def gather(tile, to_slot):
        base = tile * tm

        def issue(r, carry):
            pltpu.make_async_copy(ys_ref.at[pl.ds(p1_ref[base + r], 1)],
                                  y1_scr.at[to_slot, pl.ds(r, 1)], sems.at[0, to_slot]).start()
            pltpu.make_async_copy(ys_ref.at[pl.ds(p2_ref[base + r], 1)],
                                  y2_scr.at[to_slot, pl.ds(r, 1)],
                                  sems.at[1, to_slot]).start(priority=1)
            return carry

        lax.fori_loop(0, tm, issue, 0, unroll=8)

    @pl.when(i == 0)
    def _():
        gather(0, 0)

    @pl.when(i + 1 < pl.num_programs(0))
    def _():
        gather(i + 1, 1 - slot)

    pltpu.make_async_copy(ys_ref.at[pl.ds(0, tm)], y1_scr.at[slot], sems.at[0, slot]).wait()
    pltpu.make_async_copy(ys_ref.at[pl.ds(0, tm)], y2_scr.at[slot], sems.at[1, slot]).wait()
    lane = lax.broadcasted_iota(jnp.int32, (1, LANES), 1)
    route = route_ref[...]
    w1 = jnp.sum(jnp.where(lane == RT_W1, route, 0.0), axis=-1, keepdims=True)
    w2 = jnp.sum(jnp.where(lane == RT_W2, route, 0.0), axis=-1, keepdims=True)
    y = w1 * y1_scr[slot] + w2 * y2_scr[slot]
    o_ref[...] = x_ref[...] + gt_ref[...] * _rms(y, gpost_ref[...])


def _combine(ys, p1, p2, route, x2, gpost, mod, seq):
    n, d = x2.shape
    tm = 512
    per_b = seq // tm
    row = lambda i, p1, p2: (i, 0)
    return pl.pallas_call(
        _combine_kernel,
        grid_spec=pltpu.PrefetchScalarGridSpec(
            num_scalar_prefetch=2,
            grid=(n // tm,),
            in_specs=[
                pl.BlockSpec(memory_space=pl.ANY),
                pl.BlockSpec((tm, LANES), row),
                pl.BlockSpec((tm, d), row),
                pl.BlockSpec((1, d), lambda i, p1, p2: (0, 0)),
                pl.BlockSpec((None, None, 1, d), lambda i, p1, p2: (i // per_b, 5, 0, 0)),
            ],
            out_specs=pl.BlockSpec((tm, d), row),
            scratch_shapes=[pltpu.VMEM((2, tm, d), F32), pltpu.VMEM((2, tm, d), F32),
                            pltpu.SemaphoreType.DMA((TOP_K, 2))],
        ),
        out_shape=jax.ShapeDtypeStruct((n, d), F32),
        compiler_params=_params(("arbitrary",)),
        name="moe_combine",
    )(p1, p2, ys, route, x2, gpost, mod)


def _route_plan(route, counts, n_tiles):
    e1 = route[:, RT_E1].astype(jnp.int32)
    e2 = route[:, RT_E2].astype(jnp.int32)
    cnt = counts[0, :N_EXPERTS].astype(jnp.int32)
    size = (cnt + MOE_TILE - 1) // MOE_TILE * MOE_TILE
    end = jnp.cumsum(size)
    start = end - size
    p1 = start[e1] + route[:, RT_R1].astype(jnp.int32)
    p2 = start[e2] + route[:, RT_R2].astype(jnp.int32)
    n_valid = end[-1] // MOE_TILE
    tile_start = jnp.minimum(jnp.arange(n_tiles), n_valid - 1) * MOE_TILE
    tile_expert = jnp.sum(tile_start[:, None] >= end[None, :], axis=1).astype(jnp.int32)
    fill_start = (start + cnt) // 8 * 8
    return p1, p2, fill_start, tile_expert, n_valid.reshape(1).astype(jnp.int32)


def _pack_kernel(w_ref, o_ref):
    gw = 4 * LANES
    n_gate = 2 * MLSTM_HEADS
    src_gate = 7 * gw
    src_swq = src_gate + n_gate
    w_swq = 2 * SWA_GROUP_HEADS * HEAD_W
    lane = lax.broadcasted_iota(jnp.int32, (1, LANES), 1)
    o_ref[:, PB_SWA_Q * LANES:PB_SWA_Q * LANES + w_swq] = (
        w_ref[:, src_swq:src_swq + w_swq].astype(BF16))
    o_ref[:, PB_MOBA_Q * LANES:PB_ML_G * LANES] = w_ref[:, 0:src_gate].astype(BF16)
    gates = w_ref[:, src_gate:src_gate + LANES]
    o_ref[:, PB_ML_G * LANES:(PB_ML_G + 1) * LANES] = jnp.where(lane < n_gate, gates, 0.0).astype(BF16)
    src_kv = src_swq + w_swq
    o_ref[:, PB_SWA_K * LANES:(PB_SWA_V + 1) * LANES] = (
        w_ref[:, src_kv:src_kv + 2 * LANES].astype(BF16))


def _pack_w_in(w_in, l):
    _, d, n_in = w_in.shape
    tr = 256
    return pl.pallas_call(
        _pack_kernel,
        grid=(d // tr,),
        in_specs=[pl.BlockSpec((None, tr, n_in), lambda i: (l, i, 0))],
        out_specs=pl.BlockSpec((tr, PROJ_BLOCKS * LANES), lambda i: (i, 0)),
        out_shape=jax.ShapeDtypeStruct((d, PROJ_BLOCKS * LANES), BF16),
        compiler_params=_params(("arbitrary",)),
        name="pack_w_in",
    )(w_in.astype(BF16))


def kernel(x, c, ada_w, ada_b, g_pre_mix, g_post_mix, g_pre_ffn, g_post_ffn, w_in, w_out, conv_w,
           conv_b, igate_b, fgate_b, mlstm_norm_w, swa_sinks, ffn_w_gate, ffn_w_up, ffn_w_down,
           moe_router, moe_w_gate, moe_w_up, moe_w_down):
    batch, seq, d = x.shape
    depth = ada_w.shape[0]
    n = batch * seq
    x2 = x.reshape(n, d)
    mod_all = _adaln(c, ada_w, ada_b).reshape(depth, batch, 6, 1, d)
    for l in range(depth):
        mod = mod_all[l]
        j = l // 2
        cast = [(w_out, l)]
        if l % 2 == 0:
            cast += [(ffn_w_gate, j), (ffn_w_up, j), (ffn_w_down, j)]
        proj, casted = _inproj(x2, g_pre_mix[l].reshape(1, d), mod, _pack_w_in(w_in, l), seq, cast)
        y_moba = _moba(proj, batch, seq)
        gate_bias = jnp.concatenate(
            [igate_b[l], fgate_b[l], jnp.zeros((LANES - 2 * MLSTM_HEADS,), F32)]).reshape(1, LANES)
        y_mlstm = _mlstm(proj, conv_w[l], conv_b[l].reshape(1, -1), gate_bias,
                         mlstm_norm_w[l].reshape(1, -1), batch, seq)
        y_swa = _swa(proj, swa_sinks[l], batch, seq)
        w_router = None
        if l % 2 == 1:
            w_router = jnp.pad(moe_router[j], ((0, 0), (0, LANES - N_EXPERTS))).astype(BF16)
        outs = _outproj(y_moba, y_mlstm, y_swa, casted[0], x2,
                        g_post_mix[l].reshape(1, d), g_pre_ffn[l].reshape(1, d), mod, seq, w_router)
        gpost = g_post_ffn[l].reshape(1, d)
        if l % 2 == 0:
            x2, h = outs
            jobs_fn = None
            if l + 1 < depth:
                jn = (l + 1) // 2
                jobs_fn = functools.partial(_expert_cast_jobs, moe_w_gate[jn], moe_w_up[jn],
                                            moe_w_down[jn])
            x2, moe_bf16 = _ffn(h, casted[1], casted[2], casted[3], x2, gpost, mod, seq, jobs_fn)
        else:
            x2, h, route, counts = outs
            n_tiles = (TOP_K * n) // MOE_TILE + N_EXPERTS + 2
            p1, p2, fill_start, tile_expert, n_valid = _route_plan(route, counts, n_tiles)
            xs = _dispatch(h, p1, p2, fill_start, n_tiles * MOE_TILE)
            wg, wu, wd = moe_bf16
            ys = _moe_ffn(xs, tile_expert, n_valid, wg, wu, wd.reshape(N_EXPERTS, -1, d))
            x2 = _combine(ys, p1, p2, route, x2, gpost, mod, seq)
    return x2.reshape(batch, seq, d)
```

```python
import functools

import jax
import jax.numpy as jnp
from jax import lax
from jax.experimental import pallas as pl
from jax.experimental.pallas import tpu as pltpu

F32 = jnp.float32
BF16 = jnp.bfloat16

LANES = 128
MXU_W = 256
HEAD_W = 64
MOBA_BLOCK = 256
MOBA_TOPK = 3
MOBA_PAIRS = 4
MLSTM_HEADS = 4
MLSTM_CHUNK = 512
MLSTM_CONV = 4
MLSTM_GATE_CAP = 15.0
SWA_WINDOW = 128
SWA_GROUPS = 2
SWA_GROUP_HEADS = 8
N_EXPERTS = 8
RMS_EPS = 1e-6
NEG_INF = float("-inf")
MASK_NEG = -1e30
LOG2_E = 1.4426950408889634
MOE_TILE = 512
MOE_TF = 256
MOE_WBUF = 4
FFN_TM, FFN_TF = 512, 512
TOP_K = 2
RT_E1, RT_E2, RT_W1, RT_W2, RT_R1, RT_R2 = range(6)
VMEM_LIMIT = 56 * 1024 * 1024

PB_SWA_Q = 0
PB_MOBA_Q = 8
PB_MOBA_K = 12
PB_MOBA_V = 16
PB_ML_Q = 20
PB_ML_K = 24
PB_ML_V = 28
PB_ML_O = 32
PB_ML_G = 36
PB_SWA_K = 37
PB_SWA_V = 38
PROJ_BLOCKS = 39


def _params(sem):
    return pltpu.CompilerParams(dimension_semantics=sem, vmem_limit_bytes=VMEM_LIMIT)


def _rms(x, g):
    return x * lax.rsqrt(jnp.mean(x * x, axis=-1, keepdims=True) + RMS_EPS) * g


def _dot(a, b):
    return jnp.dot(a, b, preferred_element_type=F32)


def _bf16_parts(x):
    hi = x.astype(BF16)
    r1 = x - hi.astype(F32)
    mid = r1.astype(BF16)
    return hi, mid, (r1 - mid.astype(F32)).astype(BF16)


def _dot_nt(a, b):
    return lax.dot_general(a, b, (((1,), (1,)), ((), ())), preferred_element_type=F32)


def _adaln_kernel(c_ref, w_ref, b_ref, o_ref):
    c = c_ref[...]
    cond = (c * jax.nn.sigmoid(c)).astype(BF16)
    o_ref[...] = _dot(cond, w_ref[...].astype(BF16)) + b_ref[...]


def _adaln(c, ada_w, ada_b):
    depth, d, n6 = ada_w.shape
    b = c.shape[0]
    tn = 1024
    return pl.pallas_call(
        _adaln_kernel,
        grid=(depth, n6 // tn),
        in_specs=[
            pl.BlockSpec((b, d), lambda l, j: (0, 0)),
            pl.BlockSpec((None, d, tn), lambda l, j: (l, 0, j)),
            pl.BlockSpec((None, 1, tn), lambda l, j: (l, 0, j)),
        ],
        out_specs=pl.BlockSpec((None, b, tn), lambda l, j: (l, 0, j)),
        out_shape=jax.ShapeDtypeStruct((depth, b, n6), F32),
        compiler_params=_params(("arbitrary", "arbitrary")),
        name="adaln",
    )(c, ada_w, ada_b.reshape(depth, 1, n6))


def _inproj_kernel(*refs, tn, n_cast):
    x_ref, g_ref, sc_ref, sh_ref, w_ref = refs[:5]
    cast_in = refs[5:5 + n_cast]
    o_ref = refs[5 + n_cast]
    cast_out = refs[6 + n_cast:]
    h = (_rms(x_ref[...], g_ref[...]) * (1.0 + sc_ref[...]) + sh_ref[...]).astype(BF16)
    nc = w_ref.shape[1]
    for c0 in range(0, nc, tn):
        c1 = min(c0 + tn, nc)
        o_ref[:, c0:c1] = _dot(h, w_ref[:, c0:c1])
    for src, dst in zip(cast_in, cast_out):
        dst[...] = src[...].astype(BF16)


def _row_cast_job(w3, idx, steps):
    _, rows, cols = w3.shape
    share = 1 if rows % (16 * steps) == 0 else 2
    assert steps % share == 0 and rows % (16 * (steps // share)) == 0, w3.shape
    blk = rows // (steps // share)
    return (w3, pl.BlockSpec((None, blk, cols), lambda i: (idx, i // share, 0)),
            jax.ShapeDtypeStruct((rows, cols), BF16),
            pl.BlockSpec((blk, cols), lambda i: (i // share, 0)))


def _inproj(x2, g, mod, w, seq, cast=()):
    n, d = x2.shape
    nc = w.shape[1]
    tm, tn = 256, 7 * MXU_W
    per_b = seq // tm
    jobs = [_row_cast_job(w3, idx, n // tm) for w3, idx in cast]
    outs = pl.pallas_call(
        functools.partial(_inproj_kernel, tn=tn, n_cast=len(jobs)),
        grid=(n // tm,),
        in_specs=[
            pl.BlockSpec((tm, d), lambda i: (i, 0)),
            pl.BlockSpec((1, d), lambda i: (0, 0)),
            pl.BlockSpec((None, None, 1, d), lambda i: (i // per_b, 1, 0, 0)),
            pl.BlockSpec((None, None, 1, d), lambda i: (i // per_b, 0, 0, 0)),
            pl.BlockSpec((d, nc), lambda i: (0, 0), pipeline_mode=pl.Buffered(1)),
        ] + [job[1] for job in jobs],
        out_specs=[pl.BlockSpec((tm, nc), lambda i: (i, 0))] + [job[3] for job in jobs],
        out_shape=[jax.ShapeDtypeStruct((n, nc), F32)] + [job[2] for job in jobs],
        compiler_params=_params(("arbitrary",)),
        name="inproj",
    )(x2, g, mod, mod, w, *[job[0] for job in jobs])
    return outs[0], tuple(outs[1:])


def _moba_kernel(q_ref, k_ref, v_ref, o_ref, ka_scr, vb_scr):
    seq = k_ref.shape[0]
    blk = MOBA_BLOCK
    n_blk = seq // blk
    lane = lax.broadcasted_iota(jnp.int32, (1, LANES), 1)
    k = k_ref[...]
    vb_scr[...] = v_ref[...].astype(BF16)
    kmean = jnp.mean(k.reshape(n_blk, blk, LANES), axis=1)
    kblk = lax.shift_right_logical(lax.broadcasted_iota(jnp.int32, (seq, 1), 0),
                                   MOBA_BLOCK.bit_length() - 1)
    ri = lax.broadcasted_iota(jnp.int32, (blk, blk), 0)
    ci = lax.broadcasted_iota(jnp.int32, (blk, blk), 1)
    causal_bias = jnp.where(ri >= ci, 0.0, MASK_NEG)
    scale = HEAD_W ** -0.5 * LOG2_E
    blk_id = lax.broadcasted_iota(jnp.int32, (n_blk, 1), 0)
    in_head, kmh = [], []
    for hh in range(2):
        base = HEAD_W * (1 - hh)
        in_head.append((lane >= HEAD_W * hh) & (lane < HEAD_W * (hh + 1)))
        ka_scr[hh] = jnp.where(in_head[hh], k, jnp.where(lane - base == kblk, 1.0, 0.0)).astype(BF16)
        kmh.append(jnp.where(in_head[hh], kmean, 0.0).astype(BF16))

    for i in range(n_blk):
        q = q_ref[i * blk:(i + 1) * blk, :]
        outs = []
        for hh in range(2):
            base = HEAD_W * (1 - hh)
            qa = jnp.where(in_head[hh], q * scale, 0.0)
            if i > MOBA_TOPK:
                qm = jnp.where(in_head[hh], q, 0.0).astype(BF16)
                valid = blk_id < i
                gate = jnp.where(valid, _dot_nt(kmh[hh], qm), NEG_INF)
                beaten_by = jnp.zeros((n_blk, blk), jnp.int32)
                for j in range(i):
                    gj = gate[j:j + 1, :]
                    beats = (gj > gate) | ((gj == gate) & (blk_id > j))
                    beaten_by = beaten_by + beats.astype(jnp.int32)
                drop = jnp.where(valid & (beaten_by >= MOBA_TOPK), MASK_NEG, 0.0)
                rows = [drop, jnp.zeros((LANES - base - n_blk, blk), F32)]
                if base:
                    rows.insert(0, jnp.zeros((base, blk), F32))
                qa = qa + jnp.concatenate(rows, axis=0).T
            s = _dot_nt(qa.astype(BF16), ka_scr[hh, 0:(i + 1) * blk, :])
            s_own = s[:, i * blk:] + causal_bias
            m = jnp.max(s_own, axis=-1, keepdims=True)
            if i:
                s_past = s[:, :i * blk]
                m = jnp.maximum(m, jnp.max(s_past, axis=-1, keepdims=True))
            p_own = jnp.exp2(s_own - m)
            l = jnp.sum(p_own, axis=-1, keepdims=True)
            acc = _dot(p_own.astype(BF16), vb_scr[i * blk:(i + 1) * blk, :])
            if i:
                p_past = jnp.exp2(s_past - m)
                l = l + jnp.sum(p_past, axis=-1, keepdims=True)
                acc = acc + _dot(p_past.astype(BF16), vb_scr[0:i * blk, :])
            outs.append(acc / l)
        o_ref[i * blk:(i + 1) * blk, :] = jnp.where(lane < HEAD_W, outs[0], outs[1]).astype(o_ref.dtype)


def _moba(proj, batch, seq):
    n = proj.shape[0]
    return pl.pallas_call(
        _moba_kernel,
        grid=(batch, MOBA_PAIRS),
        in_specs=[
            pl.BlockSpec((seq, LANES), lambda b, p: (b, PB_MOBA_Q + p)),
            pl.BlockSpec((seq, LANES), lambda b, p: (b, PB_MOBA_K + p)),
            pl.BlockSpec((seq, LANES), lambda b, p: (b, PB_MOBA_V + p)),
        ],
        out_specs=pl.BlockSpec((seq, LANES), lambda b, p: (b, p)),
        out_shape=jax.ShapeDtypeStruct((n, MOBA_PAIRS * LANES), BF16),
        scratch_shapes=[
            pltpu.VMEM((2, seq, LANES), BF16),
            pltpu.VMEM((seq, LANES), BF16),
        ],
        compiler_params=_params(("arbitrary", "arbitrary")),
        name="moba",
    )(proj, proj, proj)


def _causal_conv_silu(x, tail, w, b):
    row = lax.broadcasted_iota(jnp.int32, (8, 1), 0)
    y = b + w[MLSTM_CONV - 1:MLSTM_CONV, :] * x
    for shift in range(1, MLSTM_CONV):
        xr = pltpu.roll(x, shift, axis=0)
        head = jnp.where(row < shift, pltpu.roll(tail, shift, axis=0), xr[0:8, :])
        xs = jnp.concatenate([head, xr[8:, :]], axis=0)
        y = y + w[MLSTM_CONV - 1 - shift:MLSTM_CONV - shift, :] * xs
    return y * jax.nn.sigmoid(y)


def _mlstm_kernel(q_ref, k_ref, v_ref, og_ref, gate_ref, cwq_ref, cwk_ref, cbq_ref, cbk_ref,
                  gb_ref, nw_ref, o_ref, qtail_scr, ktail_scr, c_scr, n_scr, m_scr):
    ts = L = q_ref.shape[0]
    lane = lax.broadcasted_iota(jnp.int32, (1, LANES), 1)
    ri = lax.broadcasted_iota(jnp.int32, (L, L), 0)
    ci = lax.broadcasted_iota(jnp.int32, (L, L), 1)
    causal = ri >= ci
    lower = causal.astype(BF16)
    upper = (ri <= ci).astype(BF16)

    @pl.when(pl.program_id(1) == 0)
    def _():
        qtail_scr[...] = jnp.zeros_like(qtail_scr)
        ktail_scr[...] = jnp.zeros_like(ktail_scr)
        c_scr[...] = jnp.zeros_like(c_scr)
        n_scr[...] = jnp.zeros_like(n_scr)
        m_scr[...] = jnp.zeros_like(m_scr)

    t = MLSTM_GATE_CAP * jnp.tanh((gate_ref[...] + gb_ref[...]) / MLSTM_GATE_CAP)
    a_col = jnp.where(lane < MLSTM_HEADS, t, jax.nn.log_sigmoid(t))
    a_row = a_col.T[0:8, :]

    xq, xk = q_ref[...], k_ref[...]
    qc = _causal_conv_silu(xq, qtail_scr[...], cwq_ref[...], cbq_ref[...]).astype(BF16)
    kc = (_causal_conv_silu(xk, ktail_scr[...], cwk_ref[...], cbk_ref[...])
          * (LANES ** -0.5)).astype(BF16)
    qtail_scr[...] = xq[ts - 8:, :]
    ktail_scr[...] = xk[ts - 8:, :]

    b_c = sum(_dot(lower, part) for part in _bf16_parts(a_col))
    b_r = sum(_dot(part, upper) for part in _bf16_parts(a_row))

    local = []
    for hd in range(MLSTM_HEADS):
        cols = slice(hd * LANES, (hd + 1) * LANES)
        q, k = qc[:, cols], kc[:, cols]
        v = v_ref[:, cols].astype(BF16)
        fl = hd + MLSTM_HEADS
        b_col, li_col = b_c[:, fl:fl + 1], a_col[:, hd:hd + 1]
        b_row, li_row = b_r[fl:fl + 1, :], a_row[hd:hd + 1, :]
        b_last = b_row[:, L - 1:L]
        d_log = jnp.where(causal, b_col - b_row + li_row, NEG_INF)
        d_max = jnp.max(d_log, axis=1, keepdims=True)
        qk = _dot_nt(q, k) * jnp.exp(d_log - d_max)
        pv = _dot(qk.astype(BF16), v)
        qk_sum = jnp.sum(qk, axis=1, keepdims=True)
        a_max = jnp.max(b_last - b_row + li_row, axis=1, keepdims=True)
        kw = k.astype(F32) * jnp.exp(b_last - b_col + li_col - a_max)
        c_in = _dot(kw.T.astype(BF16), v)
        n_in = jnp.sum(kw, axis=0, keepdims=True)
        local.append((q, b_col, b_last, d_max, pv, qk_sum, a_max, c_in, n_in))

    for hd in range(MLSTM_HEADS):
        cols = slice(hd * LANES, (hd + 1) * LANES)
        q, b_col, b_last, d_max, pv, qk_sum, a_max, c_in, n_in = local[hd]
        c_st, n_st, m_st = c_scr[hd], n_scr[hd], m_scr[hd][:, 0:1]
        inter_log = b_col + m_st
        m_out = jnp.maximum(inter_log, d_max)
        w_inter = jnp.exp(inter_log - m_out)
        w_local = jnp.exp(d_max - m_out)
        num = w_local * pv + w_inter * _dot(q, c_st.astype(BF16))
        den = w_local * qk_sum + w_inter * jnp.sum(q.astype(F32) * n_st, axis=1, keepdims=True)
        hv = num / jnp.maximum(jnp.abs(den), jnp.exp(-m_out))
        hn = _rms(hv, nw_ref[:, cols])
        o_ref[:, cols] = (jax.nn.sigmoid(og_ref[:, cols]) * hn).astype(o_ref.dtype)
        m_new = jnp.maximum(b_last + m_st, a_max)
        decay = jnp.exp(b_last + m_st - m_new)
        inject = jnp.exp(a_max - m_new)
        c_scr[hd] = decay * c_st + inject * c_in
        n_scr[hd] = decay * n_st + inject * n_in
        m_scr[hd] = jnp.broadcast_to(m_new, (1, LANES))


def _mlstm(proj, conv_w, conv_b, gate_bias, norm_w, batch, seq):
    n = proj.shape[0]
    hw = MLSTM_HEADS
    gw = hw * LANES
    ts = MLSTM_CHUNK
    tiles = seq // ts

    def col(base):
        return pl.BlockSpec((ts, gw), lambda b, t: (b * tiles + t, base // hw))

    def vec(rows, blk):
        return pl.BlockSpec((rows, gw), lambda b, t: (0, blk))

    return pl.pallas_call(
        _mlstm_kernel,
        grid=(batch, tiles),
        in_specs=[
            col(PB_ML_Q), col(PB_ML_K), col(PB_ML_V), col(PB_ML_O),
            pl.BlockSpec((ts, LANES), lambda b, t: (b * tiles + t, PB_ML_G)),
            vec(MLSTM_CONV, 0), vec(MLSTM_CONV, 1), vec(1, 0), vec(1, 1),
            pl.BlockSpec((1, LANES), lambda b, t: (0, 0)),
            vec(1, 0),
        ],
        out_specs=pl.BlockSpec((ts, gw), lambda b, t: (b * tiles + t, 0)),
        out_shape=jax.ShapeDtypeStruct((n, gw), BF16),
        scratch_shapes=[
            pltpu.VMEM((8, gw), F32),
            pltpu.VMEM((8, gw), F32),
            pltpu.VMEM((hw, LANES, LANES), F32),
            pltpu.VMEM((hw, 1, LANES), F32),
            pltpu.VMEM((hw, 1, LANES), F32),
        ],
        compiler_params=_params(("arbitrary", "arbitrary")),
        name="mlstm",
    )(proj, proj, proj, proj, proj, conv_w, conv_w, conv_b, conv_b, gate_bias, norm_w)


def _swa_kernel(sink_ref, q_ref, k_ref, v_ref, o_ref, kb_scr, vb_scr):
    g = pl.program_id(1)
    seq = q_ref.shape[0]
    W = SWA_WINDOW
    n_pairs = SWA_GROUP_HEADS // 2
    lane = lax.broadcasted_iota(jnp.int32, (1, LANES), 1)
    low = lane < HEAD_W
    keep = low == (g == 0)
    for src, dst in ((k_ref, kb_scr), (v_ref, vb_scr)):
        both = src[...]
        dst[...] = jnp.where(keep, both, pltpu.roll(both, HEAD_W, axis=1)).astype(BF16)
    scale = HEAD_W ** -0.5 * LOG2_E
    sinks = [sink_ref[g * SWA_GROUP_HEADS + hd] * LOG2_E for hd in range(SWA_GROUP_HEADS)]
    r = lax.broadcasted_iota(jnp.int32, (W, 2 * W), 0)
    c = lax.broadcasted_iota(jnp.int32, (W, 2 * W), 1)
    bias_first = jnp.where(c <= r, 0.0, MASK_NEG)
    bias_rest = jnp.where((c > r) & (c <= r + W), 0.0, MASK_NEG)

    def block(nb, carry):
        start = pl.multiple_of(jnp.maximum(nb - 1, 0) * W, W)
        r0 = pl.multiple_of(nb * W, W)
        qn = q_ref[pl.ds(r0, W), :] * scale
        kband = kb_scr[pl.ds(start, 2 * W), :]
        vband = vb_scr[pl.ds(start, 2 * W), :]
        parts = []
        for p in range(n_pairs):
            qp = qn[:, p * LANES:(p + 1) * LANES]
            parts.append(jnp.where(low, qp, 0.0).astype(BF16))
            parts.append(jnp.where(low, 0.0, qp).astype(BF16))
        s_all = _dot_nt(jnp.concatenate(parts, axis=0), kband)
        bias = jnp.where(nb == 0, bias_first, bias_rest)
        probs, inv_l = [], []
        for hd in range(SWA_GROUP_HEADS):
            s = s_all[hd * W:(hd + 1) * W, :] + bias
            m = jnp.maximum(jnp.max(s, axis=-1, keepdims=True), sinks[hd])
            e = jnp.exp2(s - m)
            inv_l.append(1.0 / (jnp.sum(e, axis=-1, keepdims=True) + jnp.exp2(sinks[hd] - m)))
            probs.append(e.astype(BF16))
        o_all = _dot(jnp.concatenate(probs, axis=0), vband)
        outs = []
        for p in range(n_pairs):
            lo = o_all[(2 * p) * W:(2 * p + 1) * W, :] * inv_l[2 * p]
            hi = o_all[(2 * p + 1) * W:(2 * p + 2) * W, :] * inv_l[2 * p + 1]
            outs.append(jnp.where(low, lo, hi))
        o_ref[pl.ds(r0, W), :] = jnp.concatenate(outs, axis=1).astype(o_ref.dtype)
        return carry

    lax.fori_loop(0, seq // W, block, 0)


def _swa(proj, sinks, batch, seq):
    n = proj.shape[0]
    gw = SWA_GROUP_HEADS * HEAD_W
    gb = gw // LANES
    return pl.pallas_call(
        _swa_kernel,
        grid=(batch, SWA_GROUPS),
        in_specs=[
            pl.BlockSpec(memory_space=pltpu.SMEM),
            pl.BlockSpec((seq, gw), lambda b, g: (b, PB_SWA_Q // gb + g)),
            pl.BlockSpec((seq, LANES), lambda b, g: (b, PB_SWA_K)),
            pl.BlockSpec((seq, LANES), lambda b, g: (b, PB_SWA_V)),
        ],
        out_specs=pl.BlockSpec((seq, gw), lambda b, g: (b, g)),
        out_shape=jax.ShapeDtypeStruct((n, SWA_GROUPS * gw), BF16),
        scratch_shapes=[pltpu.VMEM((seq, LANES), BF16), pltpu.VMEM((seq, LANES), BF16)],
        compiler_params=_params(("arbitrary", "arbitrary")),
        name="swa",
    )(sinks, proj, proj, proj)


def _outproj_kernel(*refs, with_router, sub):
    if with_router:
        (ya_ref, yb_ref, yc_ref, w_ref, x_ref, gpost_ref, gt_ref, gpre_ref, sc_ref, sh_ref,
         wr_ref, xo_ref, h_ref, route_ref, count_ref, count_scr) = refs
    else:
        (ya_ref, yb_ref, yc_ref, w_ref, x_ref, gpost_ref, gt_ref, gpre_ref, sc_ref, sh_ref,
         xo_ref, h_ref) = refs
    wa = ya_ref.shape[1]
    wb = yb_ref.shape[1]
    if with_router:
        @pl.when(pl.program_id(0) == 0)
        def _():
            count_scr[...] = jnp.zeros_like(count_scr)

    for r0 in range(0, x_ref.shape[0], sub):
        rows = slice(r0, r0 + sub)
        y = _dot(ya_ref[rows, :], w_ref[0:wa, :])
        y = y + _dot(yb_ref[rows, :], w_ref[wa:wa + wb, :])
        y = y + _dot(yc_ref[rows, :], w_ref[wa + wb:, :])
        xn = x_ref[rows, :] + gt_ref[...] * _rms(y, gpost_ref[...])
        xo_ref[rows, :] = xn
        hb = (_rms(xn, gpre_ref[...]) * (1.0 + sc_ref[...]) + sh_ref[...]).astype(BF16)
        h_ref[rows, :] = hb.astype(h_ref.dtype)
        if not with_router:
            continue
        tm = sub
        lane = lax.broadcasted_iota(jnp.int32, (1, LANES), 1)
        logits = jnp.where(lane < N_EXPERTS, _dot(hb, wr_ref[...]), NEG_INF)
        m1 = jnp.max(logits, axis=-1, keepdims=True)
        i1 = jnp.min(jnp.where(logits == m1, lane, LANES), axis=-1, keepdims=True)
        rest = jnp.where(lane == i1, NEG_INF, logits)
        m2 = jnp.max(rest, axis=-1, keepdims=True)
        i2 = jnp.min(jnp.where(rest == m2, lane, LANES), axis=-1, keepdims=True)
        e2 = jnp.exp(m2 - m1)
        w1 = 1.0 / (1.0 + e2)
        picked = jnp.where((lane == i1) | (lane == i2), 1.0, 0.0)
        ri = lax.broadcasted_iota(jnp.int32, (tm, tm), 0)
        ci = lax.broadcasted_iota(jnp.int32, (tm, tm), 1)
        before = jnp.where(ri > ci, 1.0, 0.0).astype(BF16)
        rank = _dot(before, picked.astype(BF16)) + count_scr[...]
        r1 = jnp.sum(jnp.where(lane == i1, rank, 0.0), axis=-1, keepdims=True)
        r2 = jnp.sum(jnp.where(lane == i2, rank, 0.0), axis=-1, keepdims=True)
        count_scr[...] += jnp.sum(picked, axis=0, keepdims=True)
        rec = jnp.where(lane == RT_E1, i1.astype(F32), 0.0)
        for slot_lane, val in ((RT_E2, i2.astype(F32)), (RT_W1, w1), (RT_W2, e2 * w1),
                               (RT_R1, r1), (RT_R2, r2)):
            rec = jnp.where(lane == slot_lane, val, rec)
        route_ref[rows, :] = rec
    if with_router:
        count_ref[...] = jnp.broadcast_to(count_scr[...], count_ref.shape)


def _outproj(ya, yb, yc, w, x2, gpost, gpre, mod, seq, w_router=None):
    n, d = x2.shape
    tm, sub = 512, 256
    per_b = seq // tm
    with_router = w_router is not None

    def rows(width):
        return pl.BlockSpec((tm, width), lambda i: (i, 0))

    def vec():
        return pl.BlockSpec((1, d), lambda i: (0, 0))

    def modrow(k):
        return pl.BlockSpec((None, None, 1, d), lambda i: (i // per_b, k, 0, 0))

    in_specs = [rows(ya.shape[1]), rows(yb.shape[1]), rows(yc.shape[1]),
                pl.BlockSpec((d, d), lambda i: (0, 0), pipeline_mode=pl.Buffered(1)), rows(d),
                vec(), modrow(2), vec(), modrow(4), modrow(3)]
    args = [ya, yb, yc, w, x2, gpost, mod, gpre, mod, mod]
    out_specs = [rows(d), rows(d)]
    out_shape = [jax.ShapeDtypeStruct((n, d), F32),
                 jax.ShapeDtypeStruct((n, d), F32 if with_router else BF16)]
    scratch = []
    if with_router:
        in_specs.append(pl.BlockSpec((d, LANES), lambda i: (0, 0)))
        args.append(w_router)
        out_specs += [rows(LANES), pl.BlockSpec((8, LANES), lambda i: (0, 0))]
        out_shape += [jax.ShapeDtypeStruct((n, LANES), F32), jax.ShapeDtypeStruct((8, LANES), F32)]
        scratch.append(pltpu.VMEM((1, LANES), F32))
    return pl.pallas_call(
        functools.partial(_outproj_kernel, with_router=with_router, sub=sub),
        grid=(n // tm,),
        in_specs=in_specs,
        out_specs=out_specs,
        out_shape=out_shape,
        scratch_shapes=scratch,
        compiler_params=_params(("arbitrary",)),
        name="outproj",
    )(*args)


def _swiglu_step(h, wg_ref, wu_ref, wd_ref):
    gate = _dot(h, wg_ref[...])
    a = gate * jax.nn.sigmoid(gate) * _dot(h, wu_ref[...])
    return _dot(a.astype(BF16), wd_ref[...])


def _ffn_kernel(*refs, n_cast):
    h_ref, wg_ref, wu_ref, wd_ref, x_ref, gpost_ref, gt_ref = refs[:7]
    cast_in = refs[7:7 + n_cast]
    o_ref = refs[7 + n_cast]
    cast_out = refs[8 + n_cast:]
    f = pl.program_id(1)

    @pl.when(f == 0)
    def _():
        o_ref[...] = jnp.zeros_like(o_ref)

    o_ref[...] += _swiglu_step(h_ref[...], wg_ref, wu_ref, wd_ref)

    @pl.when(f == pl.num_programs(1) - 1)
    def _():
        o_ref[...] = x_ref[...] + gt_ref[...] * _rms(o_ref[...], gpost_ref[...])

    for src, dst in zip(cast_in, cast_out):
        dst[...] = src[...].astype(BF16)


def _expert_cast_jobs(w_gate, w_up, w_down, gi, gf):
    n_e, d, dff = w_gate.shape
    rows = n_e * d // gi
    per_e = d // rows
    assert rows * gi == n_e * d and per_e * rows == d and rows % 16 == 0
    assert dff == gf * MOE_TF
    drows = n_e * dff // (gi * gf)
    assert drows * gi * gf == n_e * dff and drows % 16 == 0
    gu_in = pl.BlockSpec((rows, MOE_TF), lambda i, f: (i, f))
    gu_out = pl.BlockSpec((None, rows, MOE_TF), lambda i, f: (i // per_e * gf + f, i % per_e, 0))
    gu_shape = jax.ShapeDtypeStruct((n_e * gf, d, MOE_TF), BF16)
    dn_spec = pl.BlockSpec((drows, d), lambda i, f: (i * gf + f, 0))
    dn_shape = jax.ShapeDtypeStruct((n_e * dff, d), BF16)
    return [(w_gate.reshape(n_e * d, dff), gu_in, gu_shape, gu_out),
            (w_up.reshape(n_e * d, dff), gu_in, gu_shape, gu_out),
            (w_down.reshape(n_e * dff, d), dn_spec, dn_shape, dn_spec)]


def _ffn(h, wg, wu, wd, x2, gpost, mod, seq, cast_jobs_fn=None):
    n, d = x2.shape
    tm, tf = FFN_TM, FFN_TF
    per_b = seq // tm
    gi, gf = n // tm, wg.shape[1] // tf
    row = lambda i, f: (i, 0)
    jobs = cast_jobs_fn(gi, gf) if cast_jobs_fn else []
    outs = pl.pallas_call(
        functools.partial(_ffn_kernel, n_cast=len(jobs)),
        grid=(gi, gf),
        in_specs=[
            pl.BlockSpec((tm, d), row),
            pl.BlockSpec((d, tf), lambda i, f: (0, f)),
            pl.BlockSpec((d, tf), lambda i, f: (0, f)),
            pl.BlockSpec((tf, d), lambda i, f: (f, 0)),
            pl.BlockSpec((tm, d), row),
            pl.BlockSpec((1, d), lambda i, f: (0, 0)),
            pl.BlockSpec((None, None, 1, d), lambda i, f: (i // per_b, 5, 0, 0)),
        ] + [job[1] for job in jobs],
        out_specs=[pl.BlockSpec((tm, d), row)] + [job[3] for job in jobs],
        out_shape=[jax.ShapeDtypeStruct((n, d), F32)] + [job[2] for job in jobs],
        compiler_params=_params(("arbitrary", "arbitrary")),
        name="ffn",
    )(h, wg, wu, wd, x2, gpost, mod, *[job[0] for job in jobs])
    return outs[0], tuple(outs[1:])


def _dispatch_kernel(p1_ref, p2_ref, fill_ref, h_ref, xs_ref, zero_scr, sems, fill_sem, *, n_rows):
    tm = h_ref.shape[0]
    base = pl.program_id(0) * tm

    @pl.when(pl.program_id(0) == 0)
    def _():
        zero_scr[...] = jnp.zeros_like(zero_scr)
        tail = [pltpu.make_async_copy(zero_scr.at[pl.ds(0, MOE_TILE)],
                                      xs_ref.at[pl.ds(t0, MOE_TILE)], fill_sem)
                for t0 in range(n_rows, xs_ref.shape[0], MOE_TILE)]
        for cp in tail:
            cp.start()
        for cp in tail:
            cp.wait()
        for e in range(N_EXPERTS):
            cp = pltpu.make_async_copy(
                zero_scr, xs_ref.at[pl.ds(pl.multiple_of(fill_ref[e], 8), zero_scr.shape[0])],
                fill_sem)
            cp.start()
            cp.wait()

    def issue(r, carry):
        row = h_ref.at[pl.ds(r, 1)]
        pltpu.make_async_copy(row, xs_ref.at[pl.ds(p1_ref[base + r], 1)], sems.at[0]).start()
        pltpu.make_async_copy(row, xs_ref.at[pl.ds(p2_ref[base + r], 1)],
                              sems.at[1]).start(priority=1)
        return carry

    lax.fori_loop(0, tm, issue, 0, unroll=8)
    for k in range(TOP_K):
        pltpu.make_async_copy(h_ref, xs_ref.at[pl.ds(0, tm)], sems.at[k]).wait()


def _dispatch(h, p1, p2, fill_start, n_slots):
    n, d = h.shape
    tm = 512
    return pl.pallas_call(
        functools.partial(_dispatch_kernel, n_rows=TOP_K * n),
        grid_spec=pltpu.PrefetchScalarGridSpec(
            num_scalar_prefetch=3,
            grid=(n // tm,),
            in_specs=[pl.BlockSpec((tm, d), lambda i, p1, p2, fs: (i, 0))],
            out_specs=pl.BlockSpec(memory_space=pl.ANY),
            scratch_shapes=[pltpu.VMEM((MOE_TILE + 8, d), F32),
                            pltpu.SemaphoreType.DMA((TOP_K,)),
                            pltpu.SemaphoreType.DMA(())],
        ),
        out_shape=jax.ShapeDtypeStruct((n_slots, d), F32),
        compiler_params=_params(("arbitrary",)),
        name="moe_dispatch",
    )(p1, p2, fill_start, h)


def _moe_ffn_kernel(te_ref, nv_ref, x_ref, wg_hbm, wu_hbm, wd_hbm, y_ref, h_scr,
                    wg_buf, wu_buf, wd_buf, sems, *, tf):
    j = pl.program_id(0)
    n_used = nv_ref[0]
    nf = wd_hbm.shape[1] // tf

    def slices(tile, f, slot):
        e = te_ref[tile]
        c0 = pl.multiple_of(f * tf, tf)
        return (
            pltpu.make_async_copy(wg_hbm.at[e * nf + f], wg_buf.at[slot], sems.at[0, slot]),
            pltpu.make_async_copy(wu_hbm.at[e * nf + f], wu_buf.at[slot], sems.at[1, slot]),
            pltpu.make_async_copy(wd_hbm.at[e, pl.ds(c0, tf), :], wd_buf.at[slot], sems.at[2, slot]),
        )

    y_ref[...] = jnp.zeros_like(y_ref)

    ahead = MOE_WBUF - 1

    @pl.when(j < n_used)
    def _():
        @pl.when(j == 0)
        def _():
            for f0 in range(ahead):
                for cp in slices(0, f0, f0):
                    cp.start()

        h_scr[...] = x_ref[...].astype(BF16)

        def step(f, carry):
            count = j * nf + f
            slot = lax.rem(count, MOE_WBUF)
            for cp in slices(j, f, slot):
                cp.wait()

            wrap = f + ahead >= nf
            nxt_tile = jnp.where(wrap, j + 1, j)
            nxt_f = jnp.where(wrap, f + ahead - nf, f + ahead)

            @pl.when(nxt_tile < n_used)
            def _():
                for cp in slices(nxt_tile, nxt_f, lax.rem(count + ahead, MOE_WBUF)):
                    cp.start()

            y_ref[...] += _swiglu_step(h_scr[...], wg_buf.at[slot], wu_buf.at[slot],
                                       wd_buf.at[slot])
            return carry

        lax.fori_loop(0, nf, step, 0)


def _moe_ffn(xs, tile_expert, n_valid, wg, wu, wd):
    n_slots, d = xs.shape
    tm, tf = MOE_TILE, MOE_TF

    return pl.pallas_call(
        functools.partial(_moe_ffn_kernel, tf=tf),
        grid_spec=pltpu.PrefetchScalarGridSpec(
            num_scalar_prefetch=2,
            grid=(n_slots // tm,),
            in_specs=[
                pl.BlockSpec((tm, d), lambda j, te, nv: (jnp.minimum(j, nv[0] - 1), 0)),
                pl.BlockSpec(memory_space=pl.ANY),
                pl.BlockSpec(memory_space=pl.ANY),
                pl.BlockSpec(memory_space=pl.ANY),
            ],
            out_specs=pl.BlockSpec((tm, d), lambda j, te, nv: (j, 0)),
            scratch_shapes=[
                pltpu.VMEM((tm, d), BF16),
                pltpu.VMEM((MOE_WBUF, d, tf), BF16),
                pltpu.VMEM((MOE_WBUF, d, tf), BF16),
                pltpu.VMEM((MOE_WBUF, tf, d), BF16),
                pltpu.SemaphoreType.DMA((3, MOE_WBUF)),
            ],
        ),
        out_shape=jax.ShapeDtypeStruct((n_slots, d), F32),
        compiler_params=_params(("arbitrary",)),
        name="moe_ffn",
    )(tile_expert, n_valid, xs, wg, wu, wd)


def _combine_kernel(p1_ref, p2_ref, ys_ref, route_ref, x_ref, gpost_ref, gt_ref, o_ref,
                    y1_scr, y2_scr, sems):
    tm = x_ref.shape[0]
    i = pl.program_id(0)
    slot = lax.rem(i, 2)

    def gather(tile, to_slot):
        base = tile * tm

        def issue(r, carry):
            pltpu.make_async_copy(ys_ref.at[pl.ds(p1_ref[base + r], 1)],
                                  y1_scr.at[to_slot, pl.ds(r, 1)], sems.at[0, to_slot]).start()
            pltpu.make_async_copy(ys_ref.at[pl.ds(p2_ref[base + r], 1)],
                                  y2_scr.at[to_slot, pl.ds(r, 1)],
                                  sems.at[1, to_slot]).start(priority=1)
            return carry

        lax.fori_loop(0, tm, issue, 0, unroll=8)

    @pl.when(i == 0)
    def _():
        gather(0, 0)

    @pl.when(i + 1 < pl.num_programs(0))
    def _():
        gather(i + 1, 1 - slot)

    pltpu.make_async_copy(ys_ref.at[pl.ds(0, tm)], y1_scr.at[slot], sems.at[0, slot]).wait()
    pltpu.make_async_copy(ys_ref.at[pl.ds(0, tm)], y2_scr.at[slot], sems.at[1, slot]).wait()
    lane = lax.broadcasted_iota(jnp.int32, (1, LANES), 1)
    route = route_ref[...]
    w1 = jnp.sum(jnp.where(lane == RT_W1, route, 0.0), axis=-1, keepdims=True)
    w2 = jnp.sum(jnp.where(lane == RT_W2, route, 0.0), axis=-1, keepdims=True)
    y = w1 * y1_scr[slot] + w2 * y2_scr[slot]
    o_ref[...] = x_ref[...] + gt_ref[...] * _rms(y, gpost_ref[...])


def _combine(ys, p1, p2, route, x2, gpost, mod, seq):
    n, d = x2.shape
    tm = 512
    per_b = seq // tm
    row = lambda i, p1, p2: (i, 0)
    return pl.pallas_call(
        _combine_kernel,
        grid_spec=pltpu.PrefetchScalarGridSpec(
            num_scalar_prefetch=2,
            grid=(n // tm,),
            in_specs=[
                pl.BlockSpec(memory_space=pl.ANY),
                pl.BlockSpec((tm, LANES), row),
                pl.BlockSpec((tm, d), row),
                pl.BlockSpec((1, d), lambda i, p1, p2: (0, 0)),
                pl.BlockSpec((None, None, 1, d), lambda i, p1, p2: (i // per_b, 5, 0, 0)),
            ],
            out_specs=pl.BlockSpec((tm, d), row),
            scratch_shapes=[pltpu.VMEM((2, tm, d), F32), pltpu.VMEM((2, tm, d), F32),
                            pltpu.SemaphoreType.DMA((TOP_K, 2))],
        ),
        out_shape=jax.ShapeDtypeStruct((n, d), F32),
        compiler_params=_params(("arbitrary",)),
        name="moe_combine",
    )(p1, p2, ys, route, x2, gpost, mod)


def _route_plan(route, counts, n_tiles):
    e1 = route[:, RT_E1].astype(jnp.int32)
    e2 = route[:, RT_E2].astype(jnp.int32)
    cnt = counts[0, :N_EXPERTS].astype(jnp.int32)
    size = (cnt + MOE_TILE - 1) // MOE_TILE * MOE_TILE
    end = jnp.cumsum(size)
    start = end - size
    p1 = start[e1] + route[:, RT_R1].astype(jnp.int32)
    p2 = start[e2] + route[:, RT_R2].astype(jnp.int32)
    n_valid = end[-1] // MOE_TILE
    tile_start = jnp.minimum(jnp.arange(n_tiles), n_valid - 1) * MOE_TILE
    tile_expert = jnp.sum(tile_start[:, None] >= end[None, :], axis=1).astype(jnp.int32)
    fill_start = (start + cnt) // 8 * 8
    return p1, p2, fill_start, tile_expert, n_valid.reshape(1).astype(jnp.int32)


def _pack_kernel(w_ref, o_ref):
    gw = 4 * LANES
    n_gate = 2 * MLSTM_HEADS
    src_gate = 7 * gw
    src_swq = src_gate + n_gate
    w_swq = 2 * SWA_GROUP_HEADS * HEAD_W
    lane = lax.broadcasted_iota(jnp.int32, (1, LANES), 1)
    o_ref[:, PB_SWA_Q * LANES:PB_SWA_Q * LANES + w_swq] = (
        w_ref[:, src_swq:src_swq + w_swq].astype(BF16))
    o_ref[:, PB_MOBA_Q * LANES:PB_ML_G * LANES] = w_ref[:, 0:src_gate].astype(BF16)
    gates = w_ref[:, src_gate:src_gate + LANES]
    o_ref[:, PB_ML_G * LANES:(PB_ML_G + 1) * LANES] = jnp.where(lane < n_gate, gates, 0.0).astype(BF16)
    src_kv = src_swq + w_swq
    o_ref[:, PB_SWA_K * LANES:(PB_SWA_V + 1) * LANES] = (
        w_ref[:, src_kv:src_kv + 2 * LANES].astype(BF16))


def _pack_w_in(w_in, l):
    _, d, n_in = w_in.shape
    tr = 256
    return pl.pallas_call(
        _pack_kernel,
        grid=(d // tr,),
        in_specs=[pl.BlockSpec((None, tr, n_in), lambda i: (l, i, 0))],
        out_specs=pl.BlockSpec((tr, PROJ_BLOCKS * LANES), lambda i: (i, 0)),
        out_shape=jax.ShapeDtypeStruct((d, PROJ_BLOCKS * LANES), BF16),
        compiler_params=_params(("arbitrary",)),
        name="pack_w_in",
    )(w_in.astype(BF16))


def kernel(x, c, ada_w, ada_b, g_pre_mix, g_post_mix, g_pre_ffn, g_post_ffn, w_in, w_out, conv_w,
           conv_b, igate_b, fgate_b, mlstm_norm_w, swa_sinks, ffn_w_gate, ffn_w_up, ffn_w_down,
           moe_router, moe_w_gate, moe_w_up, moe_w_down):
    batch, seq, d = x.shape
    depth = ada_w.shape[0]
    n = batch * seq
    x2 = x.reshape(n, d)
    mod_all = _adaln(c, ada_w, ada_b).reshape(depth, batch, 6, 1, d)
    for l in range(depth):
        mod = mod_all[l]
        j = l // 2
        cast = [(w_out, l)]
        if l % 2 == 0:
            cast += [(ffn_w_gate, j), (ffn_w_up, j), (ffn_w_down, j)]
        proj, casted = _inproj(x2, g_pre_mix[l].reshape(1, d), mod, _pack_w_in(w_in, l), seq, cast)
        y_moba = _moba(proj, batch, seq)
        gate_bias = jnp.concatenate(
            [igate_b[l], fgate_b[l], jnp.zeros((LANES - 2 * MLSTM_HEADS,), F32)]).reshape(1, LANES)
        y_mlstm = _mlstm(proj, conv_w[l], conv_b[l].reshape(1, -1), gate_bias,
                         mlstm_norm_w[l].reshape(1, -1), batch, seq)
        y_swa = _swa(proj, swa_sinks[l], batch, seq)
        w_router = None
        if l % 2 == 1:
            w_router = jnp.pad(moe_router[j], ((0, 0), (0, LANES - N_EXPERTS))).astype(BF16)
        outs = _outproj(y_moba, y_mlstm, y_swa, casted[0], x2,
                        g_post_mix[l].reshape(1, d), g_pre_ffn[l].reshape(1, d), mod, seq, w_router)
        gpost = g_post_ffn[l].reshape(1, d)
        if l % 2 == 0:
            x2, h = outs
            jobs_fn = None
            if l + 1 < depth:
                jn = (l + 1) // 2
                jobs_fn = functools.partial(_expert_cast_jobs, moe_w_gate[jn], moe_w_up[jn],
                                            moe_w_down[jn])
            x2, moe_bf16 = _ffn(h, casted[1], casted[2], casted[3], x2, gpost, mod, seq, jobs_fn)
        else:
            x2, h, route, counts = outs
            n_tiles = (TOP_K * n) // MOE_TILE + N_EXPERTS + 2
            p1, p2, fill_start, tile_expert, n_valid = _route_plan(route, counts, n_tiles)
            xs = _dispatch(h, p1, p2, fill_start, n_tiles * MOE_TILE)
            wg, wu, wd = moe_bf16
            ys = _moe_ffn(xs, tile_expert, n_valid, wg, wu, wd.reshape(N_EXPERTS, -1, d))
            x2 = _combine(ys, p1, p2, route, x2, gpost, mod, seq)
    return x2.reshape(batch, seq, d)
```

```python
import functools

import jax
import jax.numpy as jnp
from jax import lax
from jax.experimental import pallas as pl
from jax.experimental.pallas import tpu as pltpu

F32 = jnp.float32
BF16 = jnp.bfloat16

LANES = 128
MXU_W = 256
HEAD_W = 64
MOBA_BLOCK = 256
MOBA_TOPK = 3
MOBA_PAIRS = 4
MLSTM_HEADS = 4
MLSTM_CHUNK = 512
MLSTM_CONV = 4
MLSTM_GATE_CAP = 15.0
SWA_WINDOW = 128
SWA_GROUPS = 2
SWA_GROUP_HEADS = 8
N_EXPERTS = 8
RMS_EPS = 1e-6
NEG_INF = float("-inf")
MASK_NEG = -1e30
LOG2_E = 1.4426950408889634
MOE_TILE = 512
MOE_TF = 256
MOE_WBUF = 4
FFN_TM, FFN_TF = 512, 512
TOP_K = 2
RT_E1, RT_E2, RT_W1, RT_W2, RT_R1, RT_R2 = range(6)
VMEM_LIMIT = 56 * 1024 * 1024

PB_SWA_Q = 0
PB_MOBA_Q = 8
PB_MOBA_K = 12
PB_MOBA_V = 16
PB_ML_Q = 20
PB_ML_K = 24
PB_ML_V = 28
PB_ML_O = 32
PB_ML_G = 36
PB_SWA_K = 37
PB_SWA_V = 38
PROJ_BLOCKS = 39


def _params(sem):
    return pltpu.CompilerParams(dimension_semantics=sem, vmem_limit_bytes=VMEM_LIMIT)


def _rms(x, g):
    return x * lax.rsqrt(jnp.mean(x * x, axis=-1, keepdims=True) + RMS_EPS) * g


def _dot(a, b):
    return jnp.dot(a, b, preferred_element_type=F32)


def _bf16_parts(x):
    hi = x.astype(BF16)
    r1 = x - hi.astype(F32)
    mid = r1.astype(BF16)
    return hi, mid, (r1 - mid.astype(F32)).astype(BF16)


def _dot_nt(a, b):
    return lax.dot_general(a, b, (((1,), (1,)), ((), ())), preferred_element_type=F32)


def _adaln_kernel(c_ref, w_ref, b_ref, o_ref):
    c = c_ref[...]
    cond = (c * jax.nn.sigmoid(c)).astype(BF16)
    o_ref[...] = _dot(cond, w_ref[...].astype(BF16)) + b_ref[...]


def _adaln(c, ada_w, ada_b):
    depth, d, n6 = ada_w.shape
    b = c.shape[0]
    tn = 1024
    return pl.pallas_call(
        _adaln_kernel,
        grid=(depth, n6 // tn),
        in_specs=[
            pl.BlockSpec((b, d), lambda l, j: (0, 0)),
            pl.BlockSpec((None, d, tn), lambda l, j: (l, 0, j)),
            pl.BlockSpec((None, 1, tn), lambda l, j: (l, 0, j)),
        ],
        out_specs=pl.BlockSpec((None, b, tn), lambda l, j: (l, 0, j)),
        out_shape=jax.ShapeDtypeStruct((depth, b, n6), F32),
        compiler_params=_params(("arbitrary", "arbitrary")),
        name="adaln",
    )(c, ada_w, ada_b.reshape(depth, 1, n6))


def _inproj_kernel(*refs, tn, n_cast):
    x_ref, g_ref, sc_ref, sh_ref, w_ref = refs[:5]
    cast_in = refs[5:5 + n_cast]
    o_ref = refs[5 + n_cast]
    cast_out = refs[6 + n_cast:]
    h = (_rms(x_ref[...], g_ref[...]) * (1.0 + sc_ref[...]) + sh_ref[...]).astype(BF16)
    nc = w_ref.shape[1]
    for c0 in range(0, nc, tn):
        c1 = min(c0 + tn, nc)
        o_ref[:, c0:c1] = _dot(h, w_ref[:, c0:c1])
    for src, dst in zip(cast_in, cast_out):
        dst[...] = src[...].astype(BF16)


def _row_cast_job(w3, idx, steps):
    _, rows, cols = w3.shape
    share = 1 if rows % (16 * steps) == 0 else 2
    assert steps % share == 0 and rows % (16 * (steps // share)) == 0, w3.shape
    blk = rows // (steps // share)
    return (w3, pl.BlockSpec((None, blk, cols), lambda i: (idx, i // share, 0)),
            jax.ShapeDtypeStruct((rows, cols), BF16),
            pl.BlockSpec((blk, cols), lambda i: (i // share, 0)))


def _inproj(x2, g, mod, w, seq, cast=()):
    n, d = x2.shape
    nc = w.shape[1]
    tm, tn = 256, 7 * MXU_W
    per_b = seq // tm
    jobs = [_row_cast_job(w3, idx, n // tm) for w3, idx in cast]
    outs = pl.pallas_call(
        functools.partial(_inproj_kernel, tn=tn, n_cast=len(jobs)),
        grid=(n // tm,),
        in_specs=[
            pl.BlockSpec((tm, d), lambda i: (i, 0)),
            pl.BlockSpec((1, d), lambda i: (0, 0)),
            pl.BlockSpec((None, None, 1, d), lambda i: (i // per_b, 1, 0, 0)),
            pl.BlockSpec((None, None, 1, d), lambda i: (i // per_b, 0, 0, 0)),
            pl.BlockSpec((d, nc), lambda i: (0, 0), pipeline_mode=pl.Buffered(1)),
        ] + [job[1] for job in jobs],
        out_specs=[pl.BlockSpec((tm, nc), lambda i: (i, 0))] + [job[3] for job in jobs],
        out_shape=[jax.ShapeDtypeStruct((n, nc), F32)] + [job[2] for job in jobs],
        compiler_params=_params(("arbitrary",)),
        name="inproj",
    )(x2, g, mod, mod, w, *[job[0] for job in jobs])
    return outs[0], tuple(outs[1:])


def _moba_kernel(q_ref, k_ref, v_ref, o_ref, ka_scr, vb_scr):
    seq = k_ref.shape[0]
    blk = MOBA_BLOCK
    n_blk = seq // blk
    lane = lax.broadcasted_iota(jnp.int32, (1, LANES), 1)
    k = k_ref[...]
    vb_scr[...] = v_ref[...].astype(BF16)
    kmean = jnp.mean(k.reshape(n_blk, blk, LANES), axis=1)
    kblk = lax.shift_right_logical(lax.broadcasted_iota(jnp.int32, (seq, 1), 0),
                                   MOBA_BLOCK.bit_length() - 1)
    ri = lax.broadcasted_iota(jnp.int32, (blk, blk), 0)
    ci = lax.broadcasted_iota(jnp.int32, (blk, blk), 1)
    causal_bias = jnp.where(ri >= ci, 0.0, MASK_NEG)
    scale = HEAD_W ** -0.5 * LOG2_E
    blk_id = lax.broadcasted_iota(jnp.int32, (n_blk, 1), 0)
    in_head, kmh = [], []
    for hh in range(2):
        base = HEAD_W * (1 - hh)
        in_head.append((lane >= HEAD_W * hh) & (lane < HEAD_W * (hh + 1)))
        ka_scr[hh] = jnp.where(in_head[hh], k, jnp.where(lane - base == kblk, 1.0, 0.0)).astype(BF16)
        kmh.append(jnp.where(in_head[hh], kmean, 0.0).astype(BF16))

    for i in range(n_blk):
        q = q_ref[i * blk:(i + 1) * blk, :]
        outs = []
        for hh in range(2):
            base = HEAD_W * (1 - hh)
            qa = jnp.where(in_head[hh], q * scale, 0.0)
            if i > MOBA_TOPK:
                qm = jnp.where(in_head[hh], q, 0.0).astype(BF16)
                valid = blk_id < i
                gate = jnp.where(valid, _dot_nt(kmh[hh], qm), NEG_INF)
                beaten_by = jnp.zeros((n_blk, blk), jnp.int32)
                for j in range(i):
                    gj = gate[j:j + 1, :]
                    beats = (gj > gate) | ((gj == gate) & (blk_id > j))
                    beaten_by = beaten_by + beats.astype(jnp.int32)
                drop = jnp.where(valid & (beaten_by >= MOBA_TOPK), MASK_NEG, 0.0)
                rows = [drop, jnp.zeros((LANES - base - n_blk, blk), F32)]
                if base:
                    rows.insert(0, jnp.zeros((base, blk), F32))
                qa = qa + jnp.concatenate(rows, axis=0).T
            s = _dot_nt(qa.astype(BF16), ka_scr[hh, 0:(i + 1) * blk, :])
            s_own = s[:, i * blk:] + causal_bias
            m = jnp.max(s_own, axis=-1, keepdims=True)
            if i:
                s_past = s[:, :i * blk]
                m = jnp.maximum(m, jnp.max(s_past, axis=-1, keepdims=True))
            p_own = jnp.exp2(s_own - m)
            l = jnp.sum(p_own, axis=-1, keepdims=True)
            acc = _dot(p_own.astype(BF16), vb_scr[i * blk:(i + 1) * blk, :])
            if i:
                p_past = jnp.exp2(s_past - m)
                l = l + jnp.sum(p_past, axis=-1, keepdims=True)
                acc = acc + _dot(p_past.astype(BF16), vb_scr[0:i * blk, :])
            outs.append(acc / l)
        o_ref[i * blk:(i + 1) * blk, :] = jnp.where(lane < HEAD_W, outs[0], outs[1]).astype(o_ref.dtype)


def _moba(proj, batch, seq):
    n = proj.shape[0]
    return pl.pallas_call(
        _moba_kernel,
        grid=(batch, MOBA_PAIRS),
        in_specs=[
            pl.BlockSpec((seq, LANES), lambda b, p: (b, PB_MOBA_Q + p)),
            pl.BlockSpec((seq, LANES), lambda b, p: (b, PB_MOBA_K + p)),
            pl.BlockSpec((seq, LANES), lambda b, p: (b, PB_MOBA_V + p)),
        ],
        out_specs=pl.BlockSpec((seq, LANES), lambda b, p: (b, p)),
        out_shape=jax.ShapeDtypeStruct((n, MOBA_PAIRS * LANES), BF16),
        scratch_shapes=[
            pltpu.VMEM((2, seq, LANES), BF16),
            pltpu.VMEM((seq, LANES), BF16),
        ],
        compiler_params=_params(("arbitrary", "arbitrary")),
        name="moba",
    )(proj, proj, proj)


def _causal_conv_silu(x, tail, w, b):
    row = lax.broadcasted_iota(jnp.int32, (8, 1), 0)
    y = b + w[MLSTM_CONV - 1:MLSTM_CONV, :] * x
    for shift in range(1, MLSTM_CONV):
        xr = pltpu.roll(x, shift, axis=0)
        head = jnp.where(row < shift, pltpu.roll(tail, shift, axis=0), xr[0:8, :])
        xs = jnp.concatenate([head, xr[8:, :]], axis=0)
        y = y + w[MLSTM_CONV - 1 - shift:MLSTM_CONV - shift, :] * xs
    return y * jax.nn.sigmoid(y)


def _mlstm_kernel(q_ref, k_ref, v_ref, og_ref, gate_ref, cwq_ref, cwk_ref, cbq_ref, cbk_ref,
                  gb_ref, nw_ref, o_ref, qtail_scr, ktail_scr, c_scr, n_scr, m_scr):
    ts = L = q_ref.shape[0]
    lane = lax.broadcasted_iota(jnp.int32, (1, LANES), 1)
    ri = lax.broadcasted_iota(jnp.int32, (L, L), 0)
    ci = lax.broadcasted_iota(jnp.int32, (L, L), 1)
    causal = ri >= ci
    lower = causal.astype(BF16)
    upper = (ri <= ci).astype(BF16)

    @pl.when(pl.program_id(1) == 0)
    def _():
        qtail_scr[...] = jnp.zeros_like(qtail_scr)
        ktail_scr[...] = jnp.zeros_like(ktail_scr)
        c_scr[...] = jnp.zeros_like(c_scr)
        n_scr[...] = jnp.zeros_like(n_scr)
        m_scr[...] = jnp.zeros_like(m_scr)

    t = MLSTM_GATE_CAP * jnp.tanh((gate_ref[...] + gb_ref[...]) / MLSTM_GATE_CAP)
    a_col = jnp.where(lane < MLSTM_HEADS, t, jax.nn.log_sigmoid(t))
    a_row = a_col.T[0:8, :]

    xq, xk = q_ref[...], k_ref[...]
    qc = _causal_conv_silu(xq, qtail_scr[...], cwq_ref[...], cbq_ref[...]).astype(BF16)
    kc = (_causal_conv_silu(xk, ktail_scr[...], cwk_ref[...], cbk_ref[...])
          * (LANES ** -0.5)).astype(BF16)
    qtail_scr[...] = xq[ts - 8:, :]
    ktail_scr[...] = xk[ts - 8:, :]

    b_c = sum(_dot(lower, part) for part in _bf16_parts(a_col))
    b_r = sum(_dot(part, upper) for part in _bf16_parts(a_row))

    local = []
    for hd in range(MLSTM_HEADS):
        cols = slice(hd * LANES, (hd + 1) * LANES)
        q, k = qc[:, cols], kc[:, cols]
        v = v_ref[:, cols].astype(BF16)
        fl = hd + MLSTM_HEADS
        b_col, li_col = b_c[:, fl:fl + 1], a_col[:, hd:hd + 1]
        b_row, li_row = b_r[fl:fl + 1, :], a_row[hd:hd + 1, :]
        b_last = b_row[:, L - 1:L]
        d_log = jnp.where(causal, b_col - b_row + li_row, NEG_INF)
        d_max = jnp.max(d_log, axis=1, keepdims=True)
        qk = _dot_nt(q, k) * jnp.exp(d_log - d_max)
        pv = _dot(qk.astype(BF16), v)
        qk_sum = jnp.sum(qk, axis=1, keepdims=True)
        a_max = jnp.max(b_last - b_row + li_row, axis=1, keepdims=True)
        kw = k.astype(F32) * jnp.exp(b_last - b_col + li_col - a_max)
        c_in = _dot(kw.T.astype(BF16), v)
        n_in = jnp.sum(kw, axis=0, keepdims=True)
        local.append((q, b_col, b_last, d_max, pv, qk_sum, a_max, c_in, n_in))

    for hd in range(MLSTM_HEADS):
        cols = slice(hd * LANES, (hd + 1) * LANES)
        q, b_col, b_last, d_max, pv, qk_sum, a_max, c_in, n_in = local[hd]
        c_st, n_st, m_st = c_scr[hd], n_scr[hd], m_scr[hd][:, 0:1]
        inter_log = b_col + m_st
        m_out = jnp.maximum(inter_log, d_max)
        w_inter = jnp.exp(inter_log - m_out)
        w_local = jnp.exp(d_max - m_out)
        num = w_local * pv + w_inter * _dot(q, c_st.astype(BF16))
        den = w_local * qk_sum + w_inter * jnp.sum(q.astype(F32) * n_st, axis=1, keepdims=True)
        hv = num / jnp.maximum(jnp.abs(den), jnp.exp(-m_out))
        hn = _rms(hv, nw_ref[:, cols])
        o_ref[:, cols] = (jax.nn.sigmoid(og_ref[:, cols]) * hn).astype(o_ref.dtype)
        m_new = jnp.maximum(b_last + m_st, a_max)
        decay = jnp.exp(b_last + m_st - m_new)
        inject = jnp.exp(a_max - m_new)
        c_scr[hd] = decay * c_st + inject * c_in
        n_scr[hd] = decay * n_st + inject * n_in
        m_scr[hd] = jnp.broadcast_to(m_new, (1, LANES))


def _mlstm(proj, conv_w, conv_b, gate_bias, norm_w, batch, seq):
    n = proj.shape[0]
    hw = MLSTM_HEADS
    gw = hw * LANES
    ts = MLSTM_CHUNK
    tiles = seq // ts

    def col(base):
        return pl.BlockSpec((ts, gw), lambda b, t: (b * tiles + t, base // hw))

    def vec(rows, blk):
        return pl.BlockSpec((rows, gw), lambda b, t: (0, blk))

    return pl.pallas_call(
        _mlstm_kernel,
        grid=(batch, tiles),
        in_specs=[
            col(PB_ML_Q), col(PB_ML_K), col(PB_ML_V), col(PB_ML_O),
            pl.BlockSpec((ts, LANES), lambda b, t: (b * tiles + t, PB_ML_G)),
            vec(MLSTM_CONV, 0), vec(MLSTM_CONV, 1), vec(1, 0), vec(1, 1),
            pl.BlockSpec((1, LANES), lambda b, t: (0, 0)),
            vec(1, 0),
        ],
        out_specs=pl.BlockSpec((ts, gw), lambda b, t: (b * tiles + t, 0)),
        out_shape=jax.ShapeDtypeStruct((n, gw), BF16),
        scratch_shapes=[
            pltpu.VMEM((8, gw), F32),
            pltpu.VMEM((8, gw), F32),
            pltpu.VMEM((hw, LANES, LANES), F32),
            pltpu.VMEM((hw, 1, LANES), F32),
            pltpu.VMEM((hw, 1, LANES), F32),
        ],
        compiler_params=_params(("arbitrary", "arbitrary")),
        name="mlstm",
    )(proj, proj, proj, proj, proj, conv_w, conv_w, conv_b, conv_b, gate_bias, norm_w)


def _swa_kernel(sink_ref, q_ref, k_ref, v_ref, o_ref, kb_scr, vb_scr):
    g = pl.program_id(1)
    seq = q_ref.shape[0]
    W = SWA_WINDOW
    n_pairs = SWA_GROUP_HEADS // 2
    lane = lax.broadcasted_iota(jnp.int32, (1, LANES), 1)
    low = lane < HEAD_W
    keep = low == (g == 0)
    for src, dst in ((k_ref, kb_scr), (v_ref, vb_scr)):
        both = src[...]
        dst[...] = jnp.where(keep, both, pltpu.roll(both, HEAD_W, axis=1)).astype(BF16)
    scale = HEAD_W ** -0.5 * LOG2_E
    sinks = [sink_ref[g * SWA_GROUP_HEADS + hd] * LOG2_E for hd in range(SWA_GROUP_HEADS)]
    r = lax.broadcasted_iota(jnp.int32, (W, 2 * W), 0)
    c = lax.broadcasted_iota(jnp.int32, (W, 2 * W), 1)
    bias_first = jnp.where(c <= r, 0.0, MASK_NEG)
    bias_rest = jnp.where((c > r) & (c <= r + W), 0.0, MASK_NEG)

    def block(nb, carry):
        start = pl.multiple_of(jnp.maximum(nb - 1, 0) * W, W)
        r0 = pl.multiple_of(nb * W, W)
        qn = q_ref[pl.ds(r0, W), :] * scale
        kband = kb_scr[pl.ds(start, 2 * W), :]
        vband = vb_scr[pl.ds(start, 2 * W), :]
        parts = []
        for p in range(n_pairs):
            qp = qn[:, p * LANES:(p + 1) * LANES]
            parts.append(jnp.where(low, qp, 0.0).astype(BF16))
            parts.append(jnp.where(low, 0.0, qp).astype(BF16))
        s_all = _dot_nt(jnp.concatenate(parts, axis=0), kband)
        bias = jnp.where(nb == 0, bias_first, bias_rest)
        probs, inv_l = [], []
        for hd in range(SWA_GROUP_HEADS):
            s = s_all[hd * W:(hd + 1) * W, :] + bias
            m = jnp.maximum(jnp.max(s, axis=-1, keepdims=True), sinks[hd])
            e = jnp.exp2(s - m)
            inv_l.append(1.0 / (jnp.sum(e, axis=-1, keepdims=True) + jnp.exp2(sinks[hd] - m)))
            probs.append(e.astype(BF16))
        o_all = _dot(jnp.concatenate(probs, axis=0), vband)
        outs = []
        for p in range(n_pairs):
            lo = o_all[(2 * p) * W:(2 * p + 1) * W, :] * inv_l[2 * p]
            hi = o_all[(2 * p + 1) * W:(2 * p + 2) * W, :] * inv_l[2 * p + 1]
            outs.append(jnp.where(low, lo, hi))
        o_ref[pl.ds(r0, W), :] = jnp.concatenate(outs, axis=1).astype(o_ref.dtype)
        return carry

    lax.fori_loop(0, seq // W, block, 0, unroll=2)


def _swa(proj, sinks, batch, seq):
    n = proj.shape[0]
    gw = SWA_GROUP_HEADS * HEAD_W
    gb = gw // LANES
    return pl.pallas_call(
        _swa_kernel,
        grid=(batch, SWA_GROUPS),
        in_specs=[
            pl.BlockSpec(memory_space=pltpu.SMEM),
            pl.BlockSpec((seq, gw), lambda b, g: (b, PB_SWA_Q // gb + g)),
            pl.BlockSpec((seq, LANES), lambda b, g: (b, PB_SWA_K)),
            pl.BlockSpec((seq, LANES), lambda b, g: (b, PB_SWA_V)),
        ],
        out_specs=pl.BlockSpec((seq, gw), lambda b, g: (b, g)),
        out_shape=jax.ShapeDtypeStruct((n, SWA_GROUPS * gw), BF16),
        scratch_shapes=[pltpu.VMEM((seq, LANES), BF16), pltpu.VMEM((seq, LANES), BF16)],
        compiler_params=_params(("arbitrary", "arbitrary")),
        name="swa",
    )(sinks, proj, proj, proj)


def _outproj_kernel(*refs, with_router, sub):
    if with_router:
        (ya_ref, yb_ref, yc_ref, w_ref, x_ref, gpost_ref, gt_ref, gpre_ref, sc_ref, sh_ref,
         wr_ref, xo_ref, h_ref, route_ref, count_ref, count_scr) = refs
    else:
        (ya_ref, yb_ref, yc_ref, w_ref, x_ref, gpost_ref, gt_ref, gpre_ref, sc_ref, sh_ref,
         xo_ref, h_ref) = refs
    wa = ya_ref.shape[1]
    wb = yb_ref.shape[1]
    if with_router:
        @pl.when(pl.program_id(0) == 0)
        def _():
            count_scr[...] = jnp.zeros_like(count_scr)

    for r0 in range(0, x_ref.shape[0], sub):
        rows = slice(r0, r0 + sub)
        y = _dot(ya_ref[rows, :], w_ref[0:wa, :])
        y = y + _dot(yb_ref[rows, :], w_ref[wa:wa + wb, :])
        y = y + _dot(yc_ref[rows, :], w_ref[wa + wb:, :])
        xn = x_ref[rows, :] + gt_ref[...] * _rms(y, gpost_ref[...])
        xo_ref[rows, :] = xn
        hb = (_rms(xn, gpre_ref[...]) * (1.0 + sc_ref[...]) + sh_ref[...]).astype(BF16)
        h_ref[rows, :] = hb.astype(h_ref.dtype)
        if not with_router:
            continue
        tm = sub
        lane = lax.broadcasted_iota(jnp.int32, (1, LANES), 1)
        logits = jnp.where(lane < N_EXPERTS, _dot(hb, wr_ref[...]), NEG_INF)
        m1 = jnp.max(logits, axis=-1, keepdims=True)
        i1 = jnp.min(jnp.where(logits == m1, lane, LANES), axis=-1, keepdims=True)
        rest = jnp.where(lane == i1, NEG_INF, logits)
        m2 = jnp.max(rest, axis=-1, keepdims=True)
        i2 = jnp.min(jnp.where(rest == m2, lane, LANES), axis=-1, keepdims=True)
        e2 = jnp.exp(m2 - m1)
        w1 = 1.0 / (1.0 + e2)
        picked = jnp.where((lane == i1) | (lane == i2), 1.0, 0.0)
        ri = lax.broadcasted_iota(jnp.int32, (tm, tm), 0)
        ci = lax.broadcasted_iota(jnp.int32, (tm, tm), 1)
        before = jnp.where(ri > ci, 1.0, 0.0).astype(BF16)
        rank = _dot(before, picked.astype(BF16)) + count_scr[...]
        r1 = jnp.sum(jnp.where(lane == i1, rank, 0.0), axis=-1, keepdims=True)
        r2 = jnp.sum(jnp.where(lane == i2, rank, 0.0), axis=-1, keepdims=True)
        count_scr[...] += jnp.sum(picked, axis=0, keepdims=True)
        rec = jnp.where(lane == RT_E1, i1.astype(F32), 0.0)
        for slot_lane, val in ((RT_E2, i2.astype(F32)), (RT_W1, w1), (RT_W2, e2 * w1),
                               (RT_R1, r1), (RT_R2, r2)):
            rec = jnp.where(lane == slot_lane, val, rec)
        route_ref[rows, :] = rec
    if with_router:
        count_ref[...] = jnp.broadcast_to(count_scr[...], count_ref.shape)


def _outproj(ya, yb, yc, w, x2, gpost, gpre, mod, seq, w_router=None):
    n, d = x2.shape
    tm, sub = 512, 256
    per_b = seq // tm
    with_router = w_router is not None

    def rows(width):
        return pl.BlockSpec((tm, width), lambda i: (i, 0))

    def vec():
        return pl.BlockSpec((1, d), lambda i: (0, 0))

    def modrow(k):
        return pl.BlockSpec((None, None, 1, d), lambda i: (i // per_b, k, 0, 0))

    in_specs = [rows(ya.shape[1]), rows(yb.shape[1]), rows(yc.shape[1]),
                pl.BlockSpec((d, d), lambda i: (0, 0), pipeline_mode=pl.Buffered(1)), rows(d),
                vec(), modrow(2), vec(), modrow(4), modrow(3)]
    args = [ya, yb, yc, w, x2, gpost, mod, gpre, mod, mod]
    out_specs = [rows(d), rows(d)]
    out_shape = [jax.ShapeDtypeStruct((n, d), F32),
                 jax.ShapeDtypeStruct((n, d), F32 if with_router else BF16)]
    scratch = []
    if with_router:
        in_specs.append(pl.BlockSpec((d, LANES), lambda i: (0, 0)))
        args.append(w_router)
        out_specs += [rows(LANES), pl.BlockSpec((8, LANES), lambda i: (0, 0))]
        out_shape += [jax.ShapeDtypeStruct((n, LANES), F32), jax.ShapeDtypeStruct((8, LANES), F32)]
        scratch.append(pltpu.VMEM((1, LANES), F32))
    return pl.pallas_call(
        functools.partial(_outproj_kernel, with_router=with_router, sub=sub),
        grid=(n // tm,),
        in_specs=in_specs,
        out_specs=out_specs,
        out_shape=out_shape,
        scratch_shapes=scratch,
        compiler_params=_params(("arbitrary",)),
        name="outproj",
    )(*args)


def _swiglu_step(h, wg_ref, wu_ref, wd_ref):
    gate = _dot(h, wg_ref[...])
    a = gate * jax.nn.sigmoid(gate) * _dot(h, wu_ref[...])
    return _dot(a.astype(BF16), wd_ref[...])


def _ffn_kernel(*refs, n_cast):
    h_ref, wg_ref, wu_ref, wd_ref, x_ref, gpost_ref, gt_ref = refs[:7]
    cast_in = refs[7:7 + n_cast]
    o_ref = refs[7 + n_cast]
    cast_out = refs[8 + n_cast:]
    f = pl.program_id(1)

    @pl.when(f == 0)
    def _():
        o_ref[...] = jnp.zeros_like(o_ref)

    o_ref[...] += _swiglu_step(h_ref[...], wg_ref, wu_ref, wd_ref)

    @pl.when(f == pl.num_programs(1) - 1)
    def _():
        o_ref[...] = x_ref[...] + gt_ref[...] * _rms(o_ref[...], gpost_ref[...])

    for src, dst in zip(cast_in, cast_out):
        dst[...] = src[...].astype(BF16)


def _expert_cast_jobs(w_gate, w_up, w_down, gi, gf):
    n_e, d, dff = w_gate.shape
    rows = n_e * d // gi
    per_e = d // rows
    assert rows * gi == n_e * d and per_e * rows == d and rows % 16 == 0
    assert dff == gf * MOE_TF
    drows = n_e * dff // (gi * gf)
    assert drows * gi * gf == n_e * dff and drows % 16 == 0
    gu_in = pl.BlockSpec((rows, MOE_TF), lambda i, f: (i, f))
    gu_out = pl.BlockSpec((None, rows, MOE_TF), lambda i, f: (i // per_e * gf + f, i % per_e, 0))
    gu_shape = jax.ShapeDtypeStruct((n_e * gf, d, MOE_TF), BF16)
    dn_spec = pl.BlockSpec((drows, d), lambda i, f: (i * gf + f, 0))
    dn_shape = jax.ShapeDtypeStruct((n_e * dff, d), BF16)
    return [(w_gate.reshape(n_e * d, dff), gu_in, gu_shape, gu_out),
            (w_up.reshape(n_e * d, dff), gu_in, gu_shape, gu_out),
            (w_down.reshape(n_e * dff, d), dn_spec, dn_shape, dn_spec)]


def _ffn(h, wg, wu, wd, x2, gpost, mod, seq, cast_jobs_fn=None):
    n, d = x2.shape
    tm, tf = FFN_TM, FFN_TF
    per_b = seq // tm
    gi, gf = n // tm, wg.shape[1] // tf
    row = lambda i, f: (i, 0)
    jobs = cast_jobs_fn(gi, gf) if cast_jobs_fn else []
    outs = pl.pallas_call(
        functools.partial(_ffn_kernel, n_cast=len(jobs)),
        grid=(gi, gf),
        in_specs=[
            pl.BlockSpec((tm, d), row),
            pl.BlockSpec((d, tf), lambda i, f: (0, f)),
            pl.BlockSpec((d, tf), lambda i, f: (0, f)),
            pl.BlockSpec((tf, d), lambda i, f: (f, 0)),
            pl.BlockSpec((tm, d), row),
            pl.BlockSpec((1, d), lambda i, f: (0, 0)),
            pl.BlockSpec((None, None, 1, d), lambda i, f: (i // per_b, 5, 0, 0)),
        ] + [job[1] for job in jobs],
        out_specs=[pl.BlockSpec((tm, d), row)] + [job[3] for job in jobs],
        out_shape=[jax.ShapeDtypeStruct((n, d), F32)] + [job[2] for job in jobs],
        compiler_params=_params(("arbitrary", "arbitrary")),
        name="ffn",
    )(h, wg, wu, wd, x2, gpost, mod, *[job[0] for job in jobs])
    return outs[0], tuple(outs[1:])


def _dispatch_kernel(p1_ref, p2_ref, fill_ref, h_ref, xs_ref, zero_scr, sems, fill_sem, *, n_rows):
    tm = h_ref.shape[0]
    base = pl.program_id(0) * tm

    @pl.when(pl.program_id(0) == 0)
    def _():
        zero_scr[...] = jnp.zeros_like(zero_scr)
        tail = [pltpu.make_async_copy(zero_scr.at[pl.ds(0, MOE_TILE)],
                                      xs_ref.at[pl.ds(t0, MOE_TILE)], fill_sem)
                for t0 in range(n_rows, xs_ref.shape[0], MOE_TILE)]
        for cp in tail:
            cp.start()
        for cp in tail:
            cp.wait()
        for e in range(N_EXPERTS):
            cp = pltpu.make_async_copy(
                zero_scr, xs_ref.at[pl.ds(pl.multiple_of(fill_ref[e], 8), zero_scr.shape[0])],
                fill_sem)
            cp.start()
            cp.wait()

    def issue(r, carry):
        row = h_ref.at[pl.ds(r, 1)]
        pltpu.make_async_copy(row, xs_ref.at[pl.ds(p1_ref[base + r], 1)], sems.at[0]).start()
        pltpu.make_async_copy(row, xs_ref.at[pl.ds(p2_ref[base + r], 1)],
                              sems.at[1]).start(priority=1)
        return carry

    lax.fori_loop(0, tm, issue, 0, unroll=8)
    for k in range(TOP_K):
        pltpu.make_async_copy(h_ref, xs_ref.at[pl.ds(0, tm)], sems.at[k]).wait()


def _dispatch(h, p1, p2, fill_start, n_slots):
    n, d = h.shape
    tm = 512
    return pl.pallas_call(
        functools.partial(_dispatch_kernel, n_rows=TOP_K * n),
        grid_spec=pltpu.PrefetchScalarGridSpec(
            num_scalar_prefetch=3,
            grid=(n // tm,),
            in_specs=[pl.BlockSpec((tm, d), lambda i, p1, p2, fs: (i, 0))],
            out_specs=pl.BlockSpec(memory_space=pl.ANY),
            scratch_shapes=[pltpu.VMEM((MOE_TILE + 8, d), F32),
                            pltpu.SemaphoreType.DMA((TOP_K,)),
                            pltpu.SemaphoreType.DMA(())],
        ),
        out_shape=jax.ShapeDtypeStruct((n_slots, d), F32),
        compiler_params=_params(("arbitrary",)),
        name="moe_dispatch",
    )(p1, p2, fill_start, h)


def _moe_ffn_kernel(te_ref, nv_ref, x_ref, wg_hbm, wu_hbm, wd_hbm, y_ref, h_scr,
                    wg_buf, wu_buf, wd_buf, sems, *, tf):
    j = pl.program_id(0)
    n_used = nv_ref[0]
    nf = wd_hbm.shape[1] // tf

    def slices(tile, f, slot):
        e = te_ref[tile]
        c0 = pl.multiple_of(f * tf, tf)
        return (
            pltpu.make_async_copy(wg_hbm.at[e * nf + f], wg_buf.at[slot], sems.at[0, slot]),
            pltpu.make_async_copy(wu_hbm.at[e * nf + f], wu_buf.at[slot], sems.at[1, slot]),
            pltpu.make_async_copy(wd_hbm.at[e, pl.ds(c0, tf), :], wd_buf.at[slot], sems.at[2, slot]),
        )

    y_ref[...] = jnp.zeros_like(y_ref)

    ahead = MOE_WBUF - 1

    @pl.when(j < n_used)
    def _():
        @pl.when(j == 0)
        def _():
            for f0 in range(ahead):
                for cp in slices(0, f0, f0):
                    cp.start()

        h_scr[...] = x_ref[...].astype(BF16)

        def step(f, carry):
            count = j * nf + f
            slot = lax.rem(count, MOE_WBUF)
            for cp in slices(j, f, slot):
                cp.wait()

            wrap = f + ahead >= nf
            nxt_tile = jnp.where(wrap, j + 1, j)
            nxt_f = jnp.where(wrap, f + ahead - nf, f + ahead)

            @pl.when(nxt_tile < n_used)
            def _():
                for cp in slices(nxt_tile, nxt_f, lax.rem(count + ahead, MOE_WBUF)):
                    cp.start()

            y_ref[...] += _swiglu_step(h_scr[...], wg_buf.at[slot], wu_buf.at[slot],
                                       wd_buf.at[slot])
            return carry

        lax.fori_loop(0, nf, step, 0)


def _moe_ffn(xs, tile_expert, n_valid, wg, wu, wd):
    n_slots, d = xs.shape
    tm, tf = MOE_TILE, MOE_TF

    return pl.pallas_call(
        functools.partial(_moe_ffn_kernel, tf=tf),
        grid_spec=pltpu.PrefetchScalarGridSpec(
            num_scalar_prefetch=2,
            grid=(n_slots // tm,),
            in_specs=[
                pl.BlockSpec((tm, d), lambda j, te, nv: (jnp.minimum(j, nv[0] - 1), 0)),
                pl.BlockSpec(memory_space=pl.ANY),
                pl.BlockSpec(memory_space=pl.ANY),
                pl.BlockSpec(memory_space=pl.ANY),
            ],
            out_specs=pl.BlockSpec((tm, d), lambda j, te, nv: (j, 0)),
            scratch_shapes=[
                pltpu.VMEM((tm, d), BF16),
                pltpu.VMEM((MOE_WBUF, d, tf), BF16),
                pltpu.VMEM((MOE_WBUF, d, tf), BF16),
                pltpu.VMEM((MOE_WBUF, tf, d), BF16),
                pltpu.SemaphoreType.DMA((3, MOE_WBUF)),
            ],
        ),
        out_shape=jax.ShapeDtypeStruct((n_slots, d), F32),
        compiler_params=_params(("arbitrary",)),
        name="moe_ffn",
    )(tile_expert, n_valid, xs, wg, wu, wd)


def _combine_kernel(p1_ref, p2_ref, ys_ref, route_ref, x_ref, gpost_ref, gt_ref, o_ref,
                    y1_scr, y2_scr, sems):
    tm = x_ref.shape[0]
    i = pl.program_id(0)
    slot = lax.rem(i, 2)

    def gather(tile, to_slot):
        base = tile * tm

        def issue(r, carry):
            pltpu.make_async_copy(ys_ref.at[pl.ds(p1_ref[base + r], 1)],
                                  y1_scr.at[to_slot, pl.ds(r, 1)], sems.at[0, to_slot]).start()
            pltpu.make_async_copy(ys_ref.at[pl.ds(p2_ref[base + r], 1)],
                                  y2_scr.at[to_slot, pl.ds(r, 1)],
                                  sems.at[1, to_slot]).start(priority=1)
            return carry

        lax.fori_loop(0, tm, issue, 0, unroll=8)

    @pl.when(i == 0)
    def _():
        gather(0, 0)

    @pl.when(i + 1 < pl.num_programs(0))
    def _():
        gather(i + 1, 1 - slot)

    pltpu.make_async_copy(ys_ref.at[pl.ds(0, tm)], y1_scr.at[slot], sems.at[0, slot]).wait()
    pltpu.make_async_copy(ys_ref.at[pl.ds(0, tm)], y2_scr.at[slot], sems.at[1, slot]).wait()
    lane = lax.broadcasted_iota(jnp.int32, (1, LANES), 1)
    route = route_ref[...]
    w1 = jnp.sum(jnp.where(lane == RT_W1, route, 0.0), axis=-1, keepdims=True)
    w2 = jnp.sum(jnp.where(lane == RT_W2, route, 0.0), axis=-1, keepdims=True)
    y = w1 * y1_scr[slot] + w2 * y2_scr[slot]
    o_ref[...] = x_ref[...] + gt_ref[...] * _rms(y, gpost_ref[...])


def _combine(ys, p1, p2, route, x2, gpost, mod, seq):
    n, d = x2.shape
    tm = 512
    per_b = seq // tm
    row = lambda i, p1, p2: (i, 0)
    return pl.pallas_call(
        _combine_kernel,
        grid_spec=pltpu.PrefetchScalarGridSpec(
            num_scalar_prefetch=2,
            grid=(n // tm,),
            in_specs=[
                pl.BlockSpec(memory_space=pl.ANY),
                pl.BlockSpec((tm, LANES), row),
                pl.BlockSpec((tm, d), row),
                pl.BlockSpec((1, d), lambda i, p1, p2: (0, 0)),
                pl.BlockSpec((None, None, 1, d), lambda i, p1, p2: (i // per_b, 5, 0, 0)),
            ],
            out_specs=pl.BlockSpec((tm, d), row),
            scratch_shapes=[pltpu.VMEM((2, tm, d), F32), pltpu.VMEM((2, tm, d), F32),
                            pltpu.SemaphoreType.DMA((TOP_K, 2))],
        ),
        out_shape=jax.ShapeDtypeStruct((n, d), F32),
        compiler_params=_params(("arbitrary",)),
        name="moe_combine",
    )(p1, p2, ys, route, x2, gpost, mod)


def _route_plan(route, counts, n_tiles):
    e1 = route[:, RT_E1].astype(jnp.int32)
    e2 = route[:, RT_E2].astype(jnp.int32)
    cnt = counts[0, :N_EXPERTS].astype(jnp.int32)
    size = (cnt + MOE_TILE - 1) // MOE_TILE * MOE_TILE
    end = jnp.cumsum(size)
    start = end - size
    p1 = start[e1] + route[:, RT_R1].astype(jnp.int32)
    p2 = start[e2] + route[:, RT_R2].astype(jnp.int32)
    n_valid = end[-1] // MOE_TILE
    tile_start = jnp.minimum(jnp.arange(n_tiles), n_valid - 1) * MOE_TILE
    tile_expert = jnp.sum(tile_start[:, None] >= end[None, :], axis=1).astype(jnp.int32)
    fill_start = (start + cnt) // 8 * 8
    return p1, p2, fill_start, tile_expert, n_valid.reshape(1).astype(jnp.int32)


def _pack_kernel(w_ref, o_ref):
    gw = 4 * LANES
    n_gate = 2 * MLSTM_HEADS
    src_gate = 7 * gw
    src_swq = src_gate + n_gate
    w_swq = 2 * SWA_GROUP_HEADS * HEAD_W
    lane = lax.broadcasted_iota(jnp.int32, (1, LANES), 1)
    o_ref[:, PB_SWA_Q * LANES:PB_SWA_Q * LANES + w_swq] = (
        w_ref[:, src_swq:src_swq + w_swq].astype(BF16))
    o_ref[:, PB_MOBA_Q * LANES:PB_ML_G * LANES] = w_ref[:, 0:src_gate].astype(BF16)
    gates = w_ref[:, src_gate:src_gate + LANES]
    o_ref[:, PB_ML_G * LANES:(PB_ML_G + 1) * LANES] = jnp.where(lane < n_gate, gates, 0.0).astype(BF16)
    src_kv = src_swq + w_swq
    o_ref[:, PB_SWA_K * LANES:(PB_SWA_V + 1) * LANES] = (
        w_ref[:, src_kv:src_kv + 2 * LANES].astype(BF16))


def _pack_w_in(w_in, l):
    _, d, n_in = w_in.shape
    tr = 256
    return pl.pallas_call(
        _pack_kernel,
        grid=(d // tr,),
        in_specs=[pl.BlockSpec((None, tr, n_in), lambda i: (l, i, 0))],
        out_specs=pl.BlockSpec((tr, PROJ_BLOCKS * LANES), lambda i: (i, 0)),
        out_shape=jax.ShapeDtypeStruct((d, PROJ_BLOCKS * LANES), BF16),
        compiler_params=_params(("arbitrary",)),
        name="pack_w_in",
    )(w_in.astype(BF16))


def kernel(x, c, ada_w, ada_b, g_pre_mix, g_post_mix, g_pre_ffn, g_post_ffn, w_in, w_out, conv_w,
           conv_b, igate_b, fgate_b, mlstm_norm_w, swa_sinks, ffn_w_gate, ffn_w_up, ffn_w_down,
           moe_router, moe_w_gate, moe_w_up, moe_w_down):
    batch, seq, d = x.shape
    depth = ada_w.shape[0]
    n = batch * seq
    x2 = x.reshape(n, d)
    mod_all = _adaln(c, ada_w, ada_b).reshape(depth, batch, 6, 1, d)
    for l in range(depth):
        mod = mod_all[l]
        j = l // 2
        cast = [(w_out, l)]
        if l % 2 == 0:
            cast += [(ffn_w_gate, j), (ffn_w_up, j), (ffn_w_down, j)]
        proj, casted = _inproj(x2, g_pre_mix[l].reshape(1, d), mod, _pack_w_in(w_in, l), seq, cast)
        y_moba = _moba(proj, batch, seq)
        gate_bias = jnp.concatenate(
            [igate_b[l], fgate_b[l], jnp.zeros((LANES - 2 * MLSTM_HEADS,), F32)]).reshape(1, LANES)
        y_mlstm = _mlstm(proj, conv_w[l], conv_b[l].reshape(1, -1), gate_bias,
                         mlstm_norm_w[l].reshape(1, -1), batch, seq)
        y_swa = _swa(proj, swa_sinks[l], batch, seq)
        w_router = None
        if l % 2 == 1:
            w_router = jnp.pad(moe_router[j], ((0, 0), (0, LANES - N_EXPERTS))).astype(BF16)
        outs = _outproj(y_moba, y_mlstm, y_swa, casted[0], x2,
                        g_post_mix[l].reshape(1, d), g_pre_ffn[l].reshape(1, d), mod, seq, w_router)
        gpost = g_post_ffn[l].reshape(1, d)
        if l % 2 == 0:
            x2, h = outs
            jobs_fn = None
            if l + 1 < depth:
                jn = (l + 1) // 2
                jobs_fn = functools.partial(_expert_cast_jobs, moe_w_gate[jn], moe_w_up[jn],
                                            moe_w_down[jn])
            x2, moe_bf16 = _ffn(h, casted[1], casted[2], casted[3], x2, gpost, mod, seq, jobs_fn)
        else:
            x2, h, route, counts = outs
            n_tiles = (TOP_K * n) // MOE_TILE + N_EXPERTS + 2
            p1, p2, fill_start, tile_expert, n_valid = _route_plan(route, counts, n_tiles)
            xs = _dispatch(h, p1, p2, fill_start, n_tiles * MOE_TILE)
            wg, wu, wd = moe_bf16
            ys = _moe_ffn(xs, tile_expert, n_valid, wg, wu, wd.reshape(N_EXPERTS, -1, d))
            x2 = _combine(ys, p1, p2, route, x2, gpost, mod, seq)
    return x2.reshape(batch, seq, d)
```
